```python
import math
import jax, jax.numpy as jnp
from jax import lax
import numpy as np

D_MODEL = 2048
BATCH = 4
SEQ = 2048
DEPTH = 1
DEC_BATCH = 16
DEC_SEQ = 32
PAST_LEN = 2048

CHUNK = 64
N_META = 16
RMS_EPS = 1e-5
A_DH = 64
A_HEADS = D_MODEL // (2 * A_DH)
A_VDIM = 2 * A_DH
B_HEAD = 64
B_HEADS = D_MODEL // B_HEAD
DECAY_LORA = 96
AAA_LORA = 96
GATE_LORA = 256
GN_EPS = 64e-5
N_BUCKETS = 32
MAX_DISTANCE = 128
N_EXPERTS = 32
TOP_K = 4
D_FF = D_MODEL
SWIGLU_ALPHA = 1.702
SWIGLU_LIMIT = 7.0
MOE_BLOCK = 128
Q_BLOCK = 128
QK_W = A_HEADS * 2 * A_DH
AV_W = A_HEADS * A_VDIM
SHIFT_W = 3 * D_MODEL + DECAY_LORA + AAA_LORA + GATE_LORA
GATE_W = 2 * D_MODEL
IN_W = 2 * QK_W + AV_W + SHIFT_W + GATE_W

kernel_name = 'streaming_diffattn_rwkv7_moe_hybrid'


def rmsnorm(x, g):
    xf = x.astype(jnp.float32)
    y = xf * lax.rsqrt(jnp.mean(xf * xf, axis=-1, keepdims=True) + RMS_EPS)
    return (y * g.astype(jnp.float32)).astype(x.dtype)


def chunk_id(pos):
    return jnp.where(pos < N_META, -1, (pos - N_META) // CHUNK)


def t5_bucket(rel):
    nb = N_BUCKETS // 2
    ret = jnp.where(rel > 0, nb, 0)
    n = jnp.abs(rel)
    max_exact = nb // 2
    nf = jnp.maximum(n, 1).astype(jnp.float32)
    large = max_exact + (jnp.log(nf / max_exact) / math.log(MAX_DISTANCE / max_exact) * (nb - max_exact)).astype(jnp.int32)
    large = jnp.minimum(large, nb - 1)
    return ret + jnp.where(n < max_exact, n, large)


def rel_bias(q_pos, k_pos, table):
    bucket = t5_bucket(k_pos[None, :] - q_pos[:, None])
    return jnp.transpose(table.astype(jnp.float32)[bucket], (2, 0, 1))


def diff_attn_core(q, k, v, bias, mask, lam):
    q2 = q.reshape(q.shape[:-1] + (2, A_DH))
    k2 = k.reshape(k.shape[:-1] + (2, A_DH))
    s = jnp.einsum('bqhcd,bkhcd->bchqk', q2, k2, preferred_element_type=jnp.float32) * (A_DH ** -0.5) + bias
    if mask is not None:
        s = jnp.where(mask, s, -jnp.inf)
    p = jax.nn.softmax(s, axis=-1)
    a = p[:, 0] - lam * p[:, 1]
    return jnp.einsum('bhqk,bkhd->bqhd', a.astype(v.dtype), v)


def prompt_diff_attn(q, k, v, table, lam):
    b, t = q.shape[:2]
    n_blk = -(-t // Q_BLOCK)
    tq = n_blk * Q_BLOCK
    qp = jnp.pad(q, ((0, 0), (0, tq - t), (0, 0), (0, 0)))
    qb = qp.reshape(b, n_blk, Q_BLOCK, A_HEADS, 2 * A_DH).transpose(1, 0, 2, 3, 4)
    k_pos = jnp.arange(t, dtype=jnp.int32)
    k_cid = chunk_id(k_pos)

    def one_block(args):
        q_blk, i = args
        q_pos = i * Q_BLOCK + jnp.arange(Q_BLOCK, dtype=jnp.int32)
        mask = k_cid[None, :] <= chunk_id(q_pos)[:, None]
        return diff_attn_core(q_blk, k, v, rel_bias(q_pos, k_pos, table), mask, lam)

    ob = lax.map(one_block, (qb, jnp.arange(n_blk, dtype=jnp.int32)))
    return ob.transpose(1, 0, 2, 3, 4).reshape(b, tq, A_HEADS, A_VDIM)[:, :t]


def sample_diff_attn(q, k_all, v_all, table, lam):
    s = q.shape[1]
    kl = k_all.shape[1]
    q_pos = (kl - s) + jnp.arange(s, dtype=jnp.int32)
    k_pos = jnp.arange(kl, dtype=jnp.int32)
    return diff_attn_core(q, k_all, v_all, rel_bias(q_pos, k_pos, table), None, lam)


def token_shift(p, prev, mu):
    p_prev = jnp.concatenate([prev[:, None].astype(p.dtype), p[:, :-1]], axis=1)
    return p + (p_prev - p) * mu


def rwkv7_branch(xs, s0, decay_base, w_decay_up, a_base, w_a_up, w_g_up, k_k, k_a, r_k, gn_w, gn_b):
    b, t, _ = xs.shape
    f32 = jnp.float32
    d = D_MODEL
    xr, xk, xv, xw, xa, xg = jnp.split(xs, [d, 2 * d, 3 * d, 3 * d + DECAY_LORA, 3 * d + DECAY_LORA + AAA_LORA], axis=-1)
    w_log = -jax.nn.softplus(-(decay_base + jnp.tanh(xw) @ w_decay_up).astype(f32)) - 0.5
    decay = jnp.exp(-jnp.exp(w_log))
    a = jax.nn.sigmoid((a_base + xa @ w_a_up).astype(f32))
    g = jax.nn.sigmoid(xg) @ w_g_up
    kk = (xk * k_k).astype(f32).reshape(b, t, B_HEADS, B_HEAD)
    kk = kk / jnp.maximum(jnp.sqrt(jnp.sum(kk * kk, axis=-1, keepdims=True)), 1e-12)
    k = xk.astype(f32) * (1.0 + (a - 1.0) * k_a.astype(f32))

    def heads(z):
        return z.astype(f32).reshape(b, t, B_HEADS, B_HEAD)

    rh, wh, kh, vh, ah = heads(xr), heads(decay), heads(k), heads(xv), heads(a)
    seq = tuple(jnp.swapaxes(z, 0, 1) for z in (rh, wh, kh, vh, -kk, kk * ah))

    def step(S, inp):
        r_t, w_t, k_t, v_t, a_t, b_t = inp
        sa = jnp.einsum('bhij,bhj->bhi', S, a_t)
        S = S * w_t[:, :, None, :] + sa[..., None] * b_t[:, :, None, :] + v_t[..., None] * k_t[:, :, None, :]
        return S, jnp.einsum('bhij,bhj->bhi', S, r_t)

    s_fin, y = lax.scan(step, s0.astype(f32), seq)
    y = jnp.swapaxes(y, 0, 1)
    mu = jnp.mean(y, axis=-1, keepdims=True)
    var = jnp.mean(jnp.square(y - mu), axis=-1, keepdims=True)
    y = ((y - mu) * lax.rsqrt(var + GN_EPS)).reshape(b, t, d) * gn_w.astype(f32) + gn_b.astype(f32)
    bonus = jnp.sum(rh * kh * r_k.astype(f32), axis=-1, keepdims=True) * vh
    y = (y + bonus.reshape(b, t, d)) * g.astype(f32)
    return y.astype(xs.dtype), s_fin


def token_mix(h, shift_prev, wkv_prev, k_past, v_past, table, lam_init,
              w_in, shift_mu, decay_base, w_decay_up, a_base, w_a_up, w_g_up,
              k_k, k_a, r_k, gn_w, gn_b, lam_q1, lam_k1, lam_q2, lam_k2, subln_g, w_out):
    b, t, _ = h.shape
    p = h @ w_in
    q, k, v, ps, gates = jnp.split(p, [QK_W, 2 * QK_W, 2 * QK_W + AV_W, 2 * QK_W + AV_W + SHIFT_W], axis=-1)
    q = q.reshape(b, t, A_HEADS, 2 * A_DH)
    k = k.reshape(b, t, A_HEADS, 2 * A_DH)
    v = v.reshape(b, t, A_HEADS, A_VDIM)
    f32 = jnp.float32
    lam = (jnp.exp(jnp.sum(lam_q1.astype(f32) * lam_k1.astype(f32)))
           - jnp.exp(jnp.sum(lam_q2.astype(f32) * lam_k2.astype(f32))) + lam_init)
    if k_past is None:
        o_a = prompt_diff_attn(q, k, v, table, lam)
    else:
        o_a = sample_diff_attn(q, jnp.concatenate([k_past, k], axis=1),
                               jnp.concatenate([v_past, v], axis=1), table, lam)
    o_a = (rmsnorm(o_a, subln_g) * (1.0 - lam_init)).reshape(b, t, D_MODEL)
    xs = token_shift(ps, shift_prev, shift_mu)
    o_b, wkv_new = rwkv7_branch(xs, wkv_prev, decay_base, w_decay_up, a_base, w_a_up, w_g_up,
                                k_k, k_a, r_k, gn_w, gn_b)
    g_a, g_b = jnp.split(jax.nn.sigmoid(gates), 2, axis=-1)
    mixed = (g_a * o_a + g_b * o_b) @ w_out
    return mixed, k, v, ps[:, -1], wkv_new


def swiglu_clamped(hd):
    x_glu = jnp.minimum(hd[..., ::2], SWIGLU_LIMIT)
    x_lin = jnp.clip(hd[..., 1::2], -SWIGLU_LIMIT, SWIGLU_LIMIT)
    return x_glu * jax.nn.sigmoid(SWIGLU_ALPHA * x_glu) * (x_lin + 1.0)


def moe(h, w_router, b_router, w_e1, b_e1, w_e2, b_e2):
    shp = h.shape
    x = h.reshape(-1, D_MODEL)
    n = x.shape[0]
    nk = n * TOP_K
    logits = (x @ w_router).astype(jnp.float32) + b_router.astype(jnp.float32)
    top_v, top_e = lax.top_k(logits, TOP_K)
    gate = jax.nn.softmax(top_v, axis=-1)
    flat_e = top_e.reshape(-1).astype(jnp.int32)
    flat_tok = jnp.arange(nk, dtype=jnp.int32) // TOP_K
    order = jnp.argsort(flat_e)
    e_sorted = flat_e[order]
    tok_sorted = flat_tok[order]
    gate_sorted = gate.reshape(-1)[order]
    counts = jnp.zeros((N_EXPERTS,), jnp.int32).at[flat_e].add(1)
    padded = (counts + MOE_BLOCK - 1) // MOE_BLOCK * MOE_BLOCK
    pad_end = jnp.cumsum(padded)
    pad_start = pad_end - padded
    grp_start = jnp.cumsum(counts) - counts
    dest = pad_start[e_sorted] + jnp.arange(nk, dtype=jnp.int32) - grp_start[e_sorted]
    n_blk = -(-nk // MOE_BLOCK) + N_EXPERTS
    slot_tok = jnp.full((n_blk * MOE_BLOCK,), n, jnp.int32).at[dest].set(tok_sorted)
    blk_start = jnp.arange(n_blk, dtype=jnp.int32) * MOE_BLOCK
    blk_e = jnp.minimum(jnp.searchsorted(pad_end, blk_start, side='right'), N_EXPERTS - 1).astype(jnp.int32)
    x_pad = jnp.concatenate([x, jnp.zeros((1, D_MODEL), x.dtype)], axis=0)
    xb = x_pad[slot_tok].reshape(n_blk, MOE_BLOCK, D_MODEL)

    def expert_block(args):
        xblk, e = args
        hid = swiglu_clamped(xblk @ w_e1[e] + b_e1[e])
        return hid @ w_e2[e] + b_e2[e]

    yb = lax.map(expert_block, (xb, blk_e)).reshape(-1, D_MODEL)
    y_sorted = yb[dest] * gate_sorted[:, None].astype(yb.dtype)
    out = jax.ops.segment_sum(y_sorted, tok_sorted, num_segments=n)
    return out.reshape(shp)


def setup_inputs(seed: int = 0) -> dict:
    key = jax.random.key(seed)
    ks = jax.random.split(key, 40)
    f32 = jnp.float32
    d = D_MODEL

    def nrm(k, shape, s):
        return s * jax.random.normal(k, shape, f32)

    ratio = jnp.arange(d, dtype=f32) / (d - 1)
    return {
        'x_prompt': nrm(ks[0], (BATCH, SEQ, d), 1.0),
        'x_sample': nrm(ks[1], (DEC_BATCH, DEC_SEQ, d), 1.0),
        'cache_k': nrm(ks[2], (DEPTH, DEC_BATCH, PAST_LEN, A_HEADS, 2 * A_DH), 1.0),
        'cache_v': nrm(ks[3], (DEPTH, DEC_BATCH, PAST_LEN, A_HEADS, A_VDIM), 1.0),
        'state_wkv': nrm(ks[4], (DEPTH, DEC_BATCH, B_HEADS, B_HEAD, B_HEAD), 0.1),
        'state_shift': nrm(ks[5], (DEPTH, DEC_BATCH, SHIFT_W), 1.0),
        'meta_tokens': nrm(ks[6], (N_META, d), 1.0),
        'rel_bias_table': nrm(ks[7], (N_BUCKETS, A_HEADS), 0.5),
        'norm1_g': 1.0 + nrm(ks[8], (DEPTH, d), 0.01),
        'w_in': nrm(ks[9], (DEPTH, d, IN_W), d ** -0.5),
        'shift_mu': jax.random.uniform(ks[10], (DEPTH, SHIFT_W), f32, 0.1, 0.9),
        'decay_base': (-6.5 + 5.0 * ratio ** 0.85)[None, :] + nrm(ks[11], (DEPTH, d), 0.01),
        'w_decay_up': nrm(ks[12], (DEPTH, DECAY_LORA, d), 0.5 * DECAY_LORA ** -0.5),
        'a_base': nrm(ks[13], (DEPTH, d), 0.1),
        'w_a_up': nrm(ks[14], (DEPTH, AAA_LORA, d), AAA_LORA ** -0.5),
        'w_g_up': nrm(ks[15], (DEPTH, GATE_LORA, d), GATE_LORA ** -0.5),
        'k_k': 0.85 + nrm(ks[16], (DEPTH, d), 0.02),
        'k_a': 1.0 + nrm(ks[17], (DEPTH, d), 0.02),
        'r_k': nrm(ks[18], (DEPTH, B_HEADS, B_HEAD), 0.1),
        'gn_w': 1.0 + nrm(ks[19], (DEPTH, d), 0.01),
        'gn_b': nrm(ks[20], (DEPTH, d), 0.01),
        'lam_q1': nrm(ks[21], (DEPTH, A_DH), 0.1),
        'lam_k1': nrm(ks[22], (DEPTH, A_DH), 0.1),
        'lam_q2': nrm(ks[23], (DEPTH, A_DH), 0.1),
        'lam_k2': nrm(ks[24], (DEPTH, A_DH), 0.1),
        'subln_g': 1.0 + nrm(ks[25], (DEPTH, A_VDIM), 0.01),
        'w_out': nrm(ks[26], (DEPTH, d, d), d ** -0.5),
        'norm2_g': 1.0 + nrm(ks[27], (DEPTH, d), 0.01),
        'w_router': nrm(ks[28], (DEPTH, d, N_EXPERTS), d ** -0.5),
        'b_router': nrm(ks[29], (DEPTH, N_EXPERTS), 0.01),
        'w_e1': nrm(ks[30], (DEPTH, N_EXPERTS, d, 2 * D_FF), d ** -0.5),
        'b_e1': nrm(ks[31], (DEPTH, N_EXPERTS, 2 * D_FF), 0.01),
        'w_e2': nrm(ks[32], (DEPTH, N_EXPERTS, D_FF, d), D_FF ** -0.5),
        'b_e2': nrm(ks[33], (DEPTH, N_EXPERTS, d), 0.01),
        'final_g': 1.0 + nrm(ks[34], (d,), 0.01),
    }


def reference(x_prompt, x_sample, cache_k, cache_v, state_wkv, state_shift,
              meta_tokens, rel_bias_table, norm1_g, w_in, shift_mu, decay_base, w_decay_up,
              a_base, w_a_up, w_g_up, k_k, k_a, r_k, gn_w, gn_b, lam_q1, lam_k1, lam_q2, lam_k2,
              subln_g, w_out, norm2_g, w_router, b_router, w_e1, b_e1, w_e2, b_e2, final_g):
    bp = x_prompt.shape[0]
    xp = jnp.concatenate([jnp.broadcast_to(meta_tokens.astype(x_prompt.dtype)[None], (bp, N_META, D_MODEL)), x_prompt], axis=1)
    xs_ = x_sample
    kp_l, vp_l, wkvp_l, shp_l = [], [], [], []
    ks_l, vs_l, wkvs_l, shs_l = [], [], [], []
    for l in range(DEPTH):
        lam_init = 0.8 - 0.6 * math.exp(-0.3 * l)
        mix_w = (w_in[l], shift_mu[l], decay_base[l], w_decay_up[l], a_base[l], w_a_up[l], w_g_up[l],
                 k_k[l], k_a[l], r_k[l], gn_w[l], gn_b[l], lam_q1[l], lam_k1[l], lam_q2[l], lam_k2[l],
                 subln_g[l], w_out[l])
        moe_w = (w_router[l], b_router[l], w_e1[l], b_e1[l], w_e2[l], b_e2[l])
        shift0 = jnp.zeros((bp, SHIFT_W), xp.dtype)
        wkv0 = jnp.zeros((bp, B_HEADS, B_HEAD, B_HEAD), jnp.float32)
        m, k_new, v_new, sh_new, wkv_new = token_mix(rmsnorm(xp, norm1_g[l]), shift0, wkv0, None, None,
                                                     rel_bias_table, lam_init, *mix_w)
        xp = xp + m
        xp = xp + moe(rmsnorm(xp, norm2_g[l]), *moe_w)
        kp_l.append(k_new)
        vp_l.append(v_new)
        wkvp_l.append(wkv_new)
        shp_l.append(sh_new)
        m, k_new, v_new, sh_new, wkv_new = token_mix(rmsnorm(xs_, norm1_g[l]), state_shift[l], state_wkv[l],
                                                     cache_k[l], cache_v[l], rel_bias_table, lam_init, *mix_w)
        xs_ = xs_ + m
        xs_ = xs_ + moe(rmsnorm(xs_, norm2_g[l]), *moe_w)
        ks_l.append(k_new)
        vs_l.append(v_new)
        wkvs_l.append(wkv_new.astype(state_wkv.dtype))
        shs_l.append(sh_new)
    y_prompt = rmsnorm(xp, final_g)[:, N_META:]
    y_sample = rmsnorm(xs_, final_g)
    return (y_prompt, y_sample,
            jnp.stack(kp_l), jnp.stack(vp_l), jnp.stack(wkvp_l), jnp.stack(shp_l),
            jnp.stack(ks_l), jnp.stack(vs_l), jnp.stack(wkvs_l), jnp.stack(shs_l))
```

```python
import functools
import math

import numpy as np
import jax
import jax.numpy as jnp
from jax import lax
from jax.experimental import pallas as pl
from jax.experimental.pallas import tpu as pltpu

F32 = jnp.float32
BF16 = jnp.bfloat16

D = 2048
NB = 4
SEQ = 2048
DEC_B = 16
DEC_S = 32
PAST = 2048
CHUNK = 64
N_META = 16
RMS_EPS = 1e-5
A_DH = 64
A_HEADS = 16
A_VD = 128
B_HEAD = 64
B_HEADS = 32
DECAY_LORA = 96
AAA_LORA = 96
GATE_LORA = 256
GN_EPS = 64e-5
N_BUCKETS = 32
MAX_DISTANCE = 128
N_EXPERTS = 32
TOP_K = 4
D_FF = 2048
SWIGLU_ALPHA = 1.702
SWIGLU_LIMIT = 7.0
LAM_INIT = 0.8 - 0.6 * math.exp(-0.3 * 0)
SHIFT_W = 3 * D + DECAY_LORA + AAA_LORA + GATE_LORA
LORA_W = DECAY_LORA + AAA_LORA + GATE_LORA
LORA_P = 512

T_PAD = 2304
FRONT = T_PAD - N_META - SEQ
ROWS_P = NB * T_PAD
ROWS_S = DEC_B * DEC_S
ROWS = ROWS_P + ROWS_S

OFF_Q = 0
OFF_K = 2048
OFF_V = 4096
OFF_R = 6144
OFF_KX = 8192
OFF_VX = 10240
OFF_GA = 12288
OFF_GB = 14336
OFF_LORA = 16384
PW = OFF_LORA + LORA_P

NEG = -1e30
VMEM_LIMIT = 56 * 1024 * 1024


def _cparams(sem, vmem=VMEM_LIMIT):
    return pltpu.CompilerParams(dimension_semantics=sem, vmem_limit_bytes=vmem)


def _sigmoid(x):
    return 1.0 / (1.0 + jnp.exp(-x))


TM1 = 512
TN1 = 512


def _inproj_kernel(x_ref, g_ref, w_ref, o_ref, h_scr):
    @pl.when(pl.program_id(1) == 0)
    def _():
        x = x_ref[...]
        ms = jnp.mean(x * x, axis=-1, keepdims=True)
        h_scr[...] = ((x * lax.rsqrt(ms + RMS_EPS)) * g_ref[...]).astype(BF16)

    o_ref[...] = jnp.dot(h_scr[...], w_ref[...], preferred_element_type=F32)


def _inproj(x_all, g, w):
    return pl.pallas_call(
        _inproj_kernel,
        grid=(ROWS // TM1, PW // TN1),
        in_specs=[pl.BlockSpec((TM1, D), lambda i, j: (i, 0)),
                  pl.BlockSpec((1, D), lambda i, j: (0, 0)),
                  pl.BlockSpec((D, TN1), lambda i, j: (0, j))],
        out_specs=pl.BlockSpec((TM1, TN1), lambda i, j: (i, j)),
        out_shape=jax.ShapeDtypeStruct((ROWS, PW), F32),
        scratch_shapes=[pltpu.VMEM((TM1, D), BF16)],
        compiler_params=_cparams(("arbitrary", "arbitrary")),
        name="inproj",
    )(x_all, g, w)


def _lam_value(lamp_ref):
    lp = lamp_ref[...]
    s1 = jnp.sum(lp[0:1, :] * lp[1:2, :], axis=-1, keepdims=True)
    s2 = jnp.sum(lp[2:3, :] * lp[3:4, :], axis=-1, keepdims=True)
    return jnp.exp(s1) - jnp.exp(s2) + LAM_INIT


def _bias_from_buckets(bkt, table_ref, h):
    out = jnp.zeros(bkt.shape, F32)
    for n in range(N_BUCKETS):
        out = jnp.where(bkt == n, table_ref[n, h], out)
    return out


TQ = 256
NQB = T_PAD // TQ


def _attn_prompt_kernel(table_ref, q_ref, k_ref, v_ref, bkt_ref, lamp_ref, subg_ref, o_ref,
                        kb_scr, vt_scr, bias_scr):
    h = pl.program_id(0)
    b = pl.program_id(1)

    @pl.when(b == 0)
    def _():
        kk = lax.broadcasted_iota(jnp.int32, (TQ, TQ), 0)
        qq = lax.broadcasted_iota(jnp.int32, (TQ, TQ), 1)
        chunk_ok = (kk // CHUNK) <= (qq // CHUNK)
        b_diag = _bias_from_buckets(bkt_ref[0], table_ref, h)
        b_sub = _bias_from_buckets(bkt_ref[1], table_ref, h)
        d_real = jnp.where(chunk_ok, b_diag, NEG)
        d_meta = jnp.where(chunk_ok & (kk >= FRONT), b_diag, NEG)
        bias_scr[0] = jnp.concatenate([d_meta, d_meta], axis=1)
        bias_scr[1] = jnp.concatenate([d_real, d_real], axis=1)
        bias_scr[2] = jnp.concatenate([b_sub, b_sub], axis=1)

    kb_scr[...] = k_ref[...].astype(BF16)
    for i in range(NQB):
        vt_scr[i] = v_ref[i * TQ:(i + 1) * TQ, :].T.astype(BF16)

    c_far = table_ref[N_BUCKETS // 2 - 1, h]
    lam = _lam_value(lamp_ref)
    krow = lax.broadcasted_iota(jnp.int32, (TQ, 1), 0)
    pad_col = jnp.where(krow < FRONT, NEG, 0.0).astype(F32)
    drow = lax.broadcasted_iota(jnp.int32, (A_VD, TQ), 0)

    def q_operand(qi):
        q0 = pl.multiple_of(qi * TQ, TQ)
        qt = (q_ref[pl.ds(q0, TQ), :] * (A_DH ** -0.5)).T
        q1 = jnp.where(drow < A_DH, qt, 0.0)
        q2 = jnp.where(drow >= A_DH, qt, 0.0)
        return jnp.concatenate([q1, q2], axis=1).astype(BF16)

    def step(kj, bias, qpt, carry):
        m, l, acc = carry
        k0 = pl.multiple_of(kj * TQ, TQ)
        s = jnp.dot(kb_scr[pl.ds(k0, TQ), :], qpt, preferred_element_type=F32) + bias
        m_new = jnp.maximum(m, jnp.max(s, axis=0, keepdims=True))
        alpha = jnp.exp(m - m_new)
        p = jnp.exp(s - m_new)
        l = l * alpha + jnp.sum(p, axis=0, keepdims=True)
        acc = acc * alpha + jnp.dot(vt_scr[kj], p.astype(BF16), preferred_element_type=F32)
        return m_new, l, acc

    def finish(qi, carry):
        m, l, acc = carry
        on = acc / l
        ot = on[:, :TQ] - lam * on[:, TQ:]
        ms = jnp.mean(ot * ot, axis=0, keepdims=True)
        o = (ot * lax.rsqrt(ms + RMS_EPS)).T * subg_ref[...] * (1.0 - LAM_INIT)
        q0 = pl.multiple_of(qi * TQ, TQ)
        o_ref[pl.ds(q0, TQ), :] = o

    def init():
        return (jnp.full((1, 2 * TQ), NEG, F32), jnp.zeros((1, 2 * TQ), F32), jnp.zeros((A_VD, 2 * TQ), F32))

    finish(0, step(0, bias_scr[0], q_operand(0), init()))

    def q_block(qi, _):
        qpt = q_operand(qi)
        b0 = jnp.where(qi == 1, bias_scr[2], c_far) + pad_col
        carry = step(0, b0, qpt, init())
        carry = lax.fori_loop(1, jnp.maximum(qi - 1, 1), lambda kj, c: step(kj, c_far, qpt, c), carry)
        bs = jnp.where(qi >= 2, bias_scr[2], NEG)
        carry = step(jnp.maximum(qi - 1, 1), bs, qpt, carry)
        carry = step(qi, bias_scr[1], qpt, carry)
        finish(qi, carry)
        return 0

    lax.fori_loop(1, NQB, q_block, 0)


def _attn_prompt(p_all, table, bkt, lamp, subg):
    cb = D // A_VD
    return pl.pallas_call(
        _attn_prompt_kernel,
        grid_spec=pltpu.PrefetchScalarGridSpec(
            num_scalar_prefetch=0,
            grid=(A_HEADS, NB),
            in_specs=[pl.BlockSpec(memory_space=pltpu.SMEM),
                      pl.BlockSpec((T_PAD, A_VD), lambda h, b: (b, h)),
                      pl.BlockSpec((T_PAD, A_VD), lambda h, b: (b, cb + h)),
                      pl.BlockSpec((T_PAD, A_VD), lambda h, b: (b, 2 * cb + h)),
                      pl.BlockSpec((2, TQ, TQ), lambda h, b: (0, 0, 0)),
                      pl.BlockSpec((4, A_DH), lambda h, b: (0, 0)),
                      pl.BlockSpec((1, A_VD), lambda h, b: (0, 0))],
            out_specs=pl.BlockSpec((T_PAD, A_VD), lambda h, b: (b, h)),
            scratch_shapes=[pltpu.VMEM((T_PAD, A_VD), BF16),
                            pltpu.VMEM((NQB, A_VD, TQ), BF16),
                            pltpu.VMEM((3, TQ, 2 * TQ), F32)]),
        out_shape=jax.ShapeDtypeStruct((ROWS_P, D), F32),
        compiler_params=_cparams(("arbitrary", "arbitrary")),
        name="attn_prompt",
    )(table, p_all, p_all, p_all, bkt, lamp, subg)


TKS = 512
NKS = PAST // TKS


def _attn_sample_kernel(table_ref, q_ref, ck_ref, cv_ref, kn_ref, vn_ref, bl_ref, bn_ref, lamp_ref, subg_ref,
                        o_ref, m_scr, l_scr, acc_scr):
    kb = pl.program_id(1)
    lam = _lam_value(lamp_ref)
    lane = lax.broadcasted_iota(jnp.int32, (DEC_S, A_VD), 1)

    @pl.when(kb == 0)
    def _():
        m_scr[...] = jnp.full(m_scr.shape, NEG, F32)
        l_scr[...] = jnp.zeros(l_scr.shape, F32)
        acc_scr[...] = jnp.zeros(acc_scr.shape, F32)

    def update(h, s, v):
        m = m_scr[h]
        m_new = jnp.maximum(m, jnp.max(s, axis=-1, keepdims=True))
        alpha = jnp.exp(m - m_new)
        p = jnp.exp(s - m_new)
        l_scr[h] = l_scr[h] * alpha + jnp.sum(p, axis=-1, keepdims=True)
        acc_scr[h] = acc_scr[h] * alpha + jnp.dot(p.astype(BF16), v, preferred_element_type=F32)
        m_scr[h] = m_new

    def qprime(h):
        q = q_ref[:, h * A_VD:(h + 1) * A_VD] * (A_DH ** -0.5)
        return jnp.concatenate([jnp.where(lane < A_DH, q, 0.0), jnp.where(lane >= A_DH, q, 0.0)],
                               axis=0).astype(BF16)

    nt = (((1,), (1,)), ((), ()))
    for h in range(A_HEADS):
        qp = qprime(h)
        k = ck_ref[:, h * A_VD:(h + 1) * A_VD].astype(BF16)
        v = cv_ref[:, h * A_VD:(h + 1) * A_VD].astype(BF16)
        s = lax.dot_general(qp, k, nt, preferred_element_type=F32)
        c_far = table_ref[N_BUCKETS // 2 - 1, h]

        @pl.when(kb < NKS - 1)
        def _():
            update(h, s + c_far, v)

        @pl.when(kb == NKS - 1)
        def _():
            bl = _bias_from_buckets(bl_ref[...], table_ref, h)
            update(h, s + jnp.concatenate([bl, bl], axis=0), v)
            kn = kn_ref[:, h * A_VD:(h + 1) * A_VD].astype(BF16)
            vn = vn_ref[:, h * A_VD:(h + 1) * A_VD].astype(BF16)
            bn = _bias_from_buckets(bn_ref[...], table_ref, h)
            sn = lax.dot_general(qp, kn, nt, preferred_element_type=F32) + jnp.concatenate([bn, bn], axis=0)
            update(h, sn, vn)
            on = acc_scr[h] / l_scr[h]
            o = on[:DEC_S] - lam * on[DEC_S:]
            ms = jnp.mean(o * o, axis=-1, keepdims=True)
            o_ref[:, h * A_VD:(h + 1) * A_VD] = (o * lax.rsqrt(ms + RMS_EPS)) * subg_ref[...] * (1.0 - LAM_INIT)


def _attn_sample(p_all, cache_k, cache_v, table, bkt_last, bkt_new, lamp, subg):
    r0 = ROWS_P // DEC_S
    return pl.pallas_call(
        _attn_sample_kernel,
        grid=(DEC_B, NKS),
        in_specs=[pl.BlockSpec(memory_space=pltpu.SMEM),
                  pl.BlockSpec((DEC_S, D), lambda b, kb: (r0 + b, OFF_Q // D)),
                  pl.BlockSpec((None, TKS, D), lambda b, kb: (b, kb, 0)),
                  pl.BlockSpec((None, TKS, D), lambda b, kb: (b, kb, 0)),
                  pl.BlockSpec((DEC_S, D), lambda b, kb: (r0 + b, OFF_K // D)),
                  pl.BlockSpec((DEC_S, D), lambda b, kb: (r0 + b, OFF_V // D)),
                  pl.BlockSpec((DEC_S, TKS), lambda b, kb: (0, 0)),
                  pl.BlockSpec((DEC_S, DEC_S), lambda b, kb: (0, 0)),
                  pl.BlockSpec((4, A_DH), lambda b, kb: (0, 0)),
                  pl.BlockSpec((1, A_VD), lambda b, kb: (0, 0))],
        out_specs=pl.BlockSpec((DEC_S, D), lambda b, kb: (b, 0)),
        out_shape=jax.ShapeDtypeStruct((ROWS_S, D), F32),
        scratch_shapes=[pltpu.VMEM((A_HEADS, 2 * DEC_S, 1), F32),
                        pltpu.VMEM((A_HEADS, 2 * DEC_S, 1), F32),
                        pltpu.VMEM((A_HEADS, 2 * DEC_S, A_VD), F32)],
        compiler_params=_cparams(("arbitrary", "arbitrary")),
        name="attn_sample",
    )(table, p_all, cache_k, cache_v, p_all, p_all, bkt_last, bkt_new, lamp, subg)


def _rwkv_prep_kernel(r_ref, k_ref, v_ref, lo_ref, first_ref, mu_ref, mul_ref, dbase_ref, abase_ref,
                      wd_ref, wa_ref, wg_ref,
                      xr_ref, xk_ref, xv_ref, wp_ref, ap_ref, g_ref, carry_scr, carryl_scr):
    i = pl.program_id(1)
    tm = r_ref.shape[0]
    row = lax.broadcasted_iota(jnp.int32, (tm, 1), 0)

    def shift(x, first, mu):
        prev = jnp.where(row == 0, first, pltpu.roll(x, 1, axis=0))
        return x + (prev - x) * mu

    for s, (src, dst) in enumerate(((r_ref, xr_ref), (k_ref, xk_ref), (v_ref, xv_ref))):
        x = src[...]
        first = jnp.where(i == 0, first_ref[:, s * D:(s + 1) * D], carry_scr[:, s * D:(s + 1) * D])
        dst[...] = shift(x, first, mu_ref[:, s * D:(s + 1) * D])
        carry_scr[:, s * D:(s + 1) * D] = x[tm - 1:tm, :]

    xl = lo_ref[...]
    first = jnp.where(i == 0, first_ref[:, 3 * D:], carryl_scr[...])
    xs = shift(xl, first, mul_ref[...])
    carryl_scr[...] = xl[tm - 1:tm, :]
    wp_ref[...] = dbase_ref[...] + jnp.dot(jnp.tanh(xs).astype(BF16), wd_ref[...], preferred_element_type=F32)
    ap_ref[...] = abase_ref[...] + jnp.dot(xs.astype(BF16), wa_ref[...], preferred_element_type=F32)
    g_ref[...] = jnp.dot(_sigmoid(xs).astype(BF16), wg_ref[...], preferred_element_type=F32)


def _rwkv_prep(p_all, first, mu, mul, dbase, abase, wd, wa, wg, nseq, nblk, tm, row_blk0):
    rows = nseq * nblk * tm
    cs = lambda off: (lambda s, i: (row_blk0 + s * nblk + i, off))
    full = lambda shp: pl.BlockSpec(shp, lambda s, i: tuple(0 for _ in shp))
    out = jax.ShapeDtypeStruct((rows, D), F32)
    ospec = pl.BlockSpec((tm, D), lambda s, i: (s * nblk + i, 0))
    return pl.pallas_call(
        _rwkv_prep_kernel,
        grid=(nseq, nblk),
        in_specs=[pl.BlockSpec((tm, D), cs(OFF_R // D)),
                  pl.BlockSpec((tm, D), cs(OFF_KX // D)),
                  pl.BlockSpec((tm, D), cs(OFF_VX // D)),
                  pl.BlockSpec((tm, LORA_P), cs(OFF_LORA // LORA_P)),
                  pl.BlockSpec((None, 1, 3 * D + LORA_P), lambda s, i: (s, 0, 0)),
                  full((1, 3 * D)), full((1, LORA_P)), full((1, D)), full((1, D)),
                  full((LORA_P, D)), full((LORA_P, D)), full((LORA_P, D))],
        out_specs=[ospec] * 6,
        out_shape=[out] * 6,
        scratch_shapes=[pltpu.VMEM((1, 3 * D), F32), pltpu.VMEM((1, LORA_P), F32)],
        compiler_params=_cparams(("arbitrary", "arbitrary")),
        name="rwkv_prep",
    )(p_all, p_all, p_all, p_all, first, mu, mul, dbase, abase, wd, wa, wg)


TB = 32
IH = 32


def _rwkv_rec_kernel(xr_ref, xk_ref, xv_ref, wp_ref, ap_ref, kk_ref, ka_ref, rk_ref, gnw_ref, gnb_ref, s0_ref,
                     o_ref, sfin_ref, S, w_s, k_s, a_s, b_s, sa_s, y_s):
    tb = pl.program_id(1)

    @pl.when(tb == 0)
    def _():
        S[...] = s0_ref[...]

    z = -wp_ref[...]
    softplus = jnp.maximum(z, 0.0) + jnp.log(1.0 + jnp.exp(-jnp.abs(z)))
    w_s[...] = jnp.exp(-jnp.exp(-softplus - 0.5))
    a = _sigmoid(ap_ref[...])
    xk = xk_ref[...]
    kk = xk * kk_ref[...]
    nrm = jnp.sqrt(jnp.sum(kk * kk, axis=1, keepdims=True))
    kk = kk / jnp.maximum(nrm, 1e-12)
    k_s[...] = xk * (1.0 + (a - 1.0) * ka_ref[...])
    a_s[...] = -kk
    b_s[...] = kk * a

    for hf in range(B_HEAD // IH):
        i0 = hf * IH
        acc = jnp.zeros((IH, 128), F32)
        for j in range(B_HEAD):
            acc = acc + S[j, i0:i0 + IH, :] * a_s[0, j:j + 1, :]
        sa_s[0, i0:i0 + IH, :] = acc

    def token(t, _):
        cur = t % 2
        tn = jnp.minimum(t + 1, TB - 1)
        for hf in range(B_HEAD // IH):
            i0 = hf * IH
            sa = sa_s[cur, i0:i0 + IH, :]
            v = xv_ref[t, i0:i0 + IH, :]
            y = jnp.zeros((IH, 128), F32)
            san = jnp.zeros((IH, 128), F32)
            for j in range(B_HEAD):
                s_new = (S[j, i0:i0 + IH, :] * w_s[t, j:j + 1, :] + sa * b_s[t, j:j + 1, :]
                         + v * k_s[t, j:j + 1, :])
                S[j, i0:i0 + IH, :] = s_new
                y = y + s_new * xr_ref[t, j:j + 1, :]
                san = san + s_new * a_s[tn, j:j + 1, :]
            y_s[i0:i0 + IH, :] = y
            sa_s[1 - cur, i0:i0 + IH, :] = san
        y = y_s[...]
        mu = jnp.mean(y, axis=0, keepdims=True)
        yc = y - mu
        var = jnp.mean(yc * yc, axis=0, keepdims=True)
        bonus = jnp.sum(xr_ref[t] * k_s[t] * rk_ref[...], axis=0, keepdims=True)
        o_ref[t] = yc * lax.rsqrt(var + GN_EPS) * gnw_ref[...] + gnb_ref[...] + bonus * xv_ref[t]
        return 0

    lax.fori_loop(0, TB, token, 0)

    sfin_ref[...] = S[...]


def _rwkv_rec(xr, xk, xv, wp, ap, kk, ka, rk, gnw, gnb, s0):
    t, _, c = xr.shape
    tok = pl.BlockSpec((TB, B_HEAD, 128), lambda g, tb: (tb, 0, g))
    par = pl.BlockSpec((B_HEAD, 128), lambda g, tb: (0, 0))
    st = pl.BlockSpec((B_HEAD, B_HEAD, 128), lambda g, tb: (0, 0, g))
    blk = pltpu.VMEM((TB, B_HEAD, 128), F32)
    return pl.pallas_call(
        _rwkv_rec_kernel,
        grid=(c // 128, t // TB),
        in_specs=[tok] * 5 + [par] * 5 + [st],
        out_specs=[tok, st],
        out_shape=[jax.ShapeDtypeStruct((t, B_HEAD, c), F32), jax.ShapeDtypeStruct((B_HEAD, B_HEAD, c), F32)],
        scratch_shapes=[pltpu.VMEM((B_HEAD, B_HEAD, 128), F32), blk, blk, blk, blk,
                        pltpu.VMEM((2, B_HEAD, 128), F32), pltpu.VMEM((B_HEAD, 128), F32)],
        compiler_params=_cparams(("arbitrary", "arbitrary")),
        name="rwkv_rec",
    )(xr, xk, xv, wp, ap, kk, ka, rk, gnw, gnb, s0)


def _to_chains(a, nb, t):
    return a.reshape(nb, t, B_HEADS, B_HEAD).transpose(1, 3, 0, 2).reshape(t, B_HEAD, nb * B_HEADS)


def _from_chains(a, nb, t):
    return a.reshape(t, B_HEAD, nb, B_HEADS).transpose(2, 0, 3, 1).reshape(nb * t, D)


TM5 = 256


def _mix_kernel(yo_ref, g_ref, oa_ref, ga_ref, gb_ref, x_ref, wo_ref, n2_ref, wr_ref, br_ref,
                x1_ref, h2_ref, te_ref, tg_ref):
    mixed = _sigmoid(ga_ref[...]) * oa_ref[...] + _sigmoid(gb_ref[...]) * (yo_ref[...] * g_ref[...])
    x1 = x_ref[...] + jnp.dot(mixed.astype(BF16), wo_ref[...], preferred_element_type=F32)
    x1_ref[...] = x1
    ms = jnp.mean(x1 * x1, axis=-1, keepdims=True)
    h2 = (x1 * lax.rsqrt(ms + RMS_EPS)) * n2_ref[...]
    h2_ref[...] = h2
    logits = jnp.dot(h2, wr_ref[...], preferred_element_type=F32, precision=lax.Precision.HIGHEST) + br_ref[...]
    lane = lax.broadcasted_iota(jnp.int32, logits.shape, 1)
    lane_f = lane.astype(F32)
    te = jnp.zeros(logits.shape, jnp.int32)
    tv = jnp.full(logits.shape, NEG, F32)
    for k in range(TOP_K):
        m = jnp.max(logits, axis=-1, keepdims=True)
        idx = jnp.min(jnp.where(logits == m, lane_f, 128.0), axis=-1, keepdims=True).astype(jnp.int32)
        te = jnp.where(lane == k, idx, te)
        tv = jnp.where(lane == k, m, tv)
        logits = jnp.where(lane == idx, NEG, logits)
    e = jnp.where(lane < TOP_K, jnp.exp(tv - jnp.max(tv, axis=-1, keepdims=True)), 0.0)
    te_ref[...] = te
    tg_ref[...] = e / jnp.sum(e, axis=-1, keepdims=True)


def _mix(yo, g, oa, p_all, x_all, wo, n2, wr, br, row_blk0):
    rows = yo.shape[0]
    rb = lambda i: (i, 0)
    full = lambda shp: pl.BlockSpec(shp, lambda i: tuple(0 for _ in shp))
    return pl.pallas_call(
        _mix_kernel,
        grid=(rows // TM5,),
        in_specs=[pl.BlockSpec((TM5, D), rb), pl.BlockSpec((TM5, D), rb), pl.BlockSpec((TM5, D), rb),
                  pl.BlockSpec((TM5, D), lambda i: (row_blk0 + i, OFF_GA // D)),
                  pl.BlockSpec((TM5, D), lambda i: (row_blk0 + i, OFF_GB // D)),
                  pl.BlockSpec((TM5, D), lambda i: (row_blk0 + i, 0)),
                  full((D, D)), full((1, D)), full((D, 128)), full((1, 128))],
        out_specs=[pl.BlockSpec((TM5, D), rb), pl.BlockSpec((TM5, D), rb),
                   pl.BlockSpec((TM5, 128), rb), pl.BlockSpec((TM5, 128), rb)],
        out_shape=[jax.ShapeDtypeStruct((rows, D), F32), jax.ShapeDtypeStruct((rows, D), F32),
                   jax.ShapeDtypeStruct((rows, 128), jnp.int32), jax.ShapeDtypeStruct((rows, 128), F32)],
        compiler_params=_cparams(("arbitrary",)),
        name="mix",
    )(yo, g, oa, p_all, p_all, x_all, wo, n2, wr, br)


N_TOK = NB * (N_META + SEQ) + ROWS_S
NK = N_TOK * TOP_K
TMM = 256
N_BLK = -(-NK // TMM) + N_EXPERTS
NSLOT = N_BLK * TMM
TN_G1 = 1024
TN_G2 = 1024
TMC = 128


def _gather_kernel(slot_ref, nused_ref, h_hbm, o_ref, buf, sem):
    m = pl.program_id(0)

    @pl.when(m < nused_ref[0])
    def _():
        def issue(r, _):
            tok = slot_ref[m * TMM + r]
            pltpu.make_async_copy(h_hbm.at[pl.ds(tok, 1), :], buf.at[pl.ds(r, 1), :], sem).start()
            return 0

        lax.fori_loop(0, TMM, issue, 0)
        pltpu.make_async_copy(h_hbm.at[pl.ds(0, TMM), :], buf, sem).wait()
        o_ref[...] = buf[...].astype(BF16)

    @pl.when(m >= nused_ref[0])
    def _():
        o_ref[...] = jnp.zeros(o_ref.shape, BF16)


def _gather(slot_tok, nused, h2):
    return pl.pallas_call(
        _gather_kernel,
        grid_spec=pltpu.PrefetchScalarGridSpec(
            num_scalar_prefetch=2,
            grid=(N_BLK,),
            in_specs=[pl.BlockSpec(memory_space=pl.ANY)],
            out_specs=pl.BlockSpec((TMM, D), lambda m, st, nu: (m, 0)),
            scratch_shapes=[pltpu.VMEM((TMM, D), F32), pltpu.SemaphoreType.DMA(())]),
        out_shape=jax.ShapeDtypeStruct((NSLOT, D), BF16),
        compiler_params=_cparams(("arbitrary",)),
        name="moe_gather",
    )(slot_tok, nused, h2)


def _new_expert(blk_e_ref, m):
    return (m == 0) | (blk_e_ref[m] != blk_e_ref[jnp.maximum(m - 1, 0)])


def _gm1_kernel(blk_e_ref, nused_ref, x_ref, w_ref, b_ref, sel_ref, o_ref, wb_scr):
    m = pl.program_id(1)
    used = m < nused_ref[0]

    @pl.when(used & _new_expert(blk_e_ref, m))
    def _():
        wb_scr[...] = w_ref[...].astype(BF16)

    @pl.when(used)
    def _():
        hd = jnp.dot(x_ref[...], wb_scr[...], preferred_element_type=F32) + b_ref[...]
        glu = jnp.minimum(hd, SWIGLU_LIMIT)
        lin = jnp.clip(hd, -SWIGLU_LIMIT, SWIGLU_LIMIT) + 1.0
        act = glu * _sigmoid(SWIGLU_ALPHA * glu)
        for c in range(TN_G1 // 512):
            parts = []
            for q in range(4):
                sl = slice(c * 512 + q * 128, c * 512 + (q + 1) * 128)
                parts.append(act[:, sl] * pltpu.roll(lin[:, sl], 127, axis=1))
            z = jnp.concatenate(parts, axis=1).astype(BF16)
            o_ref[:, c * 256:(c + 1) * 256] = jnp.dot(z, sel_ref[...], preferred_element_type=F32).astype(BF16)

    @pl.when(jnp.logical_not(used))
    def _():
        o_ref[...] = jnp.zeros(o_ref.shape, BF16)


def _gm1(blk_e, nused, xs, w1, b1, sel):
    def meff(m, nu):
        return jnp.minimum(m, nu[0] - 1)

    return pl.pallas_call(
        _gm1_kernel,
        grid_spec=pltpu.PrefetchScalarGridSpec(
            num_scalar_prefetch=2,
            grid=(2 * D_FF // TN_G1, N_BLK),
            in_specs=[pl.BlockSpec((TMM, D), lambda f, m, be, nu: (meff(m, nu), 0)),
                      pl.BlockSpec((None, D, TN_G1), lambda f, m, be, nu: (be[meff(m, nu)], 0, f)),
                      pl.BlockSpec((None, 1, TN_G1), lambda f, m, be, nu: (be[meff(m, nu)], 0, f)),
                      pl.BlockSpec((512, 256), lambda f, m, be, nu: (0, 0))],
            out_specs=pl.BlockSpec((TMM, TN_G1 // 2), lambda f, m, be, nu: (m, f)),
            scratch_shapes=[pltpu.VMEM((D, TN_G1), BF16)]),
        out_shape=jax.ShapeDtypeStruct((NSLOT, D_FF), BF16),
        compiler_params=_cparams(("arbitrary", "arbitrary")),
        name="moe_up",
    )(blk_e, nused, xs, w1, b1, sel)


def _gm2_kernel(blk_e_ref, nused_ref, x_ref, w_ref, b_ref, o_ref, wb_scr):
    m = pl.program_id(1)
    used = m < nused_ref[0]

    @pl.when(used & _new_expert(blk_e_ref, m))
    def _():
        wb_scr[...] = w_ref[...].astype(BF16)

    @pl.when(used)
    def _():
        o_ref[...] = jnp.dot(x_ref[...], wb_scr[...], preferred_element_type=F32) + b_ref[...]

    @pl.when(jnp.logical_not(used))
    def _():
        o_ref[...] = jnp.zeros(o_ref.shape, F32)


def _gm2(blk_e, nused, hid, w2, b2):
    def meff(m, nu):
        return jnp.minimum(m, nu[0] - 1)

    return pl.pallas_call(
        _gm2_kernel,
        grid_spec=pltpu.PrefetchScalarGridSpec(
            num_scalar_prefetch=2,
            grid=(D // TN_G2, N_BLK),
            in_specs=[pl.BlockSpec((TMM, D_FF), lambda f, m, be, nu: (meff(m, nu), 0)),
                      pl.BlockSpec((None, D_FF, TN_G2), lambda f, m, be, nu: (be[meff(m, nu)], 0, f)),
                      pl.BlockSpec((None, 1, TN_G2), lambda f, m, be, nu: (be[meff(m, nu)], 0, f))],
            out_specs=pl.BlockSpec((TMM, TN_G2), lambda f, m, be, nu: (m, f)),
            scratch_shapes=[pltpu.VMEM((D_FF, TN_G2), BF16)]),
        out_shape=jax.ShapeDtypeStruct((NSLOT, D), F32),
        compiler_params=_cparams(("arbitrary", "arbitrary")),
        name="moe_down",
    )(blk_e, nused, hid, w2, b2)


def _combine_kernel(dest_ref, x1_ref, tg_ref, fg_ref, y_hbm, o_ref, buf, sem):
    m = pl.program_id(0)

    def issue(r, _):
        for k in range(TOP_K):
            d = dest_ref[(m * TMC + r) * TOP_K + k]
            pltpu.make_async_copy(y_hbm.at[pl.ds(d, 1), :], buf.at[pl.ds(k * TMC + r, 1), :], sem).start()
        return 0

    lax.fori_loop(0, TMC, issue, 0)
    pltpu.make_async_copy(y_hbm.at[pl.ds(0, TOP_K * TMC), :], buf, sem).wait()
    x2 = x1_ref[...]
    tg = tg_ref[...]
    for k in range(TOP_K):
        x2 = x2 + tg[:, k:k + 1] * buf[k * TMC:(k + 1) * TMC, :]
    ms = jnp.mean(x2 * x2, axis=-1, keepdims=True)
    o_ref[...] = (x2 * lax.rsqrt(ms + RMS_EPS)) * fg_ref[...]


def _combine(dest, x1, tg, fg, y):
    return pl.pallas_call(
        _combine_kernel,
        grid_spec=pltpu.PrefetchScalarGridSpec(
            num_scalar_prefetch=1,
            grid=(ROWS // TMC,),
            in_specs=[pl.BlockSpec((TMC, D), lambda m, d: (m, 0)),
                      pl.BlockSpec((TMC, 128), lambda m, d: (m, 0)),
                      pl.BlockSpec((1, D), lambda m, d: (0, 0)),
                      pl.BlockSpec(memory_space=pl.ANY)],
            out_specs=pl.BlockSpec((TMC, D), lambda m, d: (m, 0)),
            scratch_shapes=[pltpu.VMEM((TOP_K * TMC, D), F32), pltpu.SemaphoreType.DMA(())]),
        out_shape=jax.ShapeDtypeStruct((ROWS, D), F32),
        compiler_params=_cparams(("arbitrary",)),
        name="moe_combine",
    )(dest, x1, tg, fg, y)


def _t5_bucket(rel):
    nb = N_BUCKETS // 2
    ret = jnp.where(rel > 0, nb, 0)
    n = jnp.abs(rel)
    max_exact = nb // 2
    nf = jnp.maximum(n, 1).astype(jnp.float32)
    large = max_exact + (jnp.log(nf / max_exact) / math.log(MAX_DISTANCE / max_exact) * (nb - max_exact)).astype(jnp.int32)
    large = jnp.minimum(large, nb - 1)
    return ret + jnp.where(n < max_exact, n, large)


def _valid_rows():
    rows = [np.arange(b * T_PAD + FRONT, (b + 1) * T_PAD) for b in range(NB)]
    rows.append(np.arange(ROWS_P, ROWS))
    return np.concatenate(rows).astype(np.int32)


def kernel(x_prompt, x_sample, cache_k, cache_v, state_wkv, state_shift, meta_tokens, rel_bias_table, norm1_g, w_in, shift_mu, decay_base, w_decay_up, a_base, w_a_up, w_g_up, k_k, k_a, r_k, gn_w, gn_b, lam_q1, lam_k1, lam_q2, lam_k2, subln_g, w_out, norm2_g, w_router, b_router, w_e1, b_e1, w_e2, b_e2, final_g):
    xp = jnp.concatenate([jnp.zeros((NB, FRONT, D), F32),
                          jnp.broadcast_to(meta_tokens[None], (NB, N_META, D)), x_prompt], axis=1)
    x_all = jnp.concatenate([xp.reshape(ROWS_P, D), x_sample.reshape(ROWS_S, D)], axis=0)
    lora0 = OFF_R + 3 * D
    w_p = jnp.concatenate([w_in[0][:, :lora0], w_in[0][:, lora0 + LORA_W:], w_in[0][:, lora0:lora0 + LORA_W],
                           jnp.zeros((D, LORA_P - LORA_W), F32)], axis=1).astype(BF16)

    p_all = _inproj(x_all, norm1_g, w_p)

    kq = jnp.arange(TQ, dtype=jnp.int32)
    bkt = jnp.stack([_t5_bucket(kq[:, None] - kq[None, :]), _t5_bucket(kq[:, None] - kq[None, :] - TQ)])
    lamp = jnp.concatenate([lam_q1, lam_k1, lam_q2, lam_k2], axis=0)
    oa_p = _attn_prompt(p_all, rel_bias_table, bkt, lamp, subln_g)
    qpos = PAST + jnp.arange(DEC_S, dtype=jnp.int32)
    bkt_last = _t5_bucket((PAST - TKS + jnp.arange(TKS, dtype=jnp.int32))[None, :] - qpos[:, None])
    bkt_new = _t5_bucket(qpos[None, :] - qpos[:, None])
    oa_s = _attn_sample(p_all, cache_k[0].reshape(DEC_B, PAST, D), cache_v[0].reshape(DEC_B, PAST, D),
                        rel_bias_table, bkt_last, bkt_new, lamp, subln_g)

    mu = shift_mu[0]
    mu3 = mu[:3 * D][None]
    mul = jnp.pad(mu[3 * D:], (0, LORA_P - LORA_W))[None]
    wd = jnp.zeros((LORA_P, D), F32).at[:DECAY_LORA].set(w_decay_up[0]).astype(BF16)
    wa = jnp.zeros((LORA_P, D), F32).at[DECAY_LORA:DECAY_LORA + AAA_LORA].set(w_a_up[0]).astype(BF16)
    wg = jnp.zeros((LORA_P, D), F32).at[DECAY_LORA + AAA_LORA:LORA_W].set(w_g_up[0]).astype(BF16)
    first_p = jnp.zeros((NB, 1, 3 * D + LORA_P), F32)
    first_s = jnp.pad(state_shift[0], ((0, 0), (0, LORA_P - LORA_W)))[:, None, :]
    prep_p = _rwkv_prep(p_all, first_p, mu3, mul, decay_base, a_base, wd, wa, wg, NB, T_PAD // 256, 256, 0)
    prep_s = _rwkv_prep(p_all, first_s, mu3, mul, decay_base, a_base, wd, wa, wg, DEC_B, 1, DEC_S, ROWS_P // DEC_S)

    def chain_tile(v):
        return jnp.tile(v.reshape(B_HEADS, B_HEAD).T, (1, 128 // B_HEADS))

    par = [chain_tile(v.reshape(-1)) for v in (k_k[0], k_a[0], r_k[0], gn_w[0], gn_b[0])]
    s0_p = jnp.zeros((B_HEAD, B_HEAD, NB * B_HEADS), F32)
    s0_s = state_wkv[0].transpose(3, 2, 0, 1).reshape(B_HEAD, B_HEAD, DEC_B * B_HEADS)
    yo_p, sfin_p = _rwkv_rec(*[_to_chains(a, NB, T_PAD) for a in prep_p[:5]], *par, s0_p)
    yo_s, sfin_s = _rwkv_rec(*[_to_chains(a, DEC_B, DEC_S) for a in prep_s[:5]], *par, s0_s)

    wo = w_out[0].astype(BF16)
    wr = jnp.pad(w_router[0], ((0, 0), (0, 128 - N_EXPERTS)))
    br = jnp.concatenate([b_router[0], jnp.full((128 - N_EXPERTS,), NEG, F32)])[None]
    mix_p = _mix(_from_chains(yo_p, NB, T_PAD), prep_p[5], oa_p, p_all, x_all, wo, norm2_g, wr, br, 0)
    mix_s = _mix(_from_chains(yo_s, DEC_B, DEC_S), prep_s[5], oa_s, p_all, x_all, wo, norm2_g, wr, br,
                 ROWS_P // TM5)
    x1, h2, te, tg = [jnp.concatenate([a, b], axis=0) for a, b in zip(mix_p, mix_s)]

    valid = jnp.asarray(_valid_rows())
    flat_e = te[valid, :TOP_K].reshape(-1)
    order = jnp.argsort(flat_e)
    e_sorted = flat_e[order]
    tok_sorted = valid[order // TOP_K]
    counts = jnp.zeros((N_EXPERTS,), jnp.int32).at[flat_e].add(1)
    padded = (counts + TMM - 1) // TMM * TMM
    pad_end = jnp.cumsum(padded)
    pad_start = pad_end - padded
    grp_start = jnp.cumsum(counts) - counts
    dest_sorted = pad_start[e_sorted] + jnp.arange(NK, dtype=jnp.int32) - grp_start[e_sorted]
    slot_tok = jnp.zeros((NSLOT,), jnp.int32).at[dest_sorted].set(tok_sorted)
    blk_start = jnp.arange(N_BLK, dtype=jnp.int32) * TMM
    blk_e = jnp.minimum(jnp.searchsorted(pad_end, blk_start, side='right'), N_EXPERTS - 1).astype(jnp.int32)
    nused = (pad_end[-1:] // TMM).astype(jnp.int32)
    dest_tok = jnp.zeros((NK,), jnp.int32).at[order].set(dest_sorted).reshape(N_TOK, TOP_K)
    dest = jnp.zeros((ROWS, TOP_K), jnp.int32).at[valid].set(dest_tok).reshape(-1)

    sel_np = np.zeros((512, 256), np.float32)
    sel_np[2 * np.arange(256), np.arange(256)] = 1.0
    xs = _gather(slot_tok, nused, h2)
    hid = _gm1(blk_e, nused, xs, w_e1[0], b_e1[0][:, None, :], jnp.asarray(sel_np, BF16))
    ys = _gm2(blk_e, nused, hid, w_e2[0], b_e2[0][:, None, :])
    y_all = _combine(dest, x1, tg, final_g[None], ys)

    y_prompt = y_all[:ROWS_P].reshape(NB, T_PAD, D)[:, FRONT + N_META:]
    y_sample = y_all[ROWS_P:].reshape(DEC_B, DEC_S, D)
    pp = p_all[:ROWS_P].reshape(NB, T_PAD, PW)
    psm = p_all[ROWS_P:].reshape(DEC_B, DEC_S, PW)
    k_prompt = pp[:, FRONT:, OFF_K:OFF_K + D].reshape(1, NB, N_META + SEQ, A_HEADS, 2 * A_DH)
    v_prompt = pp[:, FRONT:, OFF_V:OFF_V + D].reshape(1, NB, N_META + SEQ, A_HEADS, A_VD)
    k_sample = psm[:, :, OFF_K:OFF_K + D].reshape(1, DEC_B, DEC_S, A_HEADS, 2 * A_DH)
    v_sample = psm[:, :, OFF_V:OFF_V + D].reshape(1, DEC_B, DEC_S, A_HEADS, A_VD)
    def last_ps(a):
        return jnp.concatenate([a[:, -1, OFF_R:OFF_R + 3 * D], a[:, -1, OFF_LORA:OFF_LORA + LORA_W]], axis=-1)[None]

    shift_prompt = last_ps(pp)
    shift_sample = last_ps(psm)
    wkv_prompt = sfin_p.reshape(B_HEAD, B_HEAD, NB, B_HEADS).transpose(2, 3, 1, 0)[None]
    wkv_sample = sfin_s.reshape(B_HEAD, B_HEAD, DEC_B, B_HEADS).transpose(2, 3, 1, 0)[None].astype(state_wkv.dtype)
    return (y_prompt, y_sample, k_prompt, v_prompt, wkv_prompt, shift_prompt,
            k_sample, v_sample, wkv_sample, shift_sample)
```

```python
import functools
import math

import numpy as np
import jax
import jax.numpy as jnp
from jax import lax
from jax.experimental import pallas as pl
from jax.experimental.pallas import tpu as pltpu

F32 = jnp.float32
BF16 = jnp.bfloat16

D = 2048
NB = 4
SEQ = 2048
DEC_B = 16
DEC_S = 32
PAST = 2048
CHUNK = 64
N_META = 16
RMS_EPS = 1e-5
A_DH = 64
A_HEADS = 16
A_VD = 128
B_HEAD = 64
B_HEADS = 32
DECAY_LORA = 96
AAA_LORA = 96
GATE_LORA = 256
GN_EPS = 64e-5
N_BUCKETS = 32
MAX_DISTANCE = 128
N_EXPERTS = 32
TOP_K = 4
D_FF = 2048
SWIGLU_ALPHA = 1.702
SWIGLU_LIMIT = 7.0
LAM_INIT = 0.8 - 0.6 * math.exp(-0.3 * 0)
SHIFT_W = 3 * D + DECAY_LORA + AAA_LORA + GATE_LORA
LORA_W = DECAY_LORA + AAA_LORA + GATE_LORA
LORA_P = 512

T_PAD = 2304
FRONT = T_PAD - N_META - SEQ
ROWS_P = NB * T_PAD
ROWS_S = DEC_B * DEC_S
ROWS = ROWS_P + ROWS_S

OFF_Q = 0
OFF_K = 2048
OFF_V = 4096
OFF_R = 6144
OFF_KX = 8192
OFF_VX = 10240
OFF_GA = 12288
OFF_GB = 14336
OFF_LORA = 16384
PW = OFF_LORA + LORA_P

NEG = -1e30
VMEM_LIMIT = 56 * 1024 * 1024


def _cparams(sem, vmem=VMEM_LIMIT):
    return pltpu.CompilerParams(dimension_semantics=sem, vmem_limit_bytes=vmem)


def _sigmoid(x):
    return 1.0 / (1.0 + jnp.exp(-x))


TM1 = 512
TN1 = 512


def _inproj_kernel(x_ref, g_ref, w_ref, o_ref, h_scr):
    @pl.when(pl.program_id(1) == 0)
    def _():
        x = x_ref[...]
        ms = jnp.mean(x * x, axis=-1, keepdims=True)
        h_scr[...] = ((x * lax.rsqrt(ms + RMS_EPS)) * g_ref[...]).astype(BF16)

    o_ref[...] = jnp.dot(h_scr[...], w_ref[...], preferred_element_type=F32)


def _inproj(x_all, g, w):
    return pl.pallas_call(
        _inproj_kernel,
        grid=(ROWS // TM1, PW // TN1),
        in_specs=[pl.BlockSpec((TM1, D), lambda i, j: (i, 0)),
                  pl.BlockSpec((1, D), lambda i, j: (0, 0)),
                  pl.BlockSpec((D, TN1), lambda i, j: (0, j))],
        out_specs=pl.BlockSpec((TM1, TN1), lambda i, j: (i, j)),
        out_shape=jax.ShapeDtypeStruct((ROWS, PW), F32),
        scratch_shapes=[pltpu.VMEM((TM1, D), BF16)],
        compiler_params=_cparams(("arbitrary", "arbitrary")),
        name="inproj",
    )(x_all, g, w)


def _lam_value(lamp_ref):
    lp = lamp_ref[...]
    s1 = jnp.sum(lp[0:1, :] * lp[1:2, :], axis=-1, keepdims=True)
    s2 = jnp.sum(lp[2:3, :] * lp[3:4, :], axis=-1, keepdims=True)
    return jnp.exp(s1) - jnp.exp(s2) + LAM_INIT


def _bias_from_buckets(bkt, table_ref, h):
    out = jnp.zeros(bkt.shape, F32)
    for n in range(N_BUCKETS):
        out = jnp.where(bkt == n, table_ref[n, h], out)
    return out


TQ = 256
NQB = T_PAD // TQ
BIAS_META, BIAS_DIAG, BIAS_SUB, BIAS_SUB_PAD, BIAS_FAR_PAD, BIAS_FAR = range(6)


def _attn_prompt_kernel(table_ref, q_ref, k_ref, v_ref, bkt_ref, lamp_ref, subg_ref, o_ref,
                        kb_scr, vt_scr, bias_scr):
    h = pl.program_id(0)
    b = pl.program_id(1)

    @pl.when(b == 0)
    def _():
        kk = lax.broadcasted_iota(jnp.int32, (TQ, TQ), 0)
        qq = lax.broadcasted_iota(jnp.int32, (TQ, TQ), 1)
        chunk_ok = (kk // CHUNK) <= (qq // CHUNK)
        pad = jnp.where(kk < FRONT, NEG, 0.0).astype(F32)
        b_diag = _bias_from_buckets(bkt_ref[0], table_ref, h)
        b_sub = _bias_from_buckets(bkt_ref[1], table_ref, h)
        b_far = jnp.zeros((TQ, TQ), F32) + table_ref[N_BUCKETS // 2 - 1, h]
        tiles = (jnp.where(chunk_ok, b_diag, NEG) + pad,
                 jnp.where(chunk_ok, b_diag, NEG),
                 b_sub,
                 b_sub + pad,
                 b_far + pad,
                 b_far)
        for i, t in enumerate(tiles):
            bias_scr[i] = jnp.concatenate([t, t], axis=1)

    kb_scr[...] = k_ref[...].astype(BF16)
    for i in range(NQB):
        vt_scr[i] = v_ref[i * TQ:(i + 1) * TQ, :].T.astype(BF16)

    lam = _lam_value(lamp_ref)
    drow = lax.broadcasted_iota(jnp.int32, (A_VD, TQ), 0)

    def q_operand(qi):
        q0 = pl.multiple_of(qi * TQ, TQ)
        qt = (q_ref[pl.ds(q0, TQ), :] * (A_DH ** -0.5)).T
        q1 = jnp.where(drow < A_DH, qt, 0.0)
        q2 = jnp.where(drow >= A_DH, qt, 0.0)
        return jnp.concatenate([q1, q2], axis=1).astype(BF16)

    def scores(kj, qpt):
        k0 = pl.multiple_of(kj * TQ, TQ)
        return jnp.dot(kb_scr[pl.ds(k0, TQ), :], qpt, preferred_element_type=F32)

    def accumulate(kj, bias_id, s, carry):
        m, l, acc = carry
        s = s + bias_scr[bias_id]
        m_new = jnp.maximum(m, jnp.max(s, axis=0, keepdims=True))
        alpha = jnp.exp(m - m_new)
        p = jnp.exp(s - m_new)
        l = l * alpha + jnp.sum(p, axis=0, keepdims=True)
        acc = acc * alpha + jnp.dot(vt_scr[kj], p.astype(BF16), preferred_element_type=F32)
        return m_new, l, acc

    def finish(qi, carry):
        m, l, acc = carry
        on = acc / l
        ot = on[:, :TQ] - lam * on[:, TQ:]
        ms = jnp.mean(ot * ot, axis=0, keepdims=True)
        o = (ot * lax.rsqrt(ms + RMS_EPS)).T * subg_ref[...] * (1.0 - LAM_INIT)
        q0 = pl.multiple_of(qi * TQ, TQ)
        o_ref[pl.ds(q0, TQ), :] = o

    def init():
        return (jnp.full((1, 2 * TQ), NEG, F32), jnp.zeros((1, 2 * TQ), F32), jnp.zeros((A_VD, 2 * TQ), F32))

    finish(0, accumulate(0, BIAS_META, scores(0, q_operand(0)), init()))

    def q_block(qi, _):
        qpt = q_operand(qi)

        def k_block(kj, c):
            s_next = scores(kj + 1, qpt)
            first = kj == 0
            bias_id = jnp.where(kj == qi - 1, jnp.where(first, BIAS_SUB_PAD, BIAS_SUB),
                                jnp.where(first, BIAS_FAR_PAD, BIAS_FAR))
            return (s_next,) + accumulate(kj, bias_id, c[0], c[1:])

        c = lax.fori_loop(0, qi, k_block, (scores(0, qpt),) + init())
        finish(qi, accumulate(qi, BIAS_DIAG, c[0], c[1:]))
        return 0

    lax.fori_loop(1, NQB, q_block, 0)


def _attn_prompt(p_all, table, bkt, lamp, subg):
    cb = D // A_VD
    return pl.pallas_call(
        _attn_prompt_kernel,
        grid_spec=pltpu.PrefetchScalarGridSpec(
            num_scalar_prefetch=0,
            grid=(A_HEADS, NB),
            in_specs=[pl.BlockSpec(memory_space=pltpu.SMEM),
                      pl.BlockSpec((T_PAD, A_VD), lambda h, b: (b, h)),
                      pl.BlockSpec((T_PAD, A_VD), lambda h, b: (b, cb + h)),
                      pl.BlockSpec((T_PAD, A_VD), lambda h, b: (b, 2 * cb + h)),
                      pl.BlockSpec((2, TQ, TQ), lambda h, b: (0, 0, 0)),
                      pl.BlockSpec((4, A_DH), lambda h, b: (0, 0)),
                      pl.BlockSpec((1, A_VD), lambda h, b: (0, 0))],
            out_specs=pl.BlockSpec((T_PAD, A_VD), lambda h, b: (b, h)),
            scratch_shapes=[pltpu.VMEM((T_PAD, A_VD), BF16),
                            pltpu.VMEM((NQB, A_VD, TQ), BF16),
                            pltpu.VMEM((6, TQ, 2 * TQ), F32)]),
        out_shape=jax.ShapeDtypeStruct((ROWS_P, D), F32),
        compiler_params=_cparams(("arbitrary", "arbitrary")),
        name="attn_prompt",
    )(table, p_all, p_all, p_all, bkt, lamp, subg)


TKS = 512
NKS = PAST // TKS


def _attn_sample_kernel(table_ref, q_ref, ck_ref, cv_ref, kn_ref, vn_ref, bl_ref, bn_ref, lamp_ref, subg_ref,
                        o_ref, m_scr, l_scr, acc_scr):
    kb = pl.program_id(1)
    lam = _lam_value(lamp_ref)
    lane = lax.broadcasted_iota(jnp.int32, (DEC_S, A_VD), 1)

    @pl.when(kb == 0)
    def _():
        m_scr[...] = jnp.full(m_scr.shape, NEG, F32)
        l_scr[...] = jnp.zeros(l_scr.shape, F32)
        acc_scr[...] = jnp.zeros(acc_scr.shape, F32)

    def update(h, s, v):
        m = m_scr[h]
        m_new = jnp.maximum(m, jnp.max(s, axis=-1, keepdims=True))
        alpha = jnp.exp(m - m_new)
        p = jnp.exp(s - m_new)
        l_scr[h] = l_scr[h] * alpha + jnp.sum(p, axis=-1, keepdims=True)
        acc_scr[h] = acc_scr[h] * alpha + jnp.dot(p.astype(BF16), v, preferred_element_type=F32)
        m_scr[h] = m_new

    def qprime(h):
        q = q_ref[:, h * A_VD:(h + 1) * A_VD] * (A_DH ** -0.5)
        return jnp.concatenate([jnp.where(lane < A_DH, q, 0.0), jnp.where(lane >= A_DH, q, 0.0)],
                               axis=0).astype(BF16)

    nt = (((1,), (1,)), ((), ()))
    for h in range(A_HEADS):
        qp = qprime(h)
        k = ck_ref[pl.ds(h, TKS, stride=A_HEADS), :].astype(BF16)
        v = cv_ref[pl.ds(h, TKS, stride=A_HEADS), :].astype(BF16)
        s = lax.dot_general(qp, k, nt, preferred_element_type=F32)
        c_far = table_ref[N_BUCKETS // 2 - 1, h]

        @pl.when(kb < NKS - 1)
        def _():
            update(h, s + c_far, v)

        @pl.when(kb == NKS - 1)
        def _():
            bl = _bias_from_buckets(bl_ref[...], table_ref, h)
            update(h, s + jnp.concatenate([bl, bl], axis=0), v)
            kn = kn_ref[:, h * A_VD:(h + 1) * A_VD].astype(BF16)
            vn = vn_ref[:, h * A_VD:(h + 1) * A_VD].astype(BF16)
            bn = _bias_from_buckets(bn_ref[...], table_ref, h)
            sn = lax.dot_general(qp, kn, nt, preferred_element_type=F32) + jnp.concatenate([bn, bn], axis=0)
            update(h, sn, vn)
            on = acc_scr[h] / l_scr[h]
            o = on[:DEC_S] - lam * on[DEC_S:]
            ms = jnp.mean(o * o, axis=-1, keepdims=True)
            o_ref[:, h * A_VD:(h + 1) * A_VD] = (o * lax.rsqrt(ms + RMS_EPS)) * subg_ref[...] * (1.0 - LAM_INIT)


def _attn_sample(p_all, cache_k, cache_v, table, bkt_last, bkt_new, lamp, subg):
    r0 = ROWS_P // DEC_S
    return pl.pallas_call(
        _attn_sample_kernel,
        grid=(DEC_B, NKS),
        in_specs=[pl.BlockSpec(memory_space=pltpu.SMEM),
                  pl.BlockSpec((DEC_S, D), lambda b, kb: (r0 + b, OFF_Q // D)),
                  pl.BlockSpec((None, TKS * A_HEADS, A_VD), lambda b, kb: (b, kb, 0)),
                  pl.BlockSpec((None, TKS * A_HEADS, A_VD), lambda b, kb: (b, kb, 0)),
                  pl.BlockSpec((DEC_S, D), lambda b, kb: (r0 + b, OFF_K // D)),
                  pl.BlockSpec((DEC_S, D), lambda b, kb: (r0 + b, OFF_V // D)),
                  pl.BlockSpec((DEC_S, TKS), lambda b, kb: (0, 0)),
                  pl.BlockSpec((DEC_S, DEC_S), lambda b, kb: (0, 0)),
                  pl.BlockSpec((4, A_DH), lambda b, kb: (0, 0)),
                  pl.BlockSpec((1, A_VD), lambda b, kb: (0, 0))],
        out_specs=pl.BlockSpec((DEC_S, D), lambda b, kb: (b, 0)),
        out_shape=jax.ShapeDtypeStruct((ROWS_S, D), F32),
        scratch_shapes=[pltpu.VMEM((A_HEADS, 2 * DEC_S, 1), F32),
                        pltpu.VMEM((A_HEADS, 2 * DEC_S, 1), F32),
                        pltpu.VMEM((A_HEADS, 2 * DEC_S, A_VD), F32)],
        compiler_params=_cparams(("arbitrary", "arbitrary")),
        name="attn_sample",
    )(table, p_all, cache_k, cache_v, p_all, p_all, bkt_last, bkt_new, lamp, subg)


def _rwkv_prep_kernel(r_ref, k_ref, v_ref, lo_ref, first_ref, mu_ref, mul_ref, dbase_ref, abase_ref,
                      wd_ref, wa_ref, wg_ref,
                      xr_ref, xk_ref, xv_ref, wp_ref, ap_ref, g_ref, carry_scr, carryl_scr):
    i = pl.program_id(1)
    tm = r_ref.shape[0]
    row = lax.broadcasted_iota(jnp.int32, (tm, 1), 0)

    def shift(x, first, mu):
        prev = jnp.where(row == 0, first, pltpu.roll(x, 1, axis=0))
        return x + (prev - x) * mu

    for s, (src, dst) in enumerate(((r_ref, xr_ref), (k_ref, xk_ref), (v_ref, xv_ref))):
        x = src[...]
        first = jnp.where(i == 0, first_ref[:, s * D:(s + 1) * D], carry_scr[:, s * D:(s + 1) * D])
        dst[...] = shift(x, first, mu_ref[:, s * D:(s + 1) * D])
        carry_scr[:, s * D:(s + 1) * D] = x[tm - 1:tm, :]

    xl = lo_ref[...]
    first = jnp.where(i == 0, first_ref[:, 3 * D:], carryl_scr[...])
    xs = shift(xl, first, mul_ref[...])
    carryl_scr[...] = xl[tm - 1:tm, :]
    wp_ref[...] = dbase_ref[...] + jnp.dot(jnp.tanh(xs).astype(BF16), wd_ref[...], preferred_element_type=F32)
    ap_ref[...] = abase_ref[...] + jnp.dot(xs.astype(BF16), wa_ref[...], preferred_element_type=F32)
    g_ref[...] = jnp.dot(_sigmoid(xs).astype(BF16), wg_ref[...], preferred_element_type=F32)


def _rwkv_prep(p_all, first, mu, mul, dbase, abase, wd, wa, wg, nseq, nblk, tm, row_blk0):
    rows = nseq * nblk * tm
    cs = lambda off: (lambda s, i: (row_blk0 + s * nblk + i, off))
    full = lambda shp: pl.BlockSpec(shp, lambda s, i: tuple(0 for _ in shp))
    out = jax.ShapeDtypeStruct((rows, D), F32)
    ospec = pl.BlockSpec((tm, D), lambda s, i: (s * nblk + i, 0))
    return pl.pallas_call(
        _rwkv_prep_kernel,
        grid=(nseq, nblk),
        in_specs=[pl.BlockSpec((tm, D), cs(OFF_R // D)),
                  pl.BlockSpec((tm, D), cs(OFF_KX // D)),
                  pl.BlockSpec((tm, D), cs(OFF_VX // D)),
                  pl.BlockSpec((tm, LORA_P), cs(OFF_LORA // LORA_P)),
                  pl.BlockSpec((None, 1, 3 * D + LORA_P), lambda s, i: (s, 0, 0)),
                  full((1, 3 * D)), full((1, LORA_P)), full((1, D)), full((1, D)),
                  full((LORA_P, D)), full((LORA_P, D)), full((LORA_P, D))],
        out_specs=[ospec] * 6,
        out_shape=[out] * 6,
        scratch_shapes=[pltpu.VMEM((1, 3 * D), F32), pltpu.VMEM((1, LORA_P), F32)],
        compiler_params=_cparams(("arbitrary", "arbitrary")),
        name="rwkv_prep",
    )(p_all, p_all, p_all, p_all, first, mu, mul, dbase, abase, wd, wa, wg)


TB = 32
IH = 32


def _rwkv_rec_kernel(skip, *refs):
    tb = pl.program_id(1)
    s0_ref, o_ref, sfin_ref, S = refs[10], refs[11], refs[12], refs[13]

    @pl.when(tb == 0)
    def _():
        S[...] = s0_ref[...]

    if skip:
        @pl.when(tb < skip)
        def _():
            o_ref[...] = jnp.zeros(o_ref.shape, F32)

        pl.when(tb >= skip)(functools.partial(_rwkv_rec_block, *refs))
    else:
        _rwkv_rec_block(*refs)
    sfin_ref[...] = S[...]


def _rwkv_rec_block(xr_ref, xk_ref, xv_ref, wp_ref, ap_ref, kk_ref, ka_ref, rk_ref, gnw_ref, gnb_ref, s0_ref,
                    o_ref, sfin_ref, S, w_s, k_s, a_s, b_s, sa_s, y_s):
    z = -wp_ref[...]
    softplus = jnp.maximum(z, 0.0) + jnp.log(1.0 + jnp.exp(-jnp.abs(z)))
    w_s[...] = jnp.exp(-jnp.exp(-softplus - 0.5))
    a = _sigmoid(ap_ref[...])
    xk = xk_ref[...]
    kk = xk * kk_ref[...]
    nrm = jnp.sqrt(jnp.sum(kk * kk, axis=1, keepdims=True))
    kk = kk / jnp.maximum(nrm, 1e-12)
    k_s[...] = xk * (1.0 + (a - 1.0) * ka_ref[...])
    a_s[...] = -kk
    b_s[...] = kk * a

    for hf in range(B_HEAD // IH):
        i0 = hf * IH
        acc = jnp.zeros((IH, 128), F32)
        for j in range(B_HEAD):
            acc = acc + S[j, i0:i0 + IH, :] * a_s[0, j:j + 1, :]
        sa_s[0, i0:i0 + IH, :] = acc

    def token(t, _):
        cur = t % 2
        tn = jnp.minimum(t + 1, TB - 1)
        for hf in range(B_HEAD // IH):
            i0 = hf * IH
            sa = sa_s[cur, i0:i0 + IH, :]
            v = xv_ref[t, i0:i0 + IH, :]
            y = jnp.zeros((IH, 128), F32)
            san = jnp.zeros((IH, 128), F32)
            for j in range(B_HEAD):
                s_new = (S[j, i0:i0 + IH, :] * w_s[t, j:j + 1, :] + sa * b_s[t, j:j + 1, :]
                         + v * k_s[t, j:j + 1, :])
                S[j, i0:i0 + IH, :] = s_new
                y = y + s_new * xr_ref[t, j:j + 1, :]
                san = san + s_new * a_s[tn, j:j + 1, :]
            y_s[i0:i0 + IH, :] = y
            sa_s[1 - cur, i0:i0 + IH, :] = san
        y = y_s[...]
        mu = jnp.mean(y, axis=0, keepdims=True)
        yc = y - mu
        var = jnp.mean(yc * yc, axis=0, keepdims=True)
        bonus = jnp.sum(xr_ref[t] * k_s[t] * rk_ref[...], axis=0, keepdims=True)
        o_ref[t] = yc * lax.rsqrt(var + GN_EPS) * gnw_ref[...] + gnb_ref[...] + bonus * xv_ref[t]
        return 0

    lax.fori_loop(0, TB, token, 0)


def _rwkv_rec(xr, xk, xv, wp, ap, kk, ka, rk, gnw, gnb, s0, skip=0):
    t, _, c = xr.shape
    tok = pl.BlockSpec((TB, B_HEAD, 128), lambda g, tb: (tb, 0, g))
    par = pl.BlockSpec((B_HEAD, 128), lambda g, tb: (0, 0))
    st = pl.BlockSpec((B_HEAD, B_HEAD, 128), lambda g, tb: (0, 0, g))
    blk = pltpu.VMEM((TB, B_HEAD, 128), F32)
    return pl.pallas_call(
        functools.partial(_rwkv_rec_kernel, skip),
        grid=(c // 128, t // TB),
        in_specs=[tok] * 5 + [par] * 5 + [st],
        out_specs=[tok, st],
        out_shape=[jax.ShapeDtypeStruct((t, B_HEAD, c), F32), jax.ShapeDtypeStruct((B_HEAD, B_HEAD, c), F32)],
        scratch_shapes=[pltpu.VMEM((B_HEAD, B_HEAD, 128), F32), blk, blk, blk, blk,
                        pltpu.VMEM((2, B_HEAD, 128), F32), pltpu.VMEM((B_HEAD, 128), F32)],
        compiler_params=_cparams(("arbitrary", "arbitrary")),
        name="rwkv_rec",
    )(xr, xk, xv, wp, ap, kk, ka, rk, gnw, gnb, s0)


def _to_chains(a, nb, t):
    return a.reshape(nb, t, B_HEADS, B_HEAD).transpose(1, 3, 0, 2).reshape(t, B_HEAD, nb * B_HEADS)


def _from_chains(a, nb, t):
    return a.reshape(t, B_HEAD, nb, B_HEADS).transpose(2, 0, 3, 1).reshape(nb * t, D)


TM5 = 256


def _mix_kernel(yo_p_ref, yo_s_ref, g_p_ref, g_s_ref, oa_p_ref, oa_s_ref, ga_ref, gb_ref, x_ref,
                wo_ref, n2_ref, wr_ref, br_ref, x1_ref, h2_ref, te_ref, tg_ref):
    is_p = pl.program_id(0) < ROWS_P // TM5
    yo = jnp.where(is_p, yo_p_ref[...], yo_s_ref[...])
    g = jnp.where(is_p, g_p_ref[...], g_s_ref[...])
    oa = jnp.where(is_p, oa_p_ref[...], oa_s_ref[...])
    mixed = _sigmoid(ga_ref[...]) * oa + _sigmoid(gb_ref[...]) * (yo * g)
    x1 = x_ref[...] + jnp.dot(mixed.astype(BF16), wo_ref[...], preferred_element_type=F32)
    x1_ref[...] = x1
    ms = jnp.mean(x1 * x1, axis=-1, keepdims=True)
    h2 = (x1 * lax.rsqrt(ms + RMS_EPS)) * n2_ref[...]
    h2_ref[...] = h2
    logits = jnp.dot(h2, wr_ref[...], preferred_element_type=F32, precision=lax.Precision.HIGHEST) + br_ref[...]
    lane = lax.broadcasted_iota(jnp.int32, logits.shape, 1)
    lane_f = lane.astype(F32)
    te = jnp.zeros(logits.shape, jnp.int32)
    tv = jnp.full(logits.shape, NEG, F32)
    for k in range(TOP_K):
        m = jnp.max(logits, axis=-1, keepdims=True)
        idx = jnp.min(jnp.where(logits == m, lane_f, 128.0), axis=-1, keepdims=True).astype(jnp.int32)
        te = jnp.where(lane == k, idx, te)
        tv = jnp.where(lane == k, m, tv)
        logits = jnp.where(lane == idx, NEG, logits)
    e = jnp.where(lane < TOP_K, jnp.exp(tv - jnp.max(tv, axis=-1, keepdims=True)), 0.0)
    te_ref[...] = te
    tg_ref[...] = e / jnp.sum(e, axis=-1, keepdims=True)


def _mix(yo_p, yo_s, g_p, g_s, oa_p, oa_s, p_all, x_all, wo, n2, wr, br):
    nbp = ROWS_P // TM5
    rb = lambda i: (i, 0)
    pb = pl.BlockSpec((TM5, D), lambda i: (jnp.minimum(i, nbp - 1), 0))
    sb = pl.BlockSpec((TM5, D), lambda i: (jnp.maximum(i - nbp, 0), 0))
    full = lambda shp: pl.BlockSpec(shp, lambda i: tuple(0 for _ in shp))
    return pl.pallas_call(
        _mix_kernel,
        grid=(ROWS // TM5,),
        in_specs=[pb, sb, pb, sb, pb, sb,
                  pl.BlockSpec((TM5, D), lambda i: (i, OFF_GA // D)),
                  pl.BlockSpec((TM5, D), lambda i: (i, OFF_GB // D)),
                  pl.BlockSpec((TM5, D), rb),
                  full((D, D)), full((1, D)), full((D, 128)), full((1, 128))],
        out_specs=[pl.BlockSpec((TM5, D), rb), pl.BlockSpec((TM5, D), rb),
                   pl.BlockSpec((TM5, 128), rb), pl.BlockSpec((TM5, 128), rb)],
        out_shape=[jax.ShapeDtypeStruct((ROWS, D), F32), jax.ShapeDtypeStruct((ROWS, D), F32),
                   jax.ShapeDtypeStruct((ROWS, 128), jnp.int32), jax.ShapeDtypeStruct((ROWS, 128), F32)],
        compiler_params=_cparams(("arbitrary",)),
        name="mix",
    )(yo_p, yo_s, g_p, g_s, oa_p, oa_s, p_all, p_all, x_all, wo, n2, wr, br)


N_TOK = NB * (N_META + SEQ) + ROWS_S
NK = N_TOK * TOP_K
TMM = 256
N_BLK = -(-NK // TMM) + N_EXPERTS
NSLOT = N_BLK * TMM
TN_G1 = 1024
TN_G2 = 1024
TMC = 128


def _gather_kernel(slot_ref, nused_ref, h_hbm, o_ref, buf, sem):
    m = pl.program_id(0)

    @pl.when(m < nused_ref[0])
    def _():
        def issue(r, _):
            tok = slot_ref[m * TMM + r]
            pltpu.make_async_copy(h_hbm.at[pl.ds(tok, 1), :], buf.at[pl.ds(r, 1), :], sem).start()
            return 0

        lax.fori_loop(0, TMM, issue, 0)
        pltpu.make_async_copy(h_hbm.at[pl.ds(0, TMM), :], buf, sem).wait()
        o_ref[...] = buf[...].astype(BF16)

    @pl.when(m >= nused_ref[0])
    def _():
        o_ref[...] = jnp.zeros(o_ref.shape, BF16)


def _gather(slot_tok, nused, h2):
    return pl.pallas_call(
        _gather_kernel,
        grid_spec=pltpu.PrefetchScalarGridSpec(
            num_scalar_prefetch=2,
            grid=(N_BLK,),
            in_specs=[pl.BlockSpec(memory_space=pl.ANY)],
            out_specs=pl.BlockSpec((TMM, D), lambda m, st, nu: (m, 0)),
            scratch_shapes=[pltpu.VMEM((TMM, D), F32), pltpu.SemaphoreType.DMA(())]),
        out_shape=jax.ShapeDtypeStruct((NSLOT, D), BF16),
        compiler_params=_cparams(("arbitrary",)),
        name="moe_gather",
    )(slot_tok, nused, h2)


def _new_expert(blk_e_ref, m):
    return (m == 0) | (blk_e_ref[m] != blk_e_ref[jnp.maximum(m - 1, 0)])


def _gm1_kernel(blk_e_ref, nused_ref, x_ref, w_ref, b_ref, sel_ref, o_ref, wb_scr):
    m = pl.program_id(1)
    used = m < nused_ref[0]

    @pl.when(used & _new_expert(blk_e_ref, m))
    def _():
        wb_scr[...] = w_ref[...].astype(BF16)

    @pl.when(used)
    def _():
        hd = jnp.dot(x_ref[...], wb_scr[...], preferred_element_type=F32) + b_ref[...]
        glu = jnp.minimum(hd, SWIGLU_LIMIT)
        lin = jnp.clip(hd, -SWIGLU_LIMIT, SWIGLU_LIMIT) + 1.0
        act = glu * _sigmoid(SWIGLU_ALPHA * glu)
        for c in range(TN_G1 // 512):
            parts = []
            for q in range(4):
                sl = slice(c * 512 + q * 128, c * 512 + (q + 1) * 128)
                parts.append(act[:, sl] * pltpu.roll(lin[:, sl], 127, axis=1))
            z = jnp.concatenate(parts, axis=1).astype(BF16)
            o_ref[:, c * 256:(c + 1) * 256] = jnp.dot(z, sel_ref[...], preferred_element_type=F32).astype(BF16)

    @pl.when(jnp.logical_not(used))
    def _():
        o_ref[...] = jnp.zeros(o_ref.shape, BF16)


def _gm1(blk_e, nused, xs, w1, b1, sel):
    def meff(m, nu):
        return jnp.minimum(m, nu[0] - 1)

    return pl.pallas_call(
        _gm1_kernel,
        grid_spec=pltpu.PrefetchScalarGridSpec(
            num_scalar_prefetch=2,
            grid=(2 * D_FF // TN_G1, N_BLK),
            in_specs=[pl.BlockSpec((TMM, D), lambda f, m, be, nu: (meff(m, nu), 0)),
                      pl.BlockSpec((None, D, TN_G1), lambda f, m, be, nu: (be[meff(m, nu)], 0, f)),
                      pl.BlockSpec((None, 1, TN_G1), lambda f, m, be, nu: (be[meff(m, nu)], 0, f)),
                      pl.BlockSpec((512, 256), lambda f, m, be, nu: (0, 0))],
            out_specs=pl.BlockSpec((TMM, TN_G1 // 2), lambda f, m, be, nu: (m, f)),
            scratch_shapes=[pltpu.VMEM((D, TN_G1), BF16)]),
        out_shape=jax.ShapeDtypeStruct((NSLOT, D_FF), BF16),
        compiler_params=_cparams(("arbitrary", "arbitrary")),
        name="moe_up",
    )(blk_e, nused, xs, w1, b1, sel)


def _gm2_kernel(blk_e_ref, nused_ref, x_ref, w_ref, b_ref, o_ref, wb_scr):
    m = pl.program_id(1)
    used = m < nused_ref[0]

    @pl.when(used & _new_expert(blk_e_ref, m))
    def _():
        wb_scr[...] = w_ref[...].astype(BF16)

    @pl.when(used)
    def _():
        o_ref[...] = jnp.dot(x_ref[...], wb_scr[...], preferred_element_type=F32) + b_ref[...]

    @pl.when(jnp.logical_not(used))
    def _():
        o_ref[...] = jnp.zeros(o_ref.shape, F32)


def _gm2(blk_e, nused, hid, w2, b2):
    def meff(m, nu):
        return jnp.minimum(m, nu[0] - 1)

    return pl.pallas_call(
        _gm2_kernel,
        grid_spec=pltpu.PrefetchScalarGridSpec(
            num_scalar_prefetch=2,
            grid=(D // TN_G2, N_BLK),
            in_specs=[pl.BlockSpec((TMM, D_FF), lambda f, m, be, nu: (meff(m, nu), 0)),
                      pl.BlockSpec((None, D_FF, TN_G2), lambda f, m, be, nu: (be[meff(m, nu)], 0, f)),
                      pl.BlockSpec((None, 1, TN_G2), lambda f, m, be, nu: (be[meff(m, nu)], 0, f))],
            out_specs=pl.BlockSpec((TMM, TN_G2), lambda f, m, be, nu: (m, f)),
            scratch_shapes=[pltpu.VMEM((D_FF, TN_G2), BF16)]),
        out_shape=jax.ShapeDtypeStruct((NSLOT, D), F32),
        compiler_params=_cparams(("arbitrary", "arbitrary")),
        name="moe_down",
    )(blk_e, nused, hid, w2, b2)


def _combine_kernel(dest_ref, x1_ref, tg_ref, fg_ref, y_hbm, o_ref, buf, sem):
    m = pl.program_id(0)

    def issue(r, _):
        for k in range(TOP_K):
            d = dest_ref[(m * TMC + r) * TOP_K + k]
            pltpu.make_async_copy(y_hbm.at[pl.ds(d, 1), :], buf.at[pl.ds(k * TMC + r, 1), :], sem).start()
        return 0

    lax.fori_loop(0, TMC, issue, 0)
    pltpu.make_async_copy(y_hbm.at[pl.ds(0, TOP_K * TMC), :], buf, sem).wait()
    x2 = x1_ref[...]
    tg = tg_ref[...]
    for k in range(TOP_K):
        x2 = x2 + tg[:, k:k + 1] * buf[k * TMC:(k + 1) * TMC, :]
    ms = jnp.mean(x2 * x2, axis=-1, keepdims=True)
    o_ref[...] = (x2 * lax.rsqrt(ms + RMS_EPS)) * fg_ref[...]


def _combine(dest, x1, tg, fg, y):
    return pl.pallas_call(
        _combine_kernel,
        grid_spec=pltpu.PrefetchScalarGridSpec(
            num_scalar_prefetch=1,
            grid=(ROWS // TMC,),
            in_specs=[pl.BlockSpec((TMC, D), lambda m, d: (m, 0)),
                      pl.BlockSpec((TMC, 128), lambda m, d: (m, 0)),
                      pl.BlockSpec((1, D), lambda m, d: (0, 0)),
                      pl.BlockSpec(memory_space=pl.ANY)],
            out_specs=pl.BlockSpec((TMC, D), lambda m, d: (m, 0)),
            scratch_shapes=[pltpu.VMEM((TOP_K * TMC, D), F32), pltpu.SemaphoreType.DMA(())]),
        out_shape=jax.ShapeDtypeStruct((ROWS, D), F32),
        compiler_params=_cparams(("arbitrary",)),
        name="moe_combine",
    )(dest, x1, tg, fg, y)


def _t5_bucket(rel):
    nb = N_BUCKETS // 2
    ret = jnp.where(rel > 0, nb, 0)
    n = jnp.abs(rel)
    max_exact = nb // 2
    nf = jnp.maximum(n, 1).astype(jnp.float32)
    large = max_exact + (jnp.log(nf / max_exact) / math.log(MAX_DISTANCE / max_exact) * (nb - max_exact)).astype(jnp.int32)
    large = jnp.minimum(large, nb - 1)
    return ret + jnp.where(n < max_exact, n, large)


def _valid_rows():
    rows = [np.arange(b * T_PAD + FRONT, (b + 1) * T_PAD) for b in range(NB)]
    rows.append(np.arange(ROWS_P, ROWS))
    return np.concatenate(rows).astype(np.int32)


def kernel(x_prompt, x_sample, cache_k, cache_v, state_wkv, state_shift, meta_tokens, rel_bias_table, norm1_g, w_in, shift_mu, decay_base, w_decay_up, a_base, w_a_up, w_g_up, k_k, k_a, r_k, gn_w, gn_b, lam_q1, lam_k1, lam_q2, lam_k2, subln_g, w_out, norm2_g, w_router, b_router, w_e1, b_e1, w_e2, b_e2, final_g):
    xp = jnp.concatenate([jnp.zeros((NB, FRONT, D), F32),
                          jnp.broadcast_to(meta_tokens[None], (NB, N_META, D)), x_prompt], axis=1)
    x_all = jnp.concatenate([xp.reshape(ROWS_P, D), x_sample.reshape(ROWS_S, D)], axis=0)
    lora0 = OFF_R + 3 * D
    w_p = jnp.concatenate([w_in[0][:, :lora0], w_in[0][:, lora0 + LORA_W:], w_in[0][:, lora0:lora0 + LORA_W],
                           jnp.zeros((D, LORA_P - LORA_W), F32)], axis=1).astype(BF16)

    p_all = _inproj(x_all, norm1_g, w_p)

    kq = jnp.arange(TQ, dtype=jnp.int32)
    bkt = jnp.stack([_t5_bucket(kq[:, None] - kq[None, :]), _t5_bucket(kq[:, None] - kq[None, :] - TQ)])
    lamp = jnp.concatenate([lam_q1, lam_k1, lam_q2, lam_k2], axis=0)
    oa_p = _attn_prompt(p_all, rel_bias_table, bkt, lamp, subln_g)
    qpos = PAST + jnp.arange(DEC_S, dtype=jnp.int32)
    bkt_last = _t5_bucket((PAST - TKS + jnp.arange(TKS, dtype=jnp.int32))[None, :] - qpos[:, None])
    bkt_new = _t5_bucket(qpos[None, :] - qpos[:, None])
    oa_s = _attn_sample(p_all, cache_k[0].reshape(DEC_B, PAST * A_HEADS, A_VD),
                        cache_v[0].reshape(DEC_B, PAST * A_HEADS, A_VD),
                        rel_bias_table, bkt_last, bkt_new, lamp, subln_g)

    mu = shift_mu[0]
    mu3 = mu[:3 * D][None]
    mul = jnp.pad(mu[3 * D:], (0, LORA_P - LORA_W))[None]
    wd = jnp.zeros((LORA_P, D), F32).at[:DECAY_LORA].set(w_decay_up[0]).astype(BF16)
    wa = jnp.zeros((LORA_P, D), F32).at[DECAY_LORA:DECAY_LORA + AAA_LORA].set(w_a_up[0]).astype(BF16)
    wg = jnp.zeros((LORA_P, D), F32).at[DECAY_LORA + AAA_LORA:LORA_W].set(w_g_up[0]).astype(BF16)
    first_p = jnp.zeros((NB, 1, 3 * D + LORA_P), F32)
    first_s = jnp.pad(state_shift[0], ((0, 0), (0, LORA_P - LORA_W)))[:, None, :]
    prep_p = _rwkv_prep(p_all, first_p, mu3, mul, decay_base, a_base, wd, wa, wg, NB, T_PAD // 256, 256, 0)
    prep_s = _rwkv_prep(p_all, first_s, mu3, mul, decay_base, a_base, wd, wa, wg, DEC_B, 1, DEC_S, ROWS_P // DEC_S)

    def chain_tile(v):
        return jnp.tile(v.reshape(B_HEADS, B_HEAD).T, (1, 128 // B_HEADS))

    par = [chain_tile(v.reshape(-1)) for v in (k_k[0], k_a[0], r_k[0], gn_w[0], gn_b[0])]
    s0_p = jnp.zeros((B_HEAD, B_HEAD, NB * B_HEADS), F32)
    s0_s = state_wkv[0].transpose(3, 2, 0, 1).reshape(B_HEAD, B_HEAD, DEC_B * B_HEADS)
    yo_p, sfin_p = _rwkv_rec(*[_to_chains(a, NB, T_PAD) for a in prep_p[:5]], *par, s0_p, skip=FRONT // TB)
    yo_s, sfin_s = _rwkv_rec(*[_to_chains(a, DEC_B, DEC_S) for a in prep_s[:5]], *par, s0_s)

    wo = w_out[0].astype(BF16)
    wr = jnp.pad(w_router[0], ((0, 0), (0, 128 - N_EXPERTS)))
    br = jnp.concatenate([b_router[0], jnp.full((128 - N_EXPERTS,), NEG, F32)])[None]
    x1, h2, te, tg = _mix(_from_chains(yo_p, NB, T_PAD), _from_chains(yo_s, DEC_B, DEC_S), prep_p[5], prep_s[5],
                          oa_p, oa_s, p_all, x_all, wo, norm2_g, wr, br)

    valid = jnp.asarray(_valid_rows())
    flat_e = te[valid, :TOP_K].reshape(-1)
    order = jnp.argsort(flat_e)
    e_sorted = flat_e[order]
    tok_sorted = valid[order // TOP_K]
    counts = jnp.zeros((N_EXPERTS,), jnp.int32).at[flat_e].add(1)
    padded = (counts + TMM - 1) // TMM * TMM
    pad_end = jnp.cumsum(padded)
    pad_start = pad_end - padded
    grp_start = jnp.cumsum(counts) - counts
    dest_sorted = pad_start[e_sorted] + jnp.arange(NK, dtype=jnp.int32) - grp_start[e_sorted]
    slot_tok = jnp.zeros((NSLOT,), jnp.int32).at[dest_sorted].set(tok_sorted)
    blk_start = jnp.arange(N_BLK, dtype=jnp.int32) * TMM
    blk_e = jnp.minimum(jnp.searchsorted(pad_end, blk_start, side='right'), N_EXPERTS - 1).astype(jnp.int32)
    nused = (pad_end[-1:] // TMM).astype(jnp.int32)
    dest_tok = jnp.zeros((NK,), jnp.int32).at[order].set(dest_sorted).reshape(N_TOK, TOP_K)
    dest = jnp.zeros((ROWS, TOP_K), jnp.int32).at[valid].set(dest_tok).reshape(-1)

    sel_np = np.zeros((512, 256), np.float32)
    sel_np[2 * np.arange(256), np.arange(256)] = 1.0
    xs = _gather(slot_tok, nused, h2)
    hid = _gm1(blk_e, nused, xs, w_e1[0], b_e1[0][:, None, :], jnp.asarray(sel_np, BF16))
    ys = _gm2(blk_e, nused, hid, w_e2[0], b_e2[0][:, None, :])
    y_all = _combine(dest, x1, tg, final_g[None], ys)

    y_prompt = y_all[:ROWS_P].reshape(NB, T_PAD, D)[:, FRONT + N_META:]
    y_sample = y_all[ROWS_P:].reshape(DEC_B, DEC_S, D)
    def prompt_rows(off):
        sec = p_all[:ROWS_P, off:off + D].reshape(NB, T_PAD, D)[:, FRONT:]
        return sec.reshape(1, NB, N_META + SEQ, A_HEADS, A_VD)

    def sample_rows(off):
        return p_all[ROWS_P:, off:off + D].reshape(1, DEC_B, DEC_S, A_HEADS, A_VD)

    def last_ps(rows):
        return jnp.concatenate([rows[:, OFF_R:OFF_R + 3 * D], rows[:, OFF_LORA:OFF_LORA + LORA_W]], axis=-1)[None]

    k_prompt, v_prompt = prompt_rows(OFF_K), prompt_rows(OFF_V)
    k_sample, v_sample = sample_rows(OFF_K), sample_rows(OFF_V)
    shift_prompt = last_ps(p_all[T_PAD - 1:ROWS_P:T_PAD])
    shift_sample = last_ps(p_all[ROWS_P + DEC_S - 1::DEC_S])
    wkv_prompt = sfin_p.reshape(B_HEAD, B_HEAD, NB, B_HEADS).transpose(2, 3, 1, 0)[None]
    wkv_sample = sfin_s.reshape(B_HEAD, B_HEAD, DEC_B, B_HEADS).transpose(2, 3, 1, 0)[None].astype(state_wkv.dtype)
    return (y_prompt, y_sample, k_prompt, v_prompt, wkv_prompt, shift_prompt,
            k_sample, v_sample, wkv_sample, shift_sample)
```

```python
import functools
import math

import numpy as np
import jax
import jax.numpy as jnp
from jax import lax
from jax.experimental import pallas as pl
from jax.experimental.pallas import tpu as pltpu

F32 = jnp.float32
BF16 = jnp.bfloat16

D = 2048
NB = 4
SEQ = 2048
DEC_B = 16
DEC_S = 32
PAST = 2048
CHUNK = 64
N_META = 16
RMS_EPS = 1e-5
A_DH = 64
A_HEADS = 16
A_VD = 128
B_HEAD = 64
B_HEADS = 32
DECAY_LORA = 96
AAA_LORA = 96
GATE_LORA = 256
GN_EPS = 64e-5
N_BUCKETS = 32
MAX_DISTANCE = 128
N_EXPERTS = 32
TOP_K = 4
D_FF = 2048
SWIGLU_ALPHA = 1.702
SWIGLU_LIMIT = 7.0
LAM_INIT = 0.8 - 0.6 * math.exp(-0.3 * 0)
SHIFT_W = 3 * D + DECAY_LORA + AAA_LORA + GATE_LORA
LORA_W = DECAY_LORA + AAA_LORA + GATE_LORA
LORA_P = 512

T_PAD = 2304
FRONT = T_PAD - N_META - SEQ
ROWS_P = NB * T_PAD
ROWS_S = DEC_B * DEC_S
ROWS = ROWS_P + ROWS_S

OFF_Q = 0
OFF_K = 2048
OFF_V = 4096
OFF_R = 6144
OFF_KX = 8192
OFF_VX = 10240
OFF_GA = 12288
OFF_GB = 14336
OFF_LORA = 16384
PW = OFF_LORA + LORA_P

NEG = -1e30
LOG2E = 1.4426950408889634
VMEM_LIMIT = 56 * 1024 * 1024


def _cparams(sem, vmem=VMEM_LIMIT):
    return pltpu.CompilerParams(dimension_semantics=sem, vmem_limit_bytes=vmem)


def _sigmoid(x):
    return 1.0 / (1.0 + jnp.exp(-x))


TM1 = 512
TN1 = 512


def _inproj_kernel(x_ref, g_ref, w_ref, o_ref, h_scr):
    @pl.when(pl.program_id(1) == 0)
    def _():
        x = x_ref[...]
        ms = jnp.mean(x * x, axis=-1, keepdims=True)
        h_scr[...] = ((x * lax.rsqrt(ms + RMS_EPS)) * g_ref[...]).astype(BF16)

    o_ref[...] = jnp.dot(h_scr[...], w_ref[...], preferred_element_type=F32)


def _inproj(x_all, g, w):
    return pl.pallas_call(
        _inproj_kernel,
        grid=(ROWS // TM1, PW // TN1),
        in_specs=[pl.BlockSpec((TM1, D), lambda i, j: (i, 0)),
                  pl.BlockSpec((1, D), lambda i, j: (0, 0)),
                  pl.BlockSpec((D, TN1), lambda i, j: (0, j))],
        out_specs=pl.BlockSpec((TM1, TN1), lambda i, j: (i, j)),
        out_shape=jax.ShapeDtypeStruct((ROWS, PW), F32),
        scratch_shapes=[pltpu.VMEM((TM1, D), BF16)],
        compiler_params=_cparams(("arbitrary", "arbitrary")),
        name="inproj",
    )(x_all, g, w)


def _lam_value(lamp_ref):
    lp = lamp_ref[...]
    s1 = jnp.sum(lp[0:1, :] * lp[1:2, :], axis=-1, keepdims=True)
    s2 = jnp.sum(lp[2:3, :] * lp[3:4, :], axis=-1, keepdims=True)
    return jnp.exp(s1) - jnp.exp(s2) + LAM_INIT


def _bias_from_buckets(bkt, table_ref, h):
    out = jnp.zeros(bkt.shape, F32)
    for n in range(N_BUCKETS):
        out = jnp.where(bkt == n, table_ref[n, h], out)
    return out


TQ = 256
NQB = T_PAD // TQ
BIAS_META, BIAS_DIAG, BIAS_SUB, BIAS_SUB_PAD, BIAS_FAR_PAD, BIAS_FAR = range(6)


def _attn_prompt_kernel(table_ref, q_ref, k_ref, v_ref, bkt_ref, lamp_ref, subg_ref, o_ref,
                        kb_scr, vt_scr, bias_scr, s_scr, acc_scr):
    h = pl.program_id(0)
    b = pl.program_id(1)

    @pl.when(b == 0)
    def _():
        kk = lax.broadcasted_iota(jnp.int32, (TQ, TQ), 0)
        qq = lax.broadcasted_iota(jnp.int32, (TQ, TQ), 1)
        chunk_ok = (kk // CHUNK) <= (qq // CHUNK)
        pad = jnp.where(kk < FRONT, NEG, 0.0).astype(F32)
        b_diag = _bias_from_buckets(bkt_ref[0], table_ref, h)
        b_sub = _bias_from_buckets(bkt_ref[1], table_ref, h)
        b_far = jnp.zeros((TQ, TQ), F32) + table_ref[N_BUCKETS // 2 - 1, h]
        tiles = (jnp.where(chunk_ok, b_diag, NEG) + pad,
                 jnp.where(chunk_ok, b_diag, NEG),
                 b_sub,
                 b_sub + pad,
                 b_far + pad,
                 b_far)
        for i, t in enumerate(tiles):
            bias_scr[i] = jnp.concatenate([t, t], axis=1) * LOG2E

    kb_scr[...] = k_ref[...].astype(BF16)
    for i in range(NQB):
        vt_scr[i] = v_ref[i * TQ:(i + 1) * TQ, :].T.astype(BF16)

    lam = _lam_value(lamp_ref)
    drow = lax.broadcasted_iota(jnp.int32, (A_VD, TQ), 0)

    def q_operand(qi):
        q0 = pl.multiple_of(qi * TQ, TQ)
        qt = (q_ref[pl.ds(q0, TQ), :] * (A_DH ** -0.5 * LOG2E)).T
        q1 = jnp.where(drow < A_DH, qt, 0.0)
        q2 = jnp.where(drow >= A_DH, qt, 0.0)
        return jnp.concatenate([q1, q2], axis=1).astype(BF16)

    def scores(kj, qpt, slot):
        k0 = pl.multiple_of(kj * TQ, TQ)
        s_scr[slot] = jnp.dot(kb_scr[pl.ds(k0, TQ), :], qpt, preferred_element_type=F32)

    def accumulate(kj, bias_id, slot, ml):
        m, l = ml
        s = s_scr[slot] + bias_scr[bias_id]
        m_new = jnp.maximum(m, jnp.max(s, axis=0, keepdims=True))
        alpha = jnp.exp2(m - m_new)
        p = jnp.exp2(s - m_new)
        l = l * alpha + jnp.sum(p, axis=0, keepdims=True)
        acc_scr[...] = acc_scr[...] * alpha + jnp.dot(vt_scr[kj], p.astype(BF16), preferred_element_type=F32)
        return m_new, l

    def finish(qi, ml):
        on = acc_scr[...] / ml[1]
        ot = on[:, :TQ] - lam * on[:, TQ:]
        ms = jnp.mean(ot * ot, axis=0, keepdims=True)
        o = (ot * lax.rsqrt(ms + RMS_EPS)).T * subg_ref[...] * (1.0 - LAM_INIT)
        q0 = pl.multiple_of(qi * TQ, TQ)
        o_ref[pl.ds(q0, TQ), :] = o

    def start(qi):
        qpt = q_operand(qi)
        acc_scr[...] = jnp.zeros(acc_scr.shape, F32)
        scores(0, qpt, 0)
        return qpt, (jnp.full((1, 2 * TQ), NEG, F32), jnp.zeros((1, 2 * TQ), F32))

    finish(0, accumulate(0, BIAS_META, 0, start(0)[1]))

    def q_block(qi, _):
        qpt, ml = start(qi)

        def half(kj, cur, ml):
            scores(kj + 1, qpt, 1 - cur)
            first = kj == 0
            bias_id = jnp.where(kj == qi - 1, jnp.where(first, BIAS_SUB_PAD, BIAS_SUB),
                                jnp.where(first, BIAS_FAR_PAD, BIAS_FAR))
            return accumulate(kj, bias_id, cur, ml)

        ml = lax.fori_loop(0, qi // 2, lambda i, ml: half(2 * i + 1, 1, half(2 * i, 0, ml)), ml)
        ml = lax.cond(qi % 2 == 1,
                      lambda ml: accumulate(qi, BIAS_DIAG, 1, half(qi - 1, 0, ml)),
                      lambda ml: accumulate(qi, BIAS_DIAG, 0, ml), ml)
        finish(qi, ml)
        return 0

    lax.fori_loop(1, NQB, q_block, 0)


def _attn_prompt(p_all, table, bkt, lamp, subg):
    cb = D // A_VD
    return pl.pallas_call(
        _attn_prompt_kernel,
        grid_spec=pltpu.PrefetchScalarGridSpec(
            num_scalar_prefetch=0,
            grid=(A_HEADS, NB),
            in_specs=[pl.BlockSpec(memory_space=pltpu.SMEM),
                      pl.BlockSpec((T_PAD, A_VD), lambda h, b: (b, h)),
                      pl.BlockSpec((T_PAD, A_VD), lambda h, b: (b, cb + h)),
                      pl.BlockSpec((T_PAD, A_VD), lambda h, b: (b, 2 * cb + h)),
                      pl.BlockSpec((2, TQ, TQ), lambda h, b: (0, 0, 0)),
                      pl.BlockSpec((4, A_DH), lambda h, b: (0, 0)),
                      pl.BlockSpec((1, A_VD), lambda h, b: (0, 0))],
            out_specs=pl.BlockSpec((T_PAD, A_VD), lambda h, b: (b, h)),
            scratch_shapes=[pltpu.VMEM((T_PAD, A_VD), BF16),
                            pltpu.VMEM((NQB, A_VD, TQ), BF16),
                            pltpu.VMEM((6, TQ, 2 * TQ), F32),
                            pltpu.VMEM((2, TQ, 2 * TQ), F32),
                            pltpu.VMEM((A_VD, 2 * TQ), F32)]),
        out_shape=jax.ShapeDtypeStruct((ROWS_P, D), F32),
        compiler_params=_cparams(("arbitrary", "arbitrary")),
        name="attn_prompt",
    )(table, p_all, p_all, p_all, bkt, lamp, subg)


HG = 4
NEAR = 512


def _attn_sample_kernel(table_ref, q_ref, kn_ref, vn_ref, bl_ref, bn_ref, lamp_ref, subg_ref, ck_hbm, cv_hbm,
                        o_ref, kbuf, vbuf, sem):
    hg = pl.program_id(1)
    ng = pl.num_programs(1)
    step = pl.program_id(0) * ng + hg
    nsteps = pl.num_programs(0) * ng

    def copies(n, slot):
        out = []
        for i in range(HG):
            h = (n % ng) * HG + i
            out.append(pltpu.make_async_copy(ck_hbm.at[n // ng, :, h, :], kbuf.at[slot, i], sem.at[slot, i]))
            out.append(pltpu.make_async_copy(cv_hbm.at[n // ng, :, h, :], vbuf.at[slot, i], sem.at[slot, HG + i]))
        return out

    cur = step % 2

    @pl.when(step == 0)
    def _():
        for c in copies(step, 0):
            c.start()

    @pl.when(step + 1 < nsteps)
    def _():
        for c in copies(step + 1, 1 - cur):
            c.start()

    for c in copies(step, cur):
        c.wait()
    ck_refs = [kbuf.at[cur, i] for i in range(HG)]
    cv_refs = [vbuf.at[cur, i] for i in range(HG)]
    lam = _lam_value(lamp_ref)
    lane = lax.broadcasted_iota(jnp.int32, (DEC_S, A_VD), 1)
    nt = (((1,), (1,)), ((), ()))
    for i in range(HG):
        h = hg * HG + i
        cols = slice(i * A_VD, (i + 1) * A_VD)
        q = q_ref[:, cols] * (A_DH ** -0.5)
        qp = jnp.concatenate([jnp.where(lane < A_DH, q, 0.0), jnp.where(lane >= A_DH, q, 0.0)],
                             axis=0).astype(BF16)
        far = jnp.zeros((DEC_S, PAST - NEAR), F32) + table_ref[N_BUCKETS // 2 - 1, h]
        bl = jnp.concatenate([far, _bias_from_buckets(bl_ref[...], table_ref, h)], axis=1)
        bn = _bias_from_buckets(bn_ref[...], table_ref, h)
        s = (lax.dot_general(qp, ck_refs[i][...].astype(BF16), nt, preferred_element_type=F32)
             + jnp.concatenate([bl, bl], axis=0))
        sn = (lax.dot_general(qp, kn_ref[:, cols].astype(BF16), nt, preferred_element_type=F32)
              + jnp.concatenate([bn, bn], axis=0))
        m = jnp.maximum(jnp.max(s, axis=-1, keepdims=True), jnp.max(sn, axis=-1, keepdims=True))
        p = jnp.exp(s - m)
        pn = jnp.exp(sn - m)
        l = jnp.sum(p, axis=-1, keepdims=True) + jnp.sum(pn, axis=-1, keepdims=True)
        acc = (jnp.dot(p.astype(BF16), cv_refs[i][...].astype(BF16), preferred_element_type=F32)
               + jnp.dot(pn.astype(BF16), vn_ref[:, cols].astype(BF16), preferred_element_type=F32))
        on = acc / l
        o = on[:DEC_S] - lam * on[DEC_S:]
        ms = jnp.mean(o * o, axis=-1, keepdims=True)
        o_ref[:, cols] = (o * lax.rsqrt(ms + RMS_EPS)) * subg_ref[...] * (1.0 - LAM_INIT)


def _attn_sample(p_all, cache_k, cache_v, table, bkt_last, bkt_new, lamp, subg):
    r0 = ROWS_P // DEC_S
    w = HG * A_VD
    new = lambda off: pl.BlockSpec((DEC_S, w), lambda b, g: (r0 + b, off // w + g))
    return pl.pallas_call(
        _attn_sample_kernel,
        grid=(DEC_B, A_HEADS // HG),
        in_specs=[pl.BlockSpec(memory_space=pltpu.SMEM), new(OFF_Q), new(OFF_K), new(OFF_V),
                  pl.BlockSpec((DEC_S, NEAR), lambda b, g: (0, 0)),
                  pl.BlockSpec((DEC_S, DEC_S), lambda b, g: (0, 0)),
                  pl.BlockSpec((4, A_DH), lambda b, g: (0, 0)),
                  pl.BlockSpec((1, A_VD), lambda b, g: (0, 0)),
                  pl.BlockSpec(memory_space=pl.ANY), pl.BlockSpec(memory_space=pl.ANY)],
        out_specs=pl.BlockSpec((DEC_S, w), lambda b, g: (b, g)),
        out_shape=jax.ShapeDtypeStruct((ROWS_S, D), F32),
        scratch_shapes=[pltpu.VMEM((2, HG, PAST, A_VD), F32), pltpu.VMEM((2, HG, PAST, A_VD), F32),
                        pltpu.SemaphoreType.DMA((2, 2 * HG))],
        compiler_params=_cparams(("arbitrary", "arbitrary")),
        name="attn_sample",
    )(table, p_all, p_all, p_all, bkt_last, bkt_new, lamp, subg, cache_k, cache_v)


def _rwkv_prep_kernel(r_ref, k_ref, v_ref, lo_ref, first_ref, mu_ref, mul_ref, dbase_ref, abase_ref,
                      wd_ref, wa_ref, wg_ref,
                      xr_ref, xk_ref, xv_ref, wp_ref, ap_ref, g_ref, last_ref, carry_scr, carryl_scr):
    i = pl.program_id(1)
    tm = r_ref.shape[0]
    row = lax.broadcasted_iota(jnp.int32, (tm, 1), 0)

    def shift(x, first, mu):
        prev = jnp.where(row == 0, first, pltpu.roll(x, 1, axis=0))
        return x + (prev - x) * mu

    for s, (src, dst) in enumerate(((r_ref, xr_ref), (k_ref, xk_ref), (v_ref, xv_ref))):
        x = src[...]
        first = jnp.where(i == 0, first_ref[:, s * D:(s + 1) * D], carry_scr[:, s * D:(s + 1) * D])
        dst[...] = shift(x, first, mu_ref[:, s * D:(s + 1) * D])
        carry_scr[:, s * D:(s + 1) * D] = x[tm - 1:tm, :]
        last_ref[:, s * D:(s + 1) * D] = x[tm - 1:tm, :]

    xl = lo_ref[...]
    first = jnp.where(i == 0, first_ref[:, 3 * D:], carryl_scr[...])
    xs = shift(xl, first, mul_ref[...])
    carryl_scr[...] = xl[tm - 1:tm, :]
    last_ref[:, 3 * D:] = xl[tm - 1:tm, :]
    wp_ref[...] = dbase_ref[...] + jnp.dot(jnp.tanh(xs).astype(BF16), wd_ref[...], preferred_element_type=F32)
    ap_ref[...] = abase_ref[...] + jnp.dot(xs.astype(BF16), wa_ref[...], preferred_element_type=F32)
    g_ref[...] = jnp.dot(_sigmoid(xs).astype(BF16), wg_ref[...], preferred_element_type=F32)


def _rwkv_prep(p_all, first, mu, mul, dbase, abase, wd, wa, wg, nseq, nblk, tm, row_blk0):
    rows = nseq * nblk * tm
    cs = lambda off: (lambda s, i: (row_blk0 + s * nblk + i, off))
    full = lambda shp: pl.BlockSpec(shp, lambda s, i: tuple(0 for _ in shp))
    out = jax.ShapeDtypeStruct((rows, D), F32)
    ospec = pl.BlockSpec((tm, D), lambda s, i: (s * nblk + i, 0))
    return pl.pallas_call(
        _rwkv_prep_kernel,
        grid=(nseq, nblk),
        in_specs=[pl.BlockSpec((tm, D), cs(OFF_R // D)),
                  pl.BlockSpec((tm, D), cs(OFF_KX // D)),
                  pl.BlockSpec((tm, D), cs(OFF_VX // D)),
                  pl.BlockSpec((tm, LORA_P), cs(OFF_LORA // LORA_P)),
                  pl.BlockSpec((None, 1, 3 * D + LORA_P), lambda s, i: (s, 0, 0)),
                  full((1, 3 * D)), full((1, LORA_P)), full((1, D)), full((1, D)),
                  full((LORA_P, D)), full((LORA_P, D)), full((LORA_P, D))],
        out_specs=[ospec] * 6 + [pl.BlockSpec((None, 1, 3 * D + LORA_P), lambda s, i: (s, 0, 0))],
        out_shape=[out] * 6 + [jax.ShapeDtypeStruct((nseq, 1, 3 * D + LORA_P), F32)],
        scratch_shapes=[pltpu.VMEM((1, 3 * D), F32), pltpu.VMEM((1, LORA_P), F32)],
        compiler_params=_cparams(("arbitrary", "arbitrary")),
        name="rwkv_prep",
    )(p_all, p_all, p_all, p_all, first, mu, mul, dbase, abase, wd, wa, wg)


TB = 32
IH = 32


def _rwkv_rec_kernel(skip, *refs):
    tb = pl.program_id(1)
    s0_ref, o_ref, sfin_ref, S = refs[10], refs[11], refs[12], refs[13]

    @pl.when(tb == 0)
    def _():
        S[...] = s0_ref[...]

    if skip:
        @pl.when(tb < skip)
        def _():
            o_ref[...] = jnp.zeros(o_ref.shape, F32)

        pl.when(tb >= skip)(functools.partial(_rwkv_rec_block, *refs))
    else:
        _rwkv_rec_block(*refs)
    sfin_ref[...] = S[...]


def _rwkv_rec_block(xr_ref, xk_ref, xv_ref, wp_ref, ap_ref, kk_ref, ka_ref, rk_ref, gnw_ref, gnb_ref, s0_ref,
                    o_ref, sfin_ref, S, w_s, k_s, a_s, b_s, sa_s, y_s):
    z = -wp_ref[...]
    softplus = jnp.maximum(z, 0.0) + jnp.log(1.0 + jnp.exp(-jnp.abs(z)))
    w_s[...] = jnp.exp(-jnp.exp(-softplus - 0.5))
    a = _sigmoid(ap_ref[...])
    xk = xk_ref[...]
    kk = xk * kk_ref[...]
    nrm = jnp.sqrt(jnp.sum(kk * kk, axis=1, keepdims=True))
    kk = kk / jnp.maximum(nrm, 1e-12)
    k_s[...] = xk * (1.0 + (a - 1.0) * ka_ref[...])
    a_s[...] = -kk
    b_s[...] = kk * a

    for hf in range(B_HEAD // IH):
        i0 = hf * IH
        acc = jnp.zeros((IH, 128), F32)
        for j in range(B_HEAD):
            acc = acc + S[j, i0:i0 + IH, :] * a_s[0, j:j + 1, :]
        sa_s[0, i0:i0 + IH, :] = acc

    def token(t, _):
        cur = t % 2
        tn = jnp.minimum(t + 1, TB - 1)
        for hf in range(B_HEAD // IH):
            i0 = hf * IH
            sa = sa_s[cur, i0:i0 + IH, :]
            v = xv_ref[t, i0:i0 + IH, :]
            y = jnp.zeros((IH, 128), F32)
            san = jnp.zeros((IH, 128), F32)
            for j in range(B_HEAD):
                s_new = (S[j, i0:i0 + IH, :] * w_s[t, j:j + 1, :] + sa * b_s[t, j:j + 1, :]
                         + v * k_s[t, j:j + 1, :])
                S[j, i0:i0 + IH, :] = s_new
                y = y + s_new * xr_ref[t, j:j + 1, :]
                san = san + s_new * a_s[tn, j:j + 1, :]
            y_s[i0:i0 + IH, :] = y
            sa_s[1 - cur, i0:i0 + IH, :] = san
        y = y_s[...]
        mu = jnp.mean(y, axis=0, keepdims=True)
        yc = y - mu
        var = jnp.mean(yc * yc, axis=0, keepdims=True)
        bonus = jnp.sum(xr_ref[t] * k_s[t] * rk_ref[...], axis=0, keepdims=True)
        o_ref[t] = yc * lax.rsqrt(var + GN_EPS) * gnw_ref[...] + gnb_ref[...] + bonus * xv_ref[t]
        return 0

    lax.fori_loop(0, TB, token, 0)


def _rwkv_rec(xr, xk, xv, wp, ap, kk, ka, rk, gnw, gnb, s0, skip=0):
    t, _, c = xr.shape
    tok = pl.BlockSpec((TB, B_HEAD, 128), lambda g, tb: (tb, 0, g))
    par = pl.BlockSpec((B_HEAD, 128), lambda g, tb: (0, 0))
    st = pl.BlockSpec((B_HEAD, B_HEAD, 128), lambda g, tb: (0, 0, g))
    blk = pltpu.VMEM((TB, B_HEAD, 128), F32)
    return pl.pallas_call(
        functools.partial(_rwkv_rec_kernel, skip),
        grid=(c // 128, t // TB),
        in_specs=[tok] * 5 + [par] * 5 + [st],
        out_specs=[tok, st],
        out_shape=[jax.ShapeDtypeStruct((t, B_HEAD, c), F32), jax.ShapeDtypeStruct((B_HEAD, B_HEAD, c), F32)],
        scratch_shapes=[pltpu.VMEM((B_HEAD, B_HEAD, 128), F32), blk, blk, blk, blk,
                        pltpu.VMEM((2, B_HEAD, 128), F32), pltpu.VMEM((B_HEAD, 128), F32)],
        compiler_params=_cparams(("arbitrary", "arbitrary")),
        name="rwkv_rec",
    )(xr, xk, xv, wp, ap, kk, ka, rk, gnw, gnb, s0)


def _to_chains(a, nb, t):
    return a.reshape(nb, t, B_HEADS, B_HEAD).transpose(1, 3, 0, 2).reshape(t, B_HEAD, nb * B_HEADS)


def _from_chains(a, nb, t):
    return a.reshape(t, B_HEAD, nb, B_HEADS).transpose(2, 0, 3, 1).reshape(nb * t, D)


TM5 = 256


def _mix_kernel(yo_p_ref, yo_s_ref, g_p_ref, g_s_ref, oa_p_ref, oa_s_ref, ga_ref, gb_ref, x_ref,
                wo_ref, n2_ref, wr_ref, br_ref, x1_ref, h2_ref, te_ref, tg_ref):
    is_p = pl.program_id(0) < ROWS_P // TM5
    yo = jnp.where(is_p, yo_p_ref[...], yo_s_ref[...])
    g = jnp.where(is_p, g_p_ref[...], g_s_ref[...])
    oa = jnp.where(is_p, oa_p_ref[...], oa_s_ref[...])
    mixed = _sigmoid(ga_ref[...]) * oa + _sigmoid(gb_ref[...]) * (yo * g)
    x1 = x_ref[...] + jnp.dot(mixed.astype(BF16), wo_ref[...], preferred_element_type=F32)
    x1_ref[...] = x1
    ms = jnp.mean(x1 * x1, axis=-1, keepdims=True)
    h2 = (x1 * lax.rsqrt(ms + RMS_EPS)) * n2_ref[...]
    h2_ref[...] = h2
    logits = jnp.dot(h2, wr_ref[...], preferred_element_type=F32, precision=lax.Precision.HIGHEST) + br_ref[...]
    lane = lax.broadcasted_iota(jnp.int32, logits.shape, 1)
    lane_f = lane.astype(F32)
    te = jnp.zeros(logits.shape, jnp.int32)
    tv = jnp.full(logits.shape, NEG, F32)
    for k in range(TOP_K):
        m = jnp.max(logits, axis=-1, keepdims=True)
        idx = jnp.min(jnp.where(logits == m, lane_f, 128.0), axis=-1, keepdims=True).astype(jnp.int32)
        te = jnp.where(lane == k, idx, te)
        tv = jnp.where(lane == k, m, tv)
        logits = jnp.where(lane == idx, NEG, logits)
    e = jnp.where(lane < TOP_K, jnp.exp(tv - jnp.max(tv, axis=-1, keepdims=True)), 0.0)
    te_ref[...] = te
    tg_ref[...] = e / jnp.sum(e, axis=-1, keepdims=True)


def _mix(yo_p, yo_s, g_p, g_s, oa_p, oa_s, p_all, x_all, wo, n2, wr, br):
    nbp = ROWS_P // TM5
    rb = lambda i: (i, 0)
    pb = pl.BlockSpec((TM5, D), lambda i: (jnp.minimum(i, nbp - 1), 0))
    sb = pl.BlockSpec((TM5, D), lambda i: (jnp.maximum(i - nbp, 0), 0))
    full = lambda shp: pl.BlockSpec(shp, lambda i: tuple(0 for _ in shp))
    return pl.pallas_call(
        _mix_kernel,
        grid=(ROWS // TM5,),
        in_specs=[pb, sb, pb, sb, pb, sb,
                  pl.BlockSpec((TM5, D), lambda i: (i, OFF_GA // D)),
                  pl.BlockSpec((TM5, D), lambda i: (i, OFF_GB // D)),
                  pl.BlockSpec((TM5, D), rb),
                  full((D, D)), full((1, D)), full((D, 128)), full((1, 128))],
        out_specs=[pl.BlockSpec((TM5, D), rb), pl.BlockSpec((TM5, D), rb),
                   pl.BlockSpec((TM5, 128), rb), pl.BlockSpec((TM5, 128), rb)],
        out_shape=[jax.ShapeDtypeStruct((ROWS, D), F32), jax.ShapeDtypeStruct((ROWS, D), F32),
                   jax.ShapeDtypeStruct((ROWS, 128), jnp.int32), jax.ShapeDtypeStruct((ROWS, 128), F32)],
        compiler_params=_cparams(("arbitrary",)),
        name="mix",
    )(yo_p, yo_s, g_p, g_s, oa_p, oa_s, p_all, p_all, x_all, wo, n2, wr, br)


N_TOK = NB * (N_META + SEQ) + ROWS_S
NK = N_TOK * TOP_K
TMM = 256
N_BLK = -(-NK // TMM) + N_EXPERTS
NSLOT = N_BLK * TMM
TN_G1 = 1024
TN_G2 = 1024
TMC = 128


def _gather_kernel(slot_ref, nused_ref, h_hbm, o_ref, buf, sem):
    m = pl.program_id(0)

    @pl.when(m < nused_ref[0])
    def _():
        def issue(r, _):
            tok = slot_ref[m * TMM + r]
            pltpu.make_async_copy(h_hbm.at[pl.ds(tok, 1), :], buf.at[pl.ds(r, 1), :], sem).start()
            return 0

        lax.fori_loop(0, TMM, issue, 0)
        pltpu.make_async_copy(h_hbm.at[pl.ds(0, TMM), :], buf, sem).wait()
        o_ref[...] = buf[...].astype(BF16)

    @pl.when(m >= nused_ref[0])
    def _():
        o_ref[...] = jnp.zeros(o_ref.shape, BF16)


def _gather(slot_tok, nused, h2):
    return pl.pallas_call(
        _gather_kernel,
        grid_spec=pltpu.PrefetchScalarGridSpec(
            num_scalar_prefetch=2,
            grid=(N_BLK,),
            in_specs=[pl.BlockSpec(memory_space=pl.ANY)],
            out_specs=pl.BlockSpec((TMM, D), lambda m, st, nu: (m, 0)),
            scratch_shapes=[pltpu.VMEM((TMM, D), F32), pltpu.SemaphoreType.DMA(())]),
        out_shape=jax.ShapeDtypeStruct((NSLOT, D), BF16),
        compiler_params=_cparams(("arbitrary",)),
        name="moe_gather",
    )(slot_tok, nused, h2)


def _new_expert(blk_e_ref, m):
    return (m == 0) | (blk_e_ref[m] != blk_e_ref[jnp.maximum(m - 1, 0)])


def _gm1_kernel(blk_e_ref, nused_ref, x_ref, w_ref, b_ref, sel_ref, o_ref, wb_scr):
    m = pl.program_id(1)
    used = m < nused_ref[0]

    @pl.when(used & _new_expert(blk_e_ref, m))
    def _():
        wb_scr[...] = w_ref[...].astype(BF16)

    @pl.when(used)
    def _():
        hd = jnp.dot(x_ref[...], wb_scr[...], preferred_element_type=F32) + b_ref[...]
        glu = jnp.minimum(hd, SWIGLU_LIMIT)
        lin = jnp.clip(hd, -SWIGLU_LIMIT, SWIGLU_LIMIT) + 1.0
        act = glu * _sigmoid(SWIGLU_ALPHA * glu)
        for c in range(TN_G1 // 512):
            parts = []
            for q in range(4):
                sl = slice(c * 512 + q * 128, c * 512 + (q + 1) * 128)
                parts.append(act[:, sl] * pltpu.roll(lin[:, sl], 127, axis=1))
            z = jnp.concatenate(parts, axis=1).astype(BF16)
            o_ref[:, c * 256:(c + 1) * 256] = jnp.dot(z, sel_ref[...], preferred_element_type=F32).astype(BF16)

    @pl.when(jnp.logical_not(used))
    def _():
        o_ref[...] = jnp.zeros(o_ref.shape, BF16)


def _gm1(blk_e, nused, xs, w1, b1, sel):
    def meff(m, nu):
        return jnp.minimum(m, nu[0] - 1)

    return pl.pallas_call(
        _gm1_kernel,
        grid_spec=pltpu.PrefetchScalarGridSpec(
            num_scalar_prefetch=2,
            grid=(2 * D_FF // TN_G1, N_BLK),
            in_specs=[pl.BlockSpec((TMM, D), lambda f, m, be, nu: (meff(m, nu), 0)),
                      pl.BlockSpec((None, D, TN_G1), lambda f, m, be, nu: (be[meff(m, nu)], 0, f)),
                      pl.BlockSpec((None, 1, TN_G1), lambda f, m, be, nu: (be[meff(m, nu)], 0, f)),
                      pl.BlockSpec((512, 256), lambda f, m, be, nu: (0, 0))],
            out_specs=pl.BlockSpec((TMM, TN_G1 // 2), lambda f, m, be, nu: (m, f)),
            scratch_shapes=[pltpu.VMEM((D, TN_G1), BF16)]),
        out_shape=jax.ShapeDtypeStruct((NSLOT, D_FF), BF16),
        compiler_params=_cparams(("arbitrary", "arbitrary")),
        name="moe_up",
    )(blk_e, nused, xs, w1, b1, sel)


def _gm2_kernel(blk_e_ref, nused_ref, x_ref, w_ref, b_ref, o_ref, wb_scr):
    m = pl.program_id(1)
    used = m < nused_ref[0]

    @pl.when(used & _new_expert(blk_e_ref, m))
    def _():
        wb_scr[...] = w_ref[...].astype(BF16)

    @pl.when(used)
    def _():
        o_ref[...] = jnp.dot(x_ref[...], wb_scr[...], preferred_element_type=F32) + b_ref[...]

    @pl.when(jnp.logical_not(used))
    def _():
        o_ref[...] = jnp.zeros(o_ref.shape, F32)


def _gm2(blk_e, nused, hid, w2, b2):
    def meff(m, nu):
        return jnp.minimum(m, nu[0] - 1)

    return pl.pallas_call(
        _gm2_kernel,
        grid_spec=pltpu.PrefetchScalarGridSpec(
            num_scalar_prefetch=2,
            grid=(D // TN_G2, N_BLK),
            in_specs=[pl.BlockSpec((TMM, D_FF), lambda f, m, be, nu: (meff(m, nu), 0)),
                      pl.BlockSpec((None, D_FF, TN_G2), lambda f, m, be, nu: (be[meff(m, nu)], 0, f)),
                      pl.BlockSpec((None, 1, TN_G2), lambda f, m, be, nu: (be[meff(m, nu)], 0, f))],
            out_specs=pl.BlockSpec((TMM, TN_G2), lambda f, m, be, nu: (m, f)),
            scratch_shapes=[pltpu.VMEM((D_FF, TN_G2), BF16)]),
        out_shape=jax.ShapeDtypeStruct((NSLOT, D), F32),
        compiler_params=_cparams(("arbitrary", "arbitrary")),
        name="moe_down",
    )(blk_e, nused, hid, w2, b2)


def _combine_kernel(dest_ref, x1_ref, tg_ref, fg_ref, y_hbm, o_ref, buf, sem):
    m = pl.program_id(0)

    def issue(r, _):
        for k in range(TOP_K):
            d = dest_ref[(m * TMC + r) * TOP_K + k]
            pltpu.make_async_copy(y_hbm.at[pl.ds(d, 1), :], buf.at[pl.ds(k * TMC + r, 1), :], sem).start()
        return 0

    lax.fori_loop(0, TMC, issue, 0)
    pltpu.make_async_copy(y_hbm.at[pl.ds(0, TOP_K * TMC), :], buf, sem).wait()
    x2 = x1_ref[...]
    tg = tg_ref[...]
    for k in range(TOP_K):
        x2 = x2 + tg[:, k:k + 1] * buf[k * TMC:(k + 1) * TMC, :]
    ms = jnp.mean(x2 * x2, axis=-1, keepdims=True)
    o_ref[...] = (x2 * lax.rsqrt(ms + RMS_EPS)) * fg_ref[...]


def _combine(dest, x1, tg, fg, y):
    return pl.pallas_call(
        _combine_kernel,
        grid_spec=pltpu.PrefetchScalarGridSpec(
            num_scalar_prefetch=1,
            grid=(ROWS // TMC,),
            in_specs=[pl.BlockSpec((TMC, D), lambda m, d: (m, 0)),
                      pl.BlockSpec((TMC, 128), lambda m, d: (m, 0)),
                      pl.BlockSpec((1, D), lambda m, d: (0, 0)),
                      pl.BlockSpec(memory_space=pl.ANY)],
            out_specs=pl.BlockSpec((TMC, D), lambda m, d: (m, 0)),
            scratch_shapes=[pltpu.VMEM((TOP_K * TMC, D), F32), pltpu.SemaphoreType.DMA(())]),
        out_shape=jax.ShapeDtypeStruct((ROWS, D), F32),
        compiler_params=_cparams(("arbitrary",)),
        name="moe_combine",
    )(dest, x1, tg, fg, y)


def _t5_bucket(rel):
    nb = N_BUCKETS // 2
    ret = jnp.where(rel > 0, nb, 0)
    n = jnp.abs(rel)
    max_exact = nb // 2
    nf = jnp.maximum(n, 1).astype(jnp.float32)
    large = max_exact + (jnp.log(nf / max_exact) / math.log(MAX_DISTANCE / max_exact) * (nb - max_exact)).astype(jnp.int32)
    large = jnp.minimum(large, nb - 1)
    return ret + jnp.where(n < max_exact, n, large)


def _valid_rows():
    rows = [np.arange(b * T_PAD + FRONT, (b + 1) * T_PAD) for b in range(NB)]
    rows.append(np.arange(ROWS_P, ROWS))
    return np.concatenate(rows).astype(np.int32)


def kernel(x_prompt, x_sample, cache_k, cache_v, state_wkv, state_shift, meta_tokens, rel_bias_table, norm1_g, w_in, shift_mu, decay_base, w_decay_up, a_base, w_a_up, w_g_up, k_k, k_a, r_k, gn_w, gn_b, lam_q1, lam_k1, lam_q2, lam_k2, subln_g, w_out, norm2_g, w_router, b_router, w_e1, b_e1, w_e2, b_e2, final_g):
    xp = jnp.concatenate([jnp.zeros((NB, FRONT, D), F32),
                          jnp.broadcast_to(meta_tokens[None], (NB, N_META, D)), x_prompt], axis=1)
    x_all = jnp.concatenate([xp.reshape(ROWS_P, D), x_sample.reshape(ROWS_S, D)], axis=0)
    lora0 = OFF_R + 3 * D
    w_p = jnp.concatenate([w_in[0][:, :lora0], w_in[0][:, lora0 + LORA_W:], w_in[0][:, lora0:lora0 + LORA_W],
                           jnp.zeros((D, LORA_P - LORA_W), F32)], axis=1).astype(BF16)

    p_all = _inproj(x_all, norm1_g, w_p)

    kq = jnp.arange(TQ, dtype=jnp.int32)
    bkt = jnp.stack([_t5_bucket(kq[:, None] - kq[None, :]), _t5_bucket(kq[:, None] - kq[None, :] - TQ)])
    lamp = jnp.concatenate([lam_q1, lam_k1, lam_q2, lam_k2], axis=0)
    oa_p = _attn_prompt(p_all, rel_bias_table, bkt, lamp, subln_g)
    qpos = PAST + jnp.arange(DEC_S, dtype=jnp.int32)
    bkt_last = _t5_bucket((PAST - NEAR + jnp.arange(NEAR, dtype=jnp.int32))[None, :] - qpos[:, None])
    bkt_new = _t5_bucket(qpos[None, :] - qpos[:, None])
    oa_s = _attn_sample(p_all, cache_k[0], cache_v[0],
                        rel_bias_table, bkt_last, bkt_new, lamp, subln_g)

    mu = shift_mu[0]
    mu3 = mu[:3 * D][None]
    mul = jnp.pad(mu[3 * D:], (0, LORA_P - LORA_W))[None]
    wd = jnp.zeros((LORA_P, D), F32).at[:DECAY_LORA].set(w_decay_up[0]).astype(BF16)
    wa = jnp.zeros((LORA_P, D), F32).at[DECAY_LORA:DECAY_LORA + AAA_LORA].set(w_a_up[0]).astype(BF16)
    wg = jnp.zeros((LORA_P, D), F32).at[DECAY_LORA + AAA_LORA:LORA_W].set(w_g_up[0]).astype(BF16)
    first_p = jnp.zeros((NB, 1, 3 * D + LORA_P), F32)
    first_s = jnp.pad(state_shift[0], ((0, 0), (0, LORA_P - LORA_W)))[:, None, :]
    prep_p = _rwkv_prep(p_all, first_p, mu3, mul, decay_base, a_base, wd, wa, wg, NB, T_PAD // 256, 256, 0)
    prep_s = _rwkv_prep(p_all, first_s, mu3, mul, decay_base, a_base, wd, wa, wg, DEC_B, 1, DEC_S, ROWS_P // DEC_S)

    def chain_tile(v):
        return jnp.tile(v.reshape(B_HEADS, B_HEAD).T, (1, 128 // B_HEADS))

    par = [chain_tile(v.reshape(-1)) for v in (k_k[0], k_a[0], r_k[0], gn_w[0], gn_b[0])]
    s0_p = jnp.zeros((B_HEAD, B_HEAD, NB * B_HEADS), F32)
    s0_s = state_wkv[0].transpose(3, 2, 0, 1).reshape(B_HEAD, B_HEAD, DEC_B * B_HEADS)
    yo_p, sfin_p = _rwkv_rec(*[_to_chains(a, NB, T_PAD) for a in prep_p[:5]], *par, s0_p, skip=FRONT // TB)
    yo_s, sfin_s = _rwkv_rec(*[_to_chains(a, DEC_B, DEC_S) for a in prep_s[:5]], *par, s0_s)

    wo = w_out[0].astype(BF16)
    wr = jnp.pad(w_router[0], ((0, 0), (0, 128 - N_EXPERTS)))
    br = jnp.concatenate([b_router[0], jnp.full((128 - N_EXPERTS,), NEG, F32)])[None]
    x1, h2, te, tg = _mix(_from_chains(yo_p, NB, T_PAD), _from_chains(yo_s, DEC_B, DEC_S), prep_p[5], prep_s[5],
                          oa_p, oa_s, p_all, x_all, wo, norm2_g, wr, br)

    valid = jnp.asarray(_valid_rows())
    flat_e = te[valid, :TOP_K].reshape(-1)
    order = jnp.argsort(flat_e)
    e_sorted = flat_e[order]
    tok_sorted = valid[order // TOP_K]
    counts = jnp.zeros((N_EXPERTS,), jnp.int32).at[flat_e].add(1)
    padded = (counts + TMM - 1) // TMM * TMM
    pad_end = jnp.cumsum(padded)
    pad_start = pad_end - padded
    grp_start = jnp.cumsum(counts) - counts
    dest_sorted = pad_start[e_sorted] + jnp.arange(NK, dtype=jnp.int32) - grp_start[e_sorted]
    slot_tok = jnp.zeros((NSLOT,), jnp.int32).at[dest_sorted].set(tok_sorted)
    blk_start = jnp.arange(N_BLK, dtype=jnp.int32) * TMM
    blk_e = jnp.minimum(jnp.searchsorted(pad_end, blk_start, side='right'), N_EXPERTS - 1).astype(jnp.int32)
    nused = (pad_end[-1:] // TMM).astype(jnp.int32)
    dest_tok = jnp.zeros((NK,), jnp.int32).at[order].set(dest_sorted).reshape(N_TOK, TOP_K)
    dest = jnp.zeros((ROWS, TOP_K), jnp.int32).at[valid].set(dest_tok).reshape(-1)

    sel_np = np.zeros((512, 256), np.float32)
    sel_np[2 * np.arange(256), np.arange(256)] = 1.0
    xs = _gather(slot_tok, nused, h2)
    hid = _gm1(blk_e, nused, xs, w_e1[0], b_e1[0][:, None, :], jnp.asarray(sel_np, BF16))
    ys = _gm2(blk_e, nused, hid, w_e2[0], b_e2[0][:, None, :])
    y_all = _combine(dest, x1, tg, final_g[None], ys)

    y_prompt = y_all[:ROWS_P].reshape(NB, T_PAD, D)[:, FRONT + N_META:]
    y_sample = y_all[ROWS_P:].reshape(DEC_B, DEC_S, D)
    def prompt_rows(off):
        sec = p_all[:ROWS_P, off:off + D].reshape(NB, T_PAD, D)[:, FRONT:]
        return sec.reshape(1, NB, N_META + SEQ, A_HEADS, A_VD)

    def sample_rows(off):
        return p_all[ROWS_P:, off:off + D].reshape(1, DEC_B, DEC_S, A_HEADS, A_VD)

    def last_ps(last):
        return last[:, 0, :SHIFT_W][None]

    k_prompt, v_prompt = prompt_rows(OFF_K), prompt_rows(OFF_V)
    k_sample, v_sample = sample_rows(OFF_K), sample_rows(OFF_V)
    shift_prompt = last_ps(prep_p[6])
    shift_sample = last_ps(prep_s[6])
    wkv_prompt = sfin_p.reshape(B_HEAD, B_HEAD, NB, B_HEADS).transpose(2, 3, 1, 0)[None]
    wkv_sample = sfin_s.reshape(B_HEAD, B_HEAD, DEC_B, B_HEADS).transpose(2, 3, 1, 0)[None].astype(state_wkv.dtype)
    return (y_prompt, y_sample, k_prompt, v_prompt, wkv_prompt, shift_prompt,
            k_sample, v_sample, wkv_sample, shift_sample)
```

```python
import functools
import math

import numpy as np
import jax
import jax.numpy as jnp
from jax import lax
from jax.experimental import pallas as pl
from jax.experimental.pallas import tpu as pltpu

F32 = jnp.float32
BF16 = jnp.bfloat16

D = 2048
NB = 4
SEQ = 2048
DEC_B = 16
DEC_S = 32
PAST = 2048
CHUNK = 64
N_META = 16
RMS_EPS = 1e-5
A_DH = 64
A_HEADS = 16
A_VD = 128
B_HEAD = 64
B_HEADS = 32
DECAY_LORA = 96
AAA_LORA = 96
GATE_LORA = 256
GN_EPS = 64e-5
N_BUCKETS = 32
MAX_DISTANCE = 128
N_EXPERTS = 32
TOP_K = 4
D_FF = 2048
SWIGLU_ALPHA = 1.702
SWIGLU_LIMIT = 7.0
LAM_INIT = 0.8 - 0.6 * math.exp(-0.3 * 0)
SHIFT_W = 3 * D + DECAY_LORA + AAA_LORA + GATE_LORA
LORA_W = DECAY_LORA + AAA_LORA + GATE_LORA
LORA_P = 512

T_PAD = 2304
FRONT = T_PAD - N_META - SEQ
ROWS_P = NB * T_PAD
ROWS_S = DEC_B * DEC_S
ROWS = ROWS_P + ROWS_S

OFF_Q = 0
OFF_K = 2048
OFF_V = 4096
OFF_R = 6144
OFF_KX = 8192
OFF_VX = 10240
OFF_GA = 12288
OFF_GB = 14336
OFF_LORA = 16384
PW = OFF_LORA + LORA_P

NEG = -1e30
LOG2E = 1.4426950408889634
VMEM_LIMIT = 56 * 1024 * 1024


def _cparams(sem, vmem=VMEM_LIMIT):
    return pltpu.CompilerParams(dimension_semantics=sem, vmem_limit_bytes=vmem)


def _sigmoid(x):
    return 1.0 / (1.0 + jnp.exp(-x))


TM1 = 512
TN1 = 512


def _inproj_kernel(x_ref, g_ref, w_ref, o_ref, h_scr):
    @pl.when(pl.program_id(1) == 0)
    def _():
        x = x_ref[...]
        ms = jnp.mean(x * x, axis=-1, keepdims=True)
        h_scr[...] = ((x * lax.rsqrt(ms + RMS_EPS)) * g_ref[...]).astype(BF16)

    o_ref[...] = jnp.dot(h_scr[...], w_ref[...], preferred_element_type=F32)


def _inproj(x_all, g, w):
    return pl.pallas_call(
        _inproj_kernel,
        grid=(ROWS // TM1, PW // TN1),
        in_specs=[pl.BlockSpec((TM1, D), lambda i, j: (i, 0)),
                  pl.BlockSpec((1, D), lambda i, j: (0, 0)),
                  pl.BlockSpec((D, TN1), lambda i, j: (0, j))],
        out_specs=pl.BlockSpec((TM1, TN1), lambda i, j: (i, j)),
        out_shape=jax.ShapeDtypeStruct((ROWS, PW), F32),
        scratch_shapes=[pltpu.VMEM((TM1, D), BF16)],
        compiler_params=_cparams(("arbitrary", "arbitrary")),
        name="inproj",
    )(x_all, g, w)


def _lam_value(lamp_ref):
    lp = lamp_ref[...]
    s1 = jnp.sum(lp[0:1, :] * lp[1:2, :], axis=-1, keepdims=True)
    s2 = jnp.sum(lp[2:3, :] * lp[3:4, :], axis=-1, keepdims=True)
    return jnp.exp(s1) - jnp.exp(s2) + LAM_INIT


def _bias_from_buckets(bkt, table_ref, h):
    out = jnp.zeros(bkt.shape, F32)
    for n in range(N_BUCKETS):
        out = jnp.where(bkt == n, table_ref[n, h], out)
    return out


TQ = 256
NQB = T_PAD // TQ
BIAS_META, BIAS_DIAG, BIAS_SUB, BIAS_SUB_PAD, BIAS_FAR_PAD, BIAS_FAR = range(6)


def _attn_prompt_kernel(table_ref, q_ref, k_ref, v_ref, bkt_ref, lamp_ref, subg_ref, o_ref,
                        kb_scr, vt_scr, bias_scr, s_scr, acc_scr):
    h = pl.program_id(0)
    b = pl.program_id(1)

    @pl.when(b == 0)
    def _():
        kk = lax.broadcasted_iota(jnp.int32, (TQ, TQ), 0)
        qq = lax.broadcasted_iota(jnp.int32, (TQ, TQ), 1)
        chunk_ok = (kk // CHUNK) <= (qq // CHUNK)
        pad = jnp.where(kk < FRONT, NEG, 0.0).astype(F32)
        b_diag = _bias_from_buckets(bkt_ref[0], table_ref, h)
        b_sub = _bias_from_buckets(bkt_ref[1], table_ref, h)
        b_far = jnp.zeros((TQ, TQ), F32) + table_ref[N_BUCKETS // 2 - 1, h]
        tiles = (jnp.where(chunk_ok, b_diag, NEG) + pad,
                 jnp.where(chunk_ok, b_diag, NEG),
                 b_sub,
                 b_sub + pad,
                 b_far + pad,
                 b_far)
        for i, t in enumerate(tiles):
            bias_scr[i] = jnp.concatenate([t, t], axis=1) * LOG2E

    kb_scr[...] = k_ref[...].astype(BF16)
    for i in range(NQB):
        vt_scr[i] = v_ref[i * TQ:(i + 1) * TQ, :].T.astype(BF16)

    lam = _lam_value(lamp_ref)
    drow = lax.broadcasted_iota(jnp.int32, (A_VD, TQ), 0)

    def q_operand(qi):
        q0 = pl.multiple_of(qi * TQ, TQ)
        qt = (q_ref[pl.ds(q0, TQ), :] * (A_DH ** -0.5 * LOG2E)).T
        q1 = jnp.where(drow < A_DH, qt, 0.0)
        q2 = jnp.where(drow >= A_DH, qt, 0.0)
        return jnp.concatenate([q1, q2], axis=1).astype(BF16)

    def scores(kj, qpt, slot):
        k0 = pl.multiple_of(kj * TQ, TQ)
        s_scr[slot] = jnp.dot(kb_scr[pl.ds(k0, TQ), :], qpt, preferred_element_type=F32)

    def accumulate(kj, bias_id, slot, ml):
        m, l = ml
        s = s_scr[slot] + bias_scr[bias_id]
        m_new = jnp.maximum(m, jnp.max(s, axis=0, keepdims=True))
        alpha = jnp.exp2(m - m_new)
        p = jnp.exp2(s - m_new)
        l = l * alpha + jnp.sum(p, axis=0, keepdims=True)
        acc_scr[...] = acc_scr[...] * alpha + jnp.dot(vt_scr[kj], p.astype(BF16), preferred_element_type=F32)
        return m_new, l

    def finish(qi, ml):
        on = acc_scr[...] / ml[1]
        ot = on[:, :TQ] - lam * on[:, TQ:]
        ms = jnp.mean(ot * ot, axis=0, keepdims=True)
        o = (ot * lax.rsqrt(ms + RMS_EPS)).T * subg_ref[...] * (1.0 - LAM_INIT)
        q0 = pl.multiple_of(qi * TQ, TQ)
        o_ref[pl.ds(q0, TQ), :] = o

    def start(qi):
        qpt = q_operand(qi)
        acc_scr[...] = jnp.zeros(acc_scr.shape, F32)
        scores(0, qpt, 0)
        return qpt, (jnp.full((1, 2 * TQ), NEG, F32), jnp.zeros((1, 2 * TQ), F32))

    finish(0, accumulate(0, BIAS_META, 0, start(0)[1]))

    def q_block(qi, _):
        qpt, ml = start(qi)

        def half(kj, cur, ml):
            scores(kj + 1, qpt, 1 - cur)
            first = kj == 0
            bias_id = jnp.where(kj == qi - 1, jnp.where(first, BIAS_SUB_PAD, BIAS_SUB),
                                jnp.where(first, BIAS_FAR_PAD, BIAS_FAR))
            return accumulate(kj, bias_id, cur, ml)

        ml = lax.fori_loop(0, qi // 2, lambda i, ml: half(2 * i + 1, 1, half(2 * i, 0, ml)), ml)
        ml = lax.cond(qi % 2 == 1,
                      lambda ml: accumulate(qi, BIAS_DIAG, 1, half(qi - 1, 0, ml)),
                      lambda ml: accumulate(qi, BIAS_DIAG, 0, ml), ml)
        finish(qi, ml)
        return 0

    lax.fori_loop(1, NQB, q_block, 0)


def _attn_prompt(p_all, table, bkt, lamp, subg):
    cb = D // A_VD
    return pl.pallas_call(
        _attn_prompt_kernel,
        grid_spec=pltpu.PrefetchScalarGridSpec(
            num_scalar_prefetch=0,
            grid=(A_HEADS, NB),
            in_specs=[pl.BlockSpec(memory_space=pltpu.SMEM),
                      pl.BlockSpec((T_PAD, A_VD), lambda h, b: (b, h)),
                      pl.BlockSpec((T_PAD, A_VD), lambda h, b: (b, cb + h)),
                      pl.BlockSpec((T_PAD, A_VD), lambda h, b: (b, 2 * cb + h)),
                      pl.BlockSpec((2, TQ, TQ), lambda h, b: (0, 0, 0)),
                      pl.BlockSpec((4, A_DH), lambda h, b: (0, 0)),
                      pl.BlockSpec((1, A_VD), lambda h, b: (0, 0))],
            out_specs=pl.BlockSpec((T_PAD, A_VD), lambda h, b: (b, h)),
            scratch_shapes=[pltpu.VMEM((T_PAD, A_VD), BF16),
                            pltpu.VMEM((NQB, A_VD, TQ), BF16),
                            pltpu.VMEM((6, TQ, 2 * TQ), F32),
                            pltpu.VMEM((2, TQ, 2 * TQ), F32),
                            pltpu.VMEM((A_VD, 2 * TQ), F32)]),
        out_shape=jax.ShapeDtypeStruct((ROWS_P, D), F32),
        compiler_params=_cparams(("arbitrary", "arbitrary")),
        name="attn_prompt",
    )(table, p_all, p_all, p_all, bkt, lamp, subg)


HG = 4
NEAR = 512


def _attn_sample_kernel(table_ref, q_ref, kn_ref, vn_ref, bl_ref, bn_ref, lamp_ref, subg_ref, ck_hbm, cv_hbm,
                        o_ref, kbuf, vbuf, sem):
    hg = pl.program_id(1)
    ng = pl.num_programs(1)
    step = pl.program_id(0) * ng + hg
    nsteps = pl.num_programs(0) * ng

    def copies(n, slot):
        out = []
        for i in range(HG):
            h = (n % ng) * HG + i
            out.append(pltpu.make_async_copy(ck_hbm.at[n // ng, :, h, :], kbuf.at[slot, i], sem.at[slot, i]))
            out.append(pltpu.make_async_copy(cv_hbm.at[n // ng, :, h, :], vbuf.at[slot, i], sem.at[slot, HG + i]))
        return out

    cur = step % 2

    @pl.when(step == 0)
    def _():
        for c in copies(step, 0):
            c.start()

    @pl.when(step + 1 < nsteps)
    def _():
        for c in copies(step + 1, 1 - cur):
            c.start()

    for c in copies(step, cur):
        c.wait()
    ck_refs = [kbuf.at[cur, i] for i in range(HG)]
    cv_refs = [vbuf.at[cur, i] for i in range(HG)]
    lam = _lam_value(lamp_ref)
    lane = lax.broadcasted_iota(jnp.int32, (DEC_S, A_VD), 1)
    nt = (((1,), (1,)), ((), ()))
    for i in range(HG):
        h = hg * HG + i
        cols = slice(i * A_VD, (i + 1) * A_VD)
        q = q_ref[:, cols] * (A_DH ** -0.5)
        qp = jnp.concatenate([jnp.where(lane < A_DH, q, 0.0), jnp.where(lane >= A_DH, q, 0.0)],
                             axis=0).astype(BF16)
        far = jnp.zeros((DEC_S, PAST - NEAR), F32) + table_ref[N_BUCKETS // 2 - 1, h]
        bl = jnp.concatenate([far, _bias_from_buckets(bl_ref[...], table_ref, h)], axis=1)
        bn = _bias_from_buckets(bn_ref[...], table_ref, h)
        s = (lax.dot_general(qp, ck_refs[i][...].astype(BF16), nt, preferred_element_type=F32)
             + jnp.concatenate([bl, bl], axis=0))
        sn = (lax.dot_general(qp, kn_ref[:, cols].astype(BF16), nt, preferred_element_type=F32)
              + jnp.concatenate([bn, bn], axis=0))
        m = jnp.maximum(jnp.max(s, axis=-1, keepdims=True), jnp.max(sn, axis=-1, keepdims=True))
        p = jnp.exp(s - m)
        pn = jnp.exp(sn - m)
        l = jnp.sum(p, axis=-1, keepdims=True) + jnp.sum(pn, axis=-1, keepdims=True)
        acc = (jnp.dot(p.astype(BF16), cv_refs[i][...].astype(BF16), preferred_element_type=F32)
               + jnp.dot(pn.astype(BF16), vn_ref[:, cols].astype(BF16), preferred_element_type=F32))
        on = acc / l
        o = on[:DEC_S] - lam * on[DEC_S:]
        ms = jnp.mean(o * o, axis=-1, keepdims=True)
        o_ref[:, cols] = (o * lax.rsqrt(ms + RMS_EPS)) * subg_ref[...] * (1.0 - LAM_INIT)


def _attn_sample(p_all, cache_k, cache_v, table, bkt_last, bkt_new, lamp, subg):
    r0 = ROWS_P // DEC_S
    w = HG * A_VD
    new = lambda off: pl.BlockSpec((DEC_S, w), lambda b, g: (r0 + b, off // w + g))
    return pl.pallas_call(
        _attn_sample_kernel,
        grid=(DEC_B, A_HEADS // HG),
        in_specs=[pl.BlockSpec(memory_space=pltpu.SMEM), new(OFF_Q), new(OFF_K), new(OFF_V),
                  pl.BlockSpec((DEC_S, NEAR), lambda b, g: (0, 0)),
                  pl.BlockSpec((DEC_S, DEC_S), lambda b, g: (0, 0)),
                  pl.BlockSpec((4, A_DH), lambda b, g: (0, 0)),
                  pl.BlockSpec((1, A_VD), lambda b, g: (0, 0)),
                  pl.BlockSpec(memory_space=pl.ANY), pl.BlockSpec(memory_space=pl.ANY)],
        out_specs=pl.BlockSpec((DEC_S, w), lambda b, g: (b, g)),
        out_shape=jax.ShapeDtypeStruct((ROWS_S, D), F32),
        scratch_shapes=[pltpu.VMEM((2, HG, PAST, A_VD), F32), pltpu.VMEM((2, HG, PAST, A_VD), F32),
                        pltpu.SemaphoreType.DMA((2, 2 * HG))],
        compiler_params=_cparams(("arbitrary", "arbitrary")),
        name="attn_sample",
    )(table, p_all, p_all, p_all, bkt_last, bkt_new, lamp, subg, cache_k, cache_v)


def _rwkv_prep_kernel(chain_out, r_ref, k_ref, v_ref, lo_ref, first_ref, mu_ref, mul_ref, dbase_ref, abase_ref,
                      wd_ref, wa_ref, wg_ref,
                      xr_ref, xk_ref, xv_ref, wp_ref, ap_ref, g_ref, last_ref, carry_scr, carryl_scr):
    i = pl.program_id(1)
    tm = r_ref.shape[0]
    row = lax.broadcasted_iota(jnp.int32, (tm, 1), 0)

    def shift(x, first, mu):
        prev = jnp.where(row == 0, first, pltpu.roll(x, 1, axis=0))
        return x + (prev - x) * mu

    def put(dst, val):
        if chain_out:
            for hp in range(B_HEADS // 2):
                t = val[:, hp * 128:(hp + 1) * 128].T
                dst[2 * hp] = t[:B_HEAD]
                dst[2 * hp + 1] = t[B_HEAD:]
        else:
            dst[...] = val

    for s, (src, dst) in enumerate(((r_ref, xr_ref), (k_ref, xk_ref), (v_ref, xv_ref))):
        x = src[...]
        first = jnp.where(i == 0, first_ref[:, s * D:(s + 1) * D], carry_scr[:, s * D:(s + 1) * D])
        put(dst, shift(x, first, mu_ref[:, s * D:(s + 1) * D]))
        carry_scr[:, s * D:(s + 1) * D] = x[tm - 1:tm, :]
        last_ref[:, s * D:(s + 1) * D] = x[tm - 1:tm, :]

    xl = lo_ref[...]
    first = jnp.where(i == 0, first_ref[:, 3 * D:], carryl_scr[...])
    xs = shift(xl, first, mul_ref[...])
    carryl_scr[...] = xl[tm - 1:tm, :]
    last_ref[:, 3 * D:] = xl[tm - 1:tm, :]
    put(wp_ref, dbase_ref[...] + jnp.dot(jnp.tanh(xs).astype(BF16), wd_ref[...], preferred_element_type=F32))
    put(ap_ref, abase_ref[...] + jnp.dot(xs.astype(BF16), wa_ref[...], preferred_element_type=F32))
    g_ref[...] = jnp.dot(_sigmoid(xs).astype(BF16), wg_ref[...], preferred_element_type=F32)


def _rwkv_prep(p_all, first, mu, mul, dbase, abase, wd, wa, wg, nseq, nblk, tm, row_blk0, chain_out):
    rows = nseq * nblk * tm
    cs = lambda off: (lambda s, i: (row_blk0 + s * nblk + i, off))
    full = lambda shp: pl.BlockSpec(shp, lambda s, i: tuple(0 for _ in shp))
    out = jax.ShapeDtypeStruct((rows, D), F32)
    ospec = pl.BlockSpec((tm, D), lambda s, i: (s * nblk + i, 0))
    if chain_out:
        cout = jax.ShapeDtypeStruct((nseq * B_HEADS, B_HEAD, nblk * tm), F32)
        cspec = pl.BlockSpec((B_HEADS, B_HEAD, tm), lambda s, i: (s, 0, i))
    else:
        cout, cspec = out, ospec
    return pl.pallas_call(
        functools.partial(_rwkv_prep_kernel, chain_out),
        grid=(nseq, nblk),
        in_specs=[pl.BlockSpec((tm, D), cs(OFF_R // D)),
                  pl.BlockSpec((tm, D), cs(OFF_KX // D)),
                  pl.BlockSpec((tm, D), cs(OFF_VX // D)),
                  pl.BlockSpec((tm, LORA_P), cs(OFF_LORA // LORA_P)),
                  pl.BlockSpec((None, 1, 3 * D + LORA_P), lambda s, i: (s, 0, 0)),
                  full((1, 3 * D)), full((1, LORA_P)), full((1, D)), full((1, D)),
                  full((LORA_P, D)), full((LORA_P, D)), full((LORA_P, D))],
        out_specs=[cspec] * 5 + [ospec, pl.BlockSpec((None, 1, 3 * D + LORA_P), lambda s, i: (s, 0, 0))],
        out_shape=[cout] * 5 + [out, jax.ShapeDtypeStruct((nseq, 1, 3 * D + LORA_P), F32)],
        scratch_shapes=[pltpu.VMEM((1, 3 * D), F32), pltpu.VMEM((1, LORA_P), F32)],
        compiler_params=_cparams(("arbitrary", "arbitrary")),
        name="rwkv_prep",
    )(p_all, p_all, p_all, p_all, first, mu, mul, dbase, abase, wd, wa, wg)


TB = 32
IH = 32


def _rwkv_rec_kernel(skip, *refs):
    tb = pl.program_id(1)
    s0_ref, o_ref, sfin_ref, S = refs[10], refs[11], refs[12], refs[13]

    @pl.when(tb == 0)
    def _():
        S[...] = s0_ref[...]

    if skip:
        @pl.when(tb < skip)
        def _():
            o_ref[...] = jnp.zeros(o_ref.shape, F32)

        pl.when(tb >= skip)(functools.partial(_rwkv_rec_block, *refs))
    else:
        _rwkv_rec_block(*refs)
    sfin_ref[...] = S[...]


def _rwkv_rec_block(xr_ref, xk_ref, xv_ref, wp_ref, ap_ref, kk_ref, ka_ref, rk_ref, gnw_ref, gnb_ref, s0_ref,
                    o_ref, sfin_ref, S, w_s, k_s, a_s, b_s, sa_s, y_s):
    z = -wp_ref[...]
    softplus = jnp.maximum(z, 0.0) + jnp.log(1.0 + jnp.exp(-jnp.abs(z)))
    w_s[...] = jnp.exp(-jnp.exp(-softplus - 0.5))
    a = _sigmoid(ap_ref[...])
    xk = xk_ref[...]
    kk = xk * kk_ref[...]
    nrm = jnp.sqrt(jnp.sum(kk * kk, axis=1, keepdims=True))
    kk = kk / jnp.maximum(nrm, 1e-12)
    k_s[...] = xk * (1.0 + (a - 1.0) * ka_ref[...])
    a_s[...] = -kk
    b_s[...] = kk * a

    for hf in range(B_HEAD // IH):
        i0 = hf * IH
        acc = jnp.zeros((IH, 128), F32)
        for j in range(B_HEAD):
            acc = acc + S[j, i0:i0 + IH, :] * a_s[0, j:j + 1, :]
        sa_s[0, i0:i0 + IH, :] = acc

    def token(t, _):
        cur = t % 2
        tn = jnp.minimum(t + 1, TB - 1)
        for hf in range(B_HEAD // IH):
            i0 = hf * IH
            sa = sa_s[cur, i0:i0 + IH, :]
            v = xv_ref[t, i0:i0 + IH, :]
            y = jnp.zeros((IH, 128), F32)
            san = jnp.zeros((IH, 128), F32)
            for j in range(B_HEAD):
                s_new = (S[j, i0:i0 + IH, :] * w_s[t, j:j + 1, :] + sa * b_s[t, j:j + 1, :]
                         + v * k_s[t, j:j + 1, :])
                S[j, i0:i0 + IH, :] = s_new
                y = y + s_new * xr_ref[t, j:j + 1, :]
                san = san + s_new * a_s[tn, j:j + 1, :]
            y_s[i0:i0 + IH, :] = y
            sa_s[1 - cur, i0:i0 + IH, :] = san
        y = y_s[...]
        mu = jnp.mean(y, axis=0, keepdims=True)
        yc = y - mu
        var = jnp.mean(yc * yc, axis=0, keepdims=True)
        bonus = jnp.sum(xr_ref[t] * k_s[t] * rk_ref[...], axis=0, keepdims=True)
        o_ref[t] = yc * lax.rsqrt(var + GN_EPS) * gnw_ref[...] + gnb_ref[...] + bonus * xv_ref[t]
        return 0

    lax.fori_loop(0, TB, token, 0)


def _rwkv_rec(xr, xk, xv, wp, ap, kk, ka, rk, gnw, gnb, s0, skip=0):
    t, _, c = xr.shape
    tok = pl.BlockSpec((TB, B_HEAD, 128), lambda g, tb: (tb, 0, g))
    par = pl.BlockSpec((B_HEAD, 128), lambda g, tb: (0, 0))
    st = pl.BlockSpec((B_HEAD, B_HEAD, 128), lambda g, tb: (0, 0, g))
    blk = pltpu.VMEM((TB, B_HEAD, 128), F32)
    return pl.pallas_call(
        functools.partial(_rwkv_rec_kernel, skip),
        grid=(c // 128, t // TB),
        in_specs=[tok] * 5 + [par] * 5 + [st],
        out_specs=[tok, st],
        out_shape=[jax.ShapeDtypeStruct((t, B_HEAD, c), F32), jax.ShapeDtypeStruct((B_HEAD, B_HEAD, c), F32)],
        scratch_shapes=[pltpu.VMEM((B_HEAD, B_HEAD, 128), F32), blk, blk, blk, blk,
                        pltpu.VMEM((2, B_HEAD, 128), F32), pltpu.VMEM((B_HEAD, 128), F32)],
        compiler_params=_cparams(("arbitrary", "arbitrary")),
        name="rwkv_rec",
    )(xr, xk, xv, wp, ap, kk, ka, rk, gnw, gnb, s0)


def _to_chains(a, nb, t):
    return a.reshape(nb, t, B_HEADS, B_HEAD).transpose(1, 3, 0, 2).reshape(t, B_HEAD, nb * B_HEADS)


def _from_chains(a, nb, t):
    return a.reshape(t, B_HEAD, nb, B_HEADS).transpose(2, 0, 3, 1).reshape(nb * t, D)


TM5 = 256


def _mix_kernel(yo_p_ref, yo_s_ref, g_p_ref, g_s_ref, oa_p_ref, oa_s_ref, ga_ref, gb_ref, x_ref,
                wo_ref, n2_ref, wr_ref, br_ref, x1_ref, h2_ref, te_ref, tg_ref):
    is_p = pl.program_id(0) < ROWS_P // TM5
    tiles = [jnp.concatenate([yo_p_ref[2 * hp], yo_p_ref[2 * hp + 1]], axis=0).T for hp in range(B_HEADS // 2)]
    yo = jnp.where(is_p, jnp.concatenate(tiles, axis=1), yo_s_ref[...])
    g = jnp.where(is_p, g_p_ref[...], g_s_ref[...])
    oa = jnp.where(is_p, oa_p_ref[...], oa_s_ref[...])
    mixed = _sigmoid(ga_ref[...]) * oa + _sigmoid(gb_ref[...]) * (yo * g)
    x1 = x_ref[...] + jnp.dot(mixed.astype(BF16), wo_ref[...], preferred_element_type=F32)
    x1_ref[...] = x1
    ms = jnp.mean(x1 * x1, axis=-1, keepdims=True)
    h2 = (x1 * lax.rsqrt(ms + RMS_EPS)) * n2_ref[...]
    h2_ref[...] = h2
    logits = jnp.dot(h2, wr_ref[...], preferred_element_type=F32, precision=lax.Precision.HIGHEST) + br_ref[...]
    lane = lax.broadcasted_iota(jnp.int32, logits.shape, 1)
    lane_f = lane.astype(F32)
    te = jnp.zeros(logits.shape, jnp.int32)
    tv = jnp.full(logits.shape, NEG, F32)
    for k in range(TOP_K):
        m = jnp.max(logits, axis=-1, keepdims=True)
        idx = jnp.min(jnp.where(logits == m, lane_f, 128.0), axis=-1, keepdims=True).astype(jnp.int32)
        te = jnp.where(lane == k, idx, te)
        tv = jnp.where(lane == k, m, tv)
        logits = jnp.where(lane == idx, NEG, logits)
    e = jnp.where(lane < TOP_K, jnp.exp(tv - jnp.max(tv, axis=-1, keepdims=True)), 0.0)
    te_ref[...] = te
    tg_ref[...] = e / jnp.sum(e, axis=-1, keepdims=True)


def _mix(yo_p, yo_s, g_p, g_s, oa_p, oa_s, p_all, x_all, wo, n2, wr, br):
    nbp = ROWS_P // TM5
    rb = lambda i: (i, 0)
    pb = pl.BlockSpec((TM5, D), lambda i: (jnp.minimum(i, nbp - 1), 0))
    sb = pl.BlockSpec((TM5, D), lambda i: (jnp.maximum(i - nbp, 0), 0))
    full = lambda shp: pl.BlockSpec(shp, lambda i: tuple(0 for _ in shp))
    return pl.pallas_call(
        _mix_kernel,
        grid=(ROWS // TM5,),
        in_specs=[pl.BlockSpec((B_HEADS, B_HEAD, TM5),
                               lambda i: (jnp.minimum(i, nbp - 1) // (T_PAD // TM5), 0,
                                          jnp.minimum(i, nbp - 1) % (T_PAD // TM5))),
                  sb, pb, sb, pb, sb,
                  pl.BlockSpec((TM5, D), lambda i: (i, OFF_GA // D)),
                  pl.BlockSpec((TM5, D), lambda i: (i, OFF_GB // D)),
                  pl.BlockSpec((TM5, D), rb),
                  full((D, D)), full((1, D)), full((D, 128)), full((1, 128))],
        out_specs=[pl.BlockSpec((TM5, D), rb), pl.BlockSpec((TM5, D), rb),
                   pl.BlockSpec((TM5, 128), rb), pl.BlockSpec((TM5, 128), rb)],
        out_shape=[jax.ShapeDtypeStruct((ROWS, D), F32), jax.ShapeDtypeStruct((ROWS, D), F32),
                   jax.ShapeDtypeStruct((ROWS, 128), jnp.int32), jax.ShapeDtypeStruct((ROWS, 128), F32)],
        compiler_params=_cparams(("arbitrary",)),
        name="mix",
    )(yo_p, yo_s, g_p, g_s, oa_p, oa_s, p_all, p_all, x_all, wo, n2, wr, br)


N_TOK = NB * (N_META + SEQ) + ROWS_S
NK = N_TOK * TOP_K
TMM = 256
N_BLK = -(-NK // TMM) + N_EXPERTS
NSLOT = N_BLK * TMM
TN_G1 = 1024
TN_G2 = 1024
TMC = 128


def _gather_kernel(slot_ref, nused_ref, h_hbm, o_ref, buf, sem):
    m = pl.program_id(0)
    cur = m % 2

    def issue(blk, slot):
        def row(r, _):
            tok = slot_ref[blk * TMM + r]
            pltpu.make_async_copy(h_hbm.at[pl.ds(tok, 1), :], buf.at[slot, pl.ds(r, 1), :], sem.at[slot]).start()
            return 0

        lax.fori_loop(0, TMM, row, 0, unroll=8)

    @pl.when((m == 0) & (m < nused_ref[0]))
    def _():
        issue(m, 0)

    @pl.when(m + 1 < nused_ref[0])
    def _():
        issue(m + 1, 1 - cur)

    @pl.when(m < nused_ref[0])
    def _():
        pltpu.make_async_copy(h_hbm.at[pl.ds(0, TMM), :], buf.at[cur], sem.at[cur]).wait()
        o_ref[...] = buf[cur].astype(BF16)

    @pl.when(m >= nused_ref[0])
    def _():
        o_ref[...] = jnp.zeros(o_ref.shape, BF16)


def _gather(slot_tok, nused, h2):
    return pl.pallas_call(
        _gather_kernel,
        grid_spec=pltpu.PrefetchScalarGridSpec(
            num_scalar_prefetch=2,
            grid=(N_BLK,),
            in_specs=[pl.BlockSpec(memory_space=pl.ANY)],
            out_specs=pl.BlockSpec((TMM, D), lambda m, st, nu: (m, 0)),
            scratch_shapes=[pltpu.VMEM((2, TMM, D), F32), pltpu.SemaphoreType.DMA((2,))]),
        out_shape=jax.ShapeDtypeStruct((NSLOT, D), BF16),
        compiler_params=_cparams(("arbitrary",)),
        name="moe_gather",
    )(slot_tok, nused, h2)


def _new_expert(blk_e_ref, m):
    return (m == 0) | (blk_e_ref[m] != blk_e_ref[jnp.maximum(m - 1, 0)])


def _gm1_kernel(blk_e_ref, nused_ref, x_ref, w_ref, b_ref, sel_ref, o_ref, wb_scr):
    m = pl.program_id(1)
    used = m < nused_ref[0]

    @pl.when(used & _new_expert(blk_e_ref, m))
    def _():
        wb_scr[...] = w_ref[...].astype(BF16)

    @pl.when(used)
    def _():
        hd = jnp.dot(x_ref[...], wb_scr[...], preferred_element_type=F32) + b_ref[...]
        glu = jnp.minimum(hd, SWIGLU_LIMIT)
        lin = jnp.clip(hd, -SWIGLU_LIMIT, SWIGLU_LIMIT) + 1.0
        act = glu * _sigmoid(SWIGLU_ALPHA * glu)
        for c in range(TN_G1 // 512):
            parts = []
            for q in range(4):
                sl = slice(c * 512 + q * 128, c * 512 + (q + 1) * 128)
                parts.append(act[:, sl] * pltpu.roll(lin[:, sl], 127, axis=1))
            z = jnp.concatenate(parts, axis=1).astype(BF16)
            o_ref[:, c * 256:(c + 1) * 256] = jnp.dot(z, sel_ref[...], preferred_element_type=F32).astype(BF16)

    @pl.when(jnp.logical_not(used))
    def _():
        o_ref[...] = jnp.zeros(o_ref.shape, BF16)


def _gm1(blk_e, nused, xs, w1, b1, sel):
    def meff(m, nu):
        return jnp.minimum(m, nu[0] - 1)

    return pl.pallas_call(
        _gm1_kernel,
        grid_spec=pltpu.PrefetchScalarGridSpec(
            num_scalar_prefetch=2,
            grid=(2 * D_FF // TN_G1, N_BLK),
            in_specs=[pl.BlockSpec((TMM, D), lambda f, m, be, nu: (meff(m, nu), 0)),
                      pl.BlockSpec((None, D, TN_G1), lambda f, m, be, nu: (be[meff(m, nu)], 0, f)),
                      pl.BlockSpec((None, 1, TN_G1), lambda f, m, be, nu: (be[meff(m, nu)], 0, f)),
                      pl.BlockSpec((512, 256), lambda f, m, be, nu: (0, 0))],
            out_specs=pl.BlockSpec((TMM, TN_G1 // 2), lambda f, m, be, nu: (m, f)),
            scratch_shapes=[pltpu.VMEM((D, TN_G1), BF16)]),
        out_shape=jax.ShapeDtypeStruct((NSLOT, D_FF), BF16),
        compiler_params=_cparams(("arbitrary", "arbitrary")),
        name="moe_up",
    )(blk_e, nused, xs, w1, b1, sel)


def _gm2_kernel(blk_e_ref, nused_ref, x_ref, w_ref, b_ref, o_ref, wb_scr):
    m = pl.program_id(1)
    used = m < nused_ref[0]

    @pl.when(used & _new_expert(blk_e_ref, m))
    def _():
        wb_scr[...] = w_ref[...].astype(BF16)

    @pl.when(used)
    def _():
        o_ref[...] = jnp.dot(x_ref[...], wb_scr[...], preferred_element_type=F32) + b_ref[...]

    @pl.when(jnp.logical_not(used))
    def _():
        o_ref[...] = jnp.zeros(o_ref.shape, F32)


def _gm2(blk_e, nused, hid, w2, b2):
    def meff(m, nu):
        return jnp.minimum(m, nu[0] - 1)

    return pl.pallas_call(
        _gm2_kernel,
        grid_spec=pltpu.PrefetchScalarGridSpec(
            num_scalar_prefetch=2,
            grid=(D // TN_G2, N_BLK),
            in_specs=[pl.BlockSpec((TMM, D_FF), lambda f, m, be, nu: (meff(m, nu), 0)),
                      pl.BlockSpec((None, D_FF, TN_G2), lambda f, m, be, nu: (be[meff(m, nu)], 0, f)),
                      pl.BlockSpec((None, 1, TN_G2), lambda f, m, be, nu: (be[meff(m, nu)], 0, f))],
            out_specs=pl.BlockSpec((TMM, TN_G2), lambda f, m, be, nu: (m, f)),
            scratch_shapes=[pltpu.VMEM((D_FF, TN_G2), BF16)]),
        out_shape=jax.ShapeDtypeStruct((NSLOT, D), F32),
        compiler_params=_cparams(("arbitrary", "arbitrary")),
        name="moe_down",
    )(blk_e, nused, hid, w2, b2)


def _combine_kernel(dest_ref, x1_ref, tg_ref, fg_ref, y_hbm, o_ref, buf, sem):
    m = pl.program_id(0)
    cur = m % 2

    def issue(blk, slot):
        def row(r, _):
            for k in range(TOP_K):
                d = dest_ref[(blk * TMC + r) * TOP_K + k]
                pltpu.make_async_copy(y_hbm.at[pl.ds(d, 1), :], buf.at[slot, pl.ds(k * TMC + r, 1), :],
                                      sem.at[slot]).start()
            return 0

        lax.fori_loop(0, TMC, row, 0, unroll=4)

    @pl.when(m == 0)
    def _():
        issue(m, 0)

    @pl.when(m + 1 < pl.num_programs(0))
    def _():
        issue(m + 1, 1 - cur)

    pltpu.make_async_copy(y_hbm.at[pl.ds(0, TOP_K * TMC), :], buf.at[cur], sem.at[cur]).wait()
    x2 = x1_ref[...]
    tg = tg_ref[...]
    for k in range(TOP_K):
        x2 = x2 + tg[:, k:k + 1] * buf[cur, k * TMC:(k + 1) * TMC, :]
    ms = jnp.mean(x2 * x2, axis=-1, keepdims=True)
    o_ref[...] = (x2 * lax.rsqrt(ms + RMS_EPS)) * fg_ref[...]


def _combine(dest, x1, tg, fg, y):
    return pl.pallas_call(
        _combine_kernel,
        grid_spec=pltpu.PrefetchScalarGridSpec(
            num_scalar_prefetch=1,
            grid=(ROWS // TMC,),
            in_specs=[pl.BlockSpec((TMC, D), lambda m, d: (m, 0)),
                      pl.BlockSpec((TMC, 128), lambda m, d: (m, 0)),
                      pl.BlockSpec((1, D), lambda m, d: (0, 0)),
                      pl.BlockSpec(memory_space=pl.ANY)],
            out_specs=pl.BlockSpec((TMC, D), lambda m, d: (m, 0)),
            scratch_shapes=[pltpu.VMEM((2, TOP_K * TMC, D), F32), pltpu.SemaphoreType.DMA((2,))]),
        out_shape=jax.ShapeDtypeStruct((ROWS, D), F32),
        compiler_params=_cparams(("arbitrary",)),
        name="moe_combine",
    )(dest, x1, tg, fg, y)


def _t5_bucket(rel):
    nb = N_BUCKETS // 2
    ret = jnp.where(rel > 0, nb, 0)
    n = jnp.abs(rel)
    max_exact = nb // 2
    nf = jnp.maximum(n, 1).astype(jnp.float32)
    large = max_exact + (jnp.log(nf / max_exact) / math.log(MAX_DISTANCE / max_exact) * (nb - max_exact)).astype(jnp.int32)
    large = jnp.minimum(large, nb - 1)
    return ret + jnp.where(n < max_exact, n, large)


def _valid_rows():
    rows = [np.arange(b * T_PAD + FRONT, (b + 1) * T_PAD) for b in range(NB)]
    rows.append(np.arange(ROWS_P, ROWS))
    return np.concatenate(rows).astype(np.int32)


def kernel(x_prompt, x_sample, cache_k, cache_v, state_wkv, state_shift, meta_tokens, rel_bias_table, norm1_g, w_in, shift_mu, decay_base, w_decay_up, a_base, w_a_up, w_g_up, k_k, k_a, r_k, gn_w, gn_b, lam_q1, lam_k1, lam_q2, lam_k2, subln_g, w_out, norm2_g, w_router, b_router, w_e1, b_e1, w_e2, b_e2, final_g):
    xp = jnp.concatenate([jnp.zeros((NB, FRONT, D), F32),
                          jnp.broadcast_to(meta_tokens[None], (NB, N_META, D)), x_prompt], axis=1)
    x_all = jnp.concatenate([xp.reshape(ROWS_P, D), x_sample.reshape(ROWS_S, D)], axis=0)
    lora0 = OFF_R + 3 * D
    w_p = jnp.concatenate([w_in[0][:, :lora0], w_in[0][:, lora0 + LORA_W:], w_in[0][:, lora0:lora0 + LORA_W],
                           jnp.zeros((D, LORA_P - LORA_W), F32)], axis=1).astype(BF16)

    p_all = _inproj(x_all, norm1_g, w_p)

    kq = jnp.arange(TQ, dtype=jnp.int32)
    bkt = jnp.stack([_t5_bucket(kq[:, None] - kq[None, :]), _t5_bucket(kq[:, None] - kq[None, :] - TQ)])
    lamp = jnp.concatenate([lam_q1, lam_k1, lam_q2, lam_k2], axis=0)
    oa_p = _attn_prompt(p_all, rel_bias_table, bkt, lamp, subln_g)
    qpos = PAST + jnp.arange(DEC_S, dtype=jnp.int32)
    bkt_last = _t5_bucket((PAST - NEAR + jnp.arange(NEAR, dtype=jnp.int32))[None, :] - qpos[:, None])
    bkt_new = _t5_bucket(qpos[None, :] - qpos[:, None])
    oa_s = _attn_sample(p_all, cache_k[0], cache_v[0],
                        rel_bias_table, bkt_last, bkt_new, lamp, subln_g)

    mu = shift_mu[0]
    mu3 = mu[:3 * D][None]
    mul = jnp.pad(mu[3 * D:], (0, LORA_P - LORA_W))[None]
    wd = jnp.zeros((LORA_P, D), F32).at[:DECAY_LORA].set(w_decay_up[0]).astype(BF16)
    wa = jnp.zeros((LORA_P, D), F32).at[DECAY_LORA:DECAY_LORA + AAA_LORA].set(w_a_up[0]).astype(BF16)
    wg = jnp.zeros((LORA_P, D), F32).at[DECAY_LORA + AAA_LORA:LORA_W].set(w_g_up[0]).astype(BF16)
    first_p = jnp.zeros((NB, 1, 3 * D + LORA_P), F32)
    first_s = jnp.pad(state_shift[0], ((0, 0), (0, LORA_P - LORA_W)))[:, None, :]
    prep_p = _rwkv_prep(p_all, first_p, mu3, mul, decay_base, a_base, wd, wa, wg, NB, T_PAD // 256, 256, 0, True)
    prep_s = _rwkv_prep(p_all, first_s, mu3, mul, decay_base, a_base, wd, wa, wg, DEC_B, 1, DEC_S, ROWS_P // DEC_S,
                        False)

    def chain_tile(v):
        return jnp.tile(v.reshape(B_HEADS, B_HEAD).T, (1, 128 // B_HEADS))

    par = [chain_tile(v.reshape(-1)) for v in (k_k[0], k_a[0], r_k[0], gn_w[0], gn_b[0])]
    s0_p = jnp.zeros((B_HEAD, B_HEAD, NB * B_HEADS), F32)
    s0_s = state_wkv[0].transpose(3, 2, 0, 1).reshape(B_HEAD, B_HEAD, DEC_B * B_HEADS)
    yo_p, sfin_p = _rwkv_rec(*[a.transpose(2, 1, 0) for a in prep_p[:5]], *par, s0_p, skip=FRONT // TB)
    yo_s, sfin_s = _rwkv_rec(*[_to_chains(a, DEC_B, DEC_S) for a in prep_s[:5]], *par, s0_s)

    wo = w_out[0].astype(BF16)
    wr = jnp.pad(w_router[0], ((0, 0), (0, 128 - N_EXPERTS)))
    br = jnp.concatenate([b_router[0], jnp.full((128 - N_EXPERTS,), NEG, F32)])[None]
    x1, h2, te, tg = _mix(yo_p.transpose(2, 1, 0), _from_chains(yo_s, DEC_B, DEC_S), prep_p[5], prep_s[5],
                          oa_p, oa_s, p_all, x_all, wo, norm2_g, wr, br)

    valid = jnp.asarray(_valid_rows())
    flat_e = te[valid, :TOP_K].reshape(-1)
    order = jnp.argsort(flat_e)
    e_sorted = flat_e[order]
    tok_sorted = valid[order // TOP_K]
    counts = jnp.zeros((N_EXPERTS,), jnp.int32).at[flat_e].add(1)
    padded = (counts + TMM - 1) // TMM * TMM
    pad_end = jnp.cumsum(padded)
    pad_start = pad_end - padded
    grp_start = jnp.cumsum(counts) - counts
    dest_sorted = pad_start[e_sorted] + jnp.arange(NK, dtype=jnp.int32) - grp_start[e_sorted]
    slot_tok = jnp.zeros((NSLOT,), jnp.int32).at[dest_sorted].set(tok_sorted)
    blk_start = jnp.arange(N_BLK, dtype=jnp.int32) * TMM
    blk_e = jnp.minimum(jnp.searchsorted(pad_end, blk_start, side='right'), N_EXPERTS - 1).astype(jnp.int32)
    nused = (pad_end[-1:] // TMM).astype(jnp.int32)
    dest_tok = jnp.zeros((NK,), jnp.int32).at[order].set(dest_sorted).reshape(N_TOK, TOP_K)
    dest = jnp.zeros((ROWS, TOP_K), jnp.int32).at[valid].set(dest_tok).reshape(-1)

    sel_np = np.zeros((512, 256), np.float32)
    sel_np[2 * np.arange(256), np.arange(256)] = 1.0
    xs = _gather(slot_tok, nused, h2)
    hid = _gm1(blk_e, nused, xs, w_e1[0], b_e1[0][:, None, :], jnp.asarray(sel_np, BF16))
    ys = _gm2(blk_e, nused, hid, w_e2[0], b_e2[0][:, None, :])
    y_all = _combine(dest, x1, tg, final_g[None], ys)

    y_prompt = y_all[:ROWS_P].reshape(NB, T_PAD, D)[:, FRONT + N_META:]
    y_sample = y_all[ROWS_P:].reshape(DEC_B, DEC_S, D)
    def prompt_rows(off):
        sec = p_all[:ROWS_P, off:off + D].reshape(NB, T_PAD, D)[:, FRONT:]
        return sec.reshape(1, NB, N_META + SEQ, A_HEADS, A_VD)

    def sample_rows(off):
        return p_all[ROWS_P:, off:off + D].reshape(1, DEC_B, DEC_S, A_HEADS, A_VD)

    def last_ps(last):
        return last[:, 0, :SHIFT_W][None]

    k_prompt, v_prompt = prompt_rows(OFF_K), prompt_rows(OFF_V)
    k_sample, v_sample = sample_rows(OFF_K), sample_rows(OFF_V)
    shift_prompt = last_ps(prep_p[6])
    shift_sample = last_ps(prep_s[6])
    wkv_prompt = sfin_p.reshape(B_HEAD, B_HEAD, NB, B_HEADS).transpose(2, 3, 1, 0)[None]
    wkv_sample = sfin_s.reshape(B_HEAD, B_HEAD, DEC_B, B_HEADS).transpose(2, 3, 1, 0)[None].astype(state_wkv.dtype)
    return (y_prompt, y_sample, k_prompt, v_prompt, wkv_prompt, shift_prompt,
            k_sample, v_sample, wkv_sample, shift_sample)
```

```python
import functools
import math

import numpy as np
import jax
import jax.numpy as jnp
from jax import lax
from jax.experimental import pallas as pl
from jax.experimental.pallas import tpu as pltpu

F32 = jnp.float32
BF16 = jnp.bfloat16

D = 2048
NB = 4
SEQ = 2048
DEC_B = 16
DEC_S = 32
PAST = 2048
CHUNK = 64
N_META = 16
RMS_EPS = 1e-5
A_DH = 64
A_HEADS = 16
A_VD = 128
B_HEAD = 64
B_HEADS = 32
DECAY_LORA = 96
AAA_LORA = 96
GATE_LORA = 256
GN_EPS = 64e-5
N_BUCKETS = 32
MAX_DISTANCE = 128
N_EXPERTS = 32
TOP_K = 4
D_FF = 2048
SWIGLU_ALPHA = 1.702
SWIGLU_LIMIT = 7.0
LAM_INIT = 0.8 - 0.6 * math.exp(-0.3 * 0)
SHIFT_W = 3 * D + DECAY_LORA + AAA_LORA + GATE_LORA
LORA_W = DECAY_LORA + AAA_LORA + GATE_LORA
LORA_P = 512

T_PAD = 2304
FRONT = T_PAD - N_META - SEQ
ROWS_P = NB * T_PAD
ROWS_S = DEC_B * DEC_S
ROWS = ROWS_P + ROWS_S

OFF_Q = 0
OFF_K = 2048
OFF_V = 4096
OFF_R = 6144
OFF_KX = 8192
OFF_VX = 10240
OFF_GA = 12288
OFF_GB = 14336
OFF_LORA = 16384
PW = OFF_LORA + LORA_P

NEG = -1e30
LOG2E = 1.4426950408889634
VMEM_LIMIT = 56 * 1024 * 1024


def _cparams(sem, vmem=VMEM_LIMIT):
    return pltpu.CompilerParams(dimension_semantics=sem, vmem_limit_bytes=vmem)


def _sigmoid(x):
    return 1.0 / (1.0 + jnp.exp(-x))


TM1 = 512
TN1 = 1536
N_MAIN = OFF_GA // TN1


def _inproj_kernel(x_ref, g_ref, wm_ref, wt_ref, o_ref, h_scr):
    j = pl.program_id(1)

    @pl.when(j == 0)
    def _():
        x = x_ref[...]
        ms = jnp.mean(x * x, axis=-1, keepdims=True)
        h_scr[...] = ((x * lax.rsqrt(ms + RMS_EPS)) * g_ref[...]).astype(BF16)

    @pl.when(j < N_MAIN)
    def _():
        o_ref[...] = jnp.dot(h_scr[...], wm_ref[...], preferred_element_type=F32)

    @pl.when(j >= N_MAIN)
    def _():
        o_ref[...] = jnp.dot(h_scr[...], wt_ref[...], preferred_element_type=F32)


def _inproj(x_all, g, w_main, w_tail):
    return pl.pallas_call(
        _inproj_kernel,
        grid=(ROWS // TM1, PW // TN1),
        in_specs=[pl.BlockSpec((TM1, D), lambda i, j: (i, 0)),
                  pl.BlockSpec((1, D), lambda i, j: (0, 0)),
                  pl.BlockSpec((D, TN1), lambda i, j: (0, jnp.minimum(j, N_MAIN - 1))),
                  pl.BlockSpec((D, TN1), lambda i, j: (0, jnp.maximum(j - N_MAIN, 0)))],
        out_specs=pl.BlockSpec((TM1, TN1), lambda i, j: (i, j)),
        out_shape=jax.ShapeDtypeStruct((ROWS, PW), F32),
        scratch_shapes=[pltpu.VMEM((TM1, D), BF16)],
        compiler_params=_cparams(("arbitrary", "arbitrary")),
        name="inproj",
    )(x_all, g, w_main, w_tail)


def _lam_value(lamp_ref):
    lp = lamp_ref[...]
    s1 = jnp.sum(lp[0:1, :] * lp[1:2, :], axis=-1, keepdims=True)
    s2 = jnp.sum(lp[2:3, :] * lp[3:4, :], axis=-1, keepdims=True)
    return jnp.exp(s1) - jnp.exp(s2) + LAM_INIT


def _bias_from_buckets(bkt, table_ref, h):
    out = jnp.zeros(bkt.shape, F32)
    for n in range(N_BUCKETS):
        out = jnp.where(bkt == n, table_ref[n, h], out)
    return out


TQ = 256
NQB = T_PAD // TQ
BIAS_META, BIAS_DIAG, BIAS_SUB, BIAS_SUB_PAD, BIAS_FAR_PAD, BIAS_FAR = range(6)


def _attn_prompt_kernel(table_ref, q_ref, k_ref, v_ref, bkt_ref, lamp_ref, subg_ref, o_ref,
                        kb_scr, vt_scr, bias_scr, s_scr, acc_scr):
    h = pl.program_id(0)
    b = pl.program_id(1)

    @pl.when(b == 0)
    def _():
        kk = lax.broadcasted_iota(jnp.int32, (TQ, TQ), 0)
        qq = lax.broadcasted_iota(jnp.int32, (TQ, TQ), 1)
        chunk_ok = (kk // CHUNK) <= (qq // CHUNK)
        pad = jnp.where(kk < FRONT, NEG, 0.0).astype(F32)
        b_diag = _bias_from_buckets(bkt_ref[0], table_ref, h)
        b_sub = _bias_from_buckets(bkt_ref[1], table_ref, h)
        b_far = jnp.zeros((TQ, TQ), F32) + table_ref[N_BUCKETS // 2 - 1, h]
        tiles = (jnp.where(chunk_ok, b_diag, NEG) + pad,
                 jnp.where(chunk_ok, b_diag, NEG),
                 b_sub,
                 b_sub + pad,
                 b_far + pad,
                 b_far)
        for i, t in enumerate(tiles):
            bias_scr[i] = jnp.concatenate([t, t], axis=1) * LOG2E

    kb_scr[...] = k_ref[...].astype(BF16)
    for i in range(NQB):
        vt_scr[i] = v_ref[i * TQ:(i + 1) * TQ, :].T.astype(BF16)

    lam = _lam_value(lamp_ref)
    drow = lax.broadcasted_iota(jnp.int32, (A_VD, TQ), 0)

    def q_operand(qi):
        q0 = pl.multiple_of(qi * TQ, TQ)
        qt = (q_ref[pl.ds(q0, TQ), :] * (A_DH ** -0.5 * LOG2E)).T
        q1 = jnp.where(drow < A_DH, qt, 0.0)
        q2 = jnp.where(drow >= A_DH, qt, 0.0)
        return jnp.concatenate([q1, q2], axis=1).astype(BF16)

    def scores(kj, qpt, slot):
        k0 = pl.multiple_of(kj * TQ, TQ)
        s_scr[slot] = jnp.dot(kb_scr[pl.ds(k0, TQ), :], qpt, preferred_element_type=F32)

    def accumulate(kj, bias_id, slot, ml):
        m, l = ml
        s = s_scr[slot] + bias_scr[bias_id]
        m_new = jnp.maximum(m, jnp.max(s, axis=0, keepdims=True))
        alpha = jnp.exp2(m - m_new)
        p = jnp.exp2(s - m_new)
        l = l * alpha + jnp.sum(p, axis=0, keepdims=True)
        acc_scr[...] = acc_scr[...] * alpha + jnp.dot(vt_scr[kj], p.astype(BF16), preferred_element_type=F32)
        return m_new, l

    def finish(qi, ml):
        on = acc_scr[...] / ml[1]
        ot = on[:, :TQ] - lam * on[:, TQ:]
        ms = jnp.mean(ot * ot, axis=0, keepdims=True)
        o = (ot * lax.rsqrt(ms + RMS_EPS)).T * subg_ref[...] * (1.0 - LAM_INIT)
        q0 = pl.multiple_of(qi * TQ, TQ)
        o_ref[pl.ds(q0, TQ), :] = o

    def start(qi):
        qpt = q_operand(qi)
        acc_scr[...] = jnp.zeros(acc_scr.shape, F32)
        scores(0, qpt, 0)
        return qpt, (jnp.full((1, 2 * TQ), NEG, F32), jnp.zeros((1, 2 * TQ), F32))

    finish(0, accumulate(0, BIAS_META, 0, start(0)[1]))

    def q_block(qi, _):
        qpt, ml = start(qi)

        def half(kj, cur, ml):
            scores(kj + 1, qpt, 1 - cur)
            first = kj == 0
            bias_id = jnp.where(kj == qi - 1, jnp.where(first, BIAS_SUB_PAD, BIAS_SUB),
                                jnp.where(first, BIAS_FAR_PAD, BIAS_FAR))
            return accumulate(kj, bias_id, cur, ml)

        ml = lax.fori_loop(0, qi // 2, lambda i, ml: half(2 * i + 1, 1, half(2 * i, 0, ml)), ml)
        ml = lax.cond(qi % 2 == 1,
                      lambda ml: accumulate(qi, BIAS_DIAG, 1, half(qi - 1, 0, ml)),
                      lambda ml: accumulate(qi, BIAS_DIAG, 0, ml), ml)
        finish(qi, ml)
        return 0

    lax.fori_loop(1, NQB, q_block, 0)


def _attn_prompt(p_all, table, bkt, lamp, subg):
    cb = D // A_VD
    return pl.pallas_call(
        _attn_prompt_kernel,
        grid_spec=pltpu.PrefetchScalarGridSpec(
            num_scalar_prefetch=0,
            grid=(A_HEADS, NB),
            in_specs=[pl.BlockSpec(memory_space=pltpu.SMEM),
                      pl.BlockSpec((T_PAD, A_VD), lambda h, b: (b, h)),
                      pl.BlockSpec((T_PAD, A_VD), lambda h, b: (b, cb + h)),
                      pl.BlockSpec((T_PAD, A_VD), lambda h, b: (b, 2 * cb + h)),
                      pl.BlockSpec((2, TQ, TQ), lambda h, b: (0, 0, 0)),
                      pl.BlockSpec((4, A_DH), lambda h, b: (0, 0)),
                      pl.BlockSpec((1, A_VD), lambda h, b: (0, 0))],
            out_specs=pl.BlockSpec((T_PAD, A_VD), lambda h, b: (b, h)),
            scratch_shapes=[pltpu.VMEM((T_PAD, A_VD), BF16),
                            pltpu.VMEM((NQB, A_VD, TQ), BF16),
                            pltpu.VMEM((6, TQ, 2 * TQ), F32),
                            pltpu.VMEM((2, TQ, 2 * TQ), F32),
                            pltpu.VMEM((A_VD, 2 * TQ), F32)]),
        out_shape=jax.ShapeDtypeStruct((ROWS_P, D), F32),
        compiler_params=_cparams(("arbitrary", "arbitrary")),
        name="attn_prompt",
    )(table, p_all, p_all, p_all, bkt, lamp, subg)


HG = 4
NEAR = 512


def _attn_sample_kernel(table_ref, q_ref, kn_ref, vn_ref, bl_ref, bn_ref, lamp_ref, subg_ref, ck_hbm, cv_hbm,
                        o_ref, kbuf, vbuf, sem):
    hg = pl.program_id(1)
    ng = pl.num_programs(1)
    step = pl.program_id(0) * ng + hg
    nsteps = pl.num_programs(0) * ng

    def copies(n, slot):
        out = []
        for i in range(HG):
            h = (n % ng) * HG + i
            out.append(pltpu.make_async_copy(ck_hbm.at[n // ng, :, h, :], kbuf.at[slot, i], sem.at[slot, i]))
            out.append(pltpu.make_async_copy(cv_hbm.at[n // ng, :, h, :], vbuf.at[slot, i], sem.at[slot, HG + i]))
        return out

    cur = step % 2

    @pl.when(step == 0)
    def _():
        for c in copies(step, 0):
            c.start()

    @pl.when(step + 1 < nsteps)
    def _():
        for c in copies(step + 1, 1 - cur):
            c.start()

    for c in copies(step, cur):
        c.wait()
    ck_refs = [kbuf.at[cur, i] for i in range(HG)]
    cv_refs = [vbuf.at[cur, i] for i in range(HG)]
    lam = _lam_value(lamp_ref)
    lane = lax.broadcasted_iota(jnp.int32, (DEC_S, A_VD), 1)
    nt = (((1,), (1,)), ((), ()))
    for i in range(HG):
        h = hg * HG + i
        cols = slice(i * A_VD, (i + 1) * A_VD)
        q = q_ref[:, cols] * (A_DH ** -0.5)
        qp = jnp.concatenate([jnp.where(lane < A_DH, q, 0.0), jnp.where(lane >= A_DH, q, 0.0)],
                             axis=0).astype(BF16)
        far = jnp.zeros((DEC_S, PAST - NEAR), F32) + table_ref[N_BUCKETS // 2 - 1, h]
        bl = jnp.concatenate([far, _bias_from_buckets(bl_ref[...], table_ref, h)], axis=1)
        bn = _bias_from_buckets(bn_ref[...], table_ref, h)
        s = (lax.dot_general(qp, ck_refs[i][...].astype(BF16), nt, preferred_element_type=F32)
             + jnp.concatenate([bl, bl], axis=0))
        sn = (lax.dot_general(qp, kn_ref[:, cols].astype(BF16), nt, preferred_element_type=F32)
              + jnp.concatenate([bn, bn], axis=0))
        m = jnp.maximum(jnp.max(s, axis=-1, keepdims=True), jnp.max(sn, axis=-1, keepdims=True))
        p = jnp.exp(s - m)
        pn = jnp.exp(sn - m)
        l = jnp.sum(p, axis=-1, keepdims=True) + jnp.sum(pn, axis=-1, keepdims=True)
        acc = (jnp.dot(p.astype(BF16), cv_refs[i][...].astype(BF16), preferred_element_type=F32)
               + jnp.dot(pn.astype(BF16), vn_ref[:, cols].astype(BF16), preferred_element_type=F32))
        on = acc / l
        o = on[:DEC_S] - lam * on[DEC_S:]
        ms = jnp.mean(o * o, axis=-1, keepdims=True)
        o_ref[:, cols] = (o * lax.rsqrt(ms + RMS_EPS)) * subg_ref[...] * (1.0 - LAM_INIT)


def _attn_sample(p_all, cache_k, cache_v, table, bkt_last, bkt_new, lamp, subg):
    r0 = ROWS_P // DEC_S
    w = HG * A_VD
    new = lambda off: pl.BlockSpec((DEC_S, w), lambda b, g: (r0 + b, off // w + g))
    return pl.pallas_call(
        _attn_sample_kernel,
        grid=(DEC_B, A_HEADS // HG),
        in_specs=[pl.BlockSpec(memory_space=pltpu.SMEM), new(OFF_Q), new(OFF_K), new(OFF_V),
                  pl.BlockSpec((DEC_S, NEAR), lambda b, g: (0, 0)),
                  pl.BlockSpec((DEC_S, DEC_S), lambda b, g: (0, 0)),
                  pl.BlockSpec((4, A_DH), lambda b, g: (0, 0)),
                  pl.BlockSpec((1, A_VD), lambda b, g: (0, 0)),
                  pl.BlockSpec(memory_space=pl.ANY), pl.BlockSpec(memory_space=pl.ANY)],
        out_specs=pl.BlockSpec((DEC_S, w), lambda b, g: (b, g)),
        out_shape=jax.ShapeDtypeStruct((ROWS_S, D), F32),
        scratch_shapes=[pltpu.VMEM((2, HG, PAST, A_VD), F32), pltpu.VMEM((2, HG, PAST, A_VD), F32),
                        pltpu.SemaphoreType.DMA((2, 2 * HG))],
        compiler_params=_cparams(("arbitrary", "arbitrary")),
        name="attn_sample",
    )(table, p_all, p_all, p_all, bkt_last, bkt_new, lamp, subg, cache_k, cache_v)


def _rwkv_prep_kernel(chain_out, r_ref, k_ref, v_ref, lo_ref, first_ref, mu_ref, mul_ref, dbase_ref, abase_ref,
                      wd_ref, wa_ref, wg_ref,
                      xr_ref, xk_ref, xv_ref, wp_ref, ap_ref, g_ref, last_ref, carry_scr, carryl_scr):
    i = pl.program_id(1)
    tm = r_ref.shape[0]
    row = lax.broadcasted_iota(jnp.int32, (tm, 1), 0)

    def shift(x, first, mu):
        prev = jnp.where(row == 0, first, pltpu.roll(x, 1, axis=0))
        return x + (prev - x) * mu

    def put(dst, val):
        if chain_out:
            for hp in range(B_HEADS // 2):
                t = val[:, hp * 128:(hp + 1) * 128].T
                dst[2 * hp] = t[:B_HEAD]
                dst[2 * hp + 1] = t[B_HEAD:]
        else:
            dst[...] = val

    for s, (src, dst) in enumerate(((r_ref, xr_ref), (k_ref, xk_ref), (v_ref, xv_ref))):
        x = src[...]
        first = jnp.where(i == 0, first_ref[:, s * D:(s + 1) * D], carry_scr[:, s * D:(s + 1) * D])
        put(dst, shift(x, first, mu_ref[:, s * D:(s + 1) * D]))
        carry_scr[:, s * D:(s + 1) * D] = x[tm - 1:tm, :]
        last_ref[:, s * D:(s + 1) * D] = x[tm - 1:tm, :]

    xl = lo_ref[...]
    first = jnp.where(i == 0, first_ref[:, 3 * D:], carryl_scr[...])
    xs = shift(xl, first, mul_ref[...])
    carryl_scr[...] = xl[tm - 1:tm, :]
    last_ref[:, 3 * D:] = xl[tm - 1:tm, :]
    put(wp_ref, dbase_ref[...] + jnp.dot(jnp.tanh(xs).astype(BF16), wd_ref[...], preferred_element_type=F32))
    put(ap_ref, abase_ref[...] + jnp.dot(xs.astype(BF16), wa_ref[...], preferred_element_type=F32))
    g_ref[...] = jnp.dot(_sigmoid(xs).astype(BF16), wg_ref[...], preferred_element_type=F32)


def _rwkv_prep(p_all, first, mu, mul, dbase, abase, wd, wa, wg, nseq, nblk, tm, row_blk0, chain_out):
    rows = nseq * nblk * tm
    cs = lambda off: (lambda s, i: (row_blk0 + s * nblk + i, off))
    full = lambda shp: pl.BlockSpec(shp, lambda s, i: tuple(0 for _ in shp))
    out = jax.ShapeDtypeStruct((rows, D), F32)
    ospec = pl.BlockSpec((tm, D), lambda s, i: (s * nblk + i, 0))
    if chain_out:
        cout = jax.ShapeDtypeStruct((nseq * B_HEADS, B_HEAD, nblk * tm), F32)
        cspec = pl.BlockSpec((B_HEADS, B_HEAD, tm), lambda s, i: (s, 0, i))
    else:
        cout, cspec = out, ospec
    return pl.pallas_call(
        functools.partial(_rwkv_prep_kernel, chain_out),
        grid=(nseq, nblk),
        in_specs=[pl.BlockSpec((tm, D), cs(OFF_R // D)),
                  pl.BlockSpec((tm, D), cs(OFF_KX // D)),
                  pl.BlockSpec((tm, D), cs(OFF_VX // D)),
                  pl.BlockSpec((tm, LORA_P), cs(OFF_LORA // LORA_P)),
                  pl.BlockSpec((None, 1, 3 * D + LORA_P), lambda s, i: (s, 0, 0)),
                  full((1, 3 * D)), full((1, LORA_P)), full((1, D)), full((1, D)),
                  full((LORA_P, D)), full((LORA_P, D)), full((LORA_P, D))],
        out_specs=[cspec] * 5 + [ospec, pl.BlockSpec((None, 1, 3 * D + LORA_P), lambda s, i: (s, 0, 0))],
        out_shape=[cout] * 5 + [out, jax.ShapeDtypeStruct((nseq, 1, 3 * D + LORA_P), F32)],
        scratch_shapes=[pltpu.VMEM((1, 3 * D), F32), pltpu.VMEM((1, LORA_P), F32)],
        compiler_params=_cparams(("arbitrary", "arbitrary")),
        name="rwkv_prep",
    )(p_all, p_all, p_all, p_all, first, mu, mul, dbase, abase, wd, wa, wg)


TB = 32
IH = 32


def _rwkv_rec_kernel(skip, *refs):
    tb = pl.program_id(1)
    s0_ref, o_ref, sfin_ref, S = refs[10], refs[11], refs[12], refs[13]

    @pl.when(tb == 0)
    def _():
        S[...] = s0_ref[...]

    if skip:
        @pl.when(tb < skip)
        def _():
            o_ref[...] = jnp.zeros(o_ref.shape, F32)

        pl.when(tb >= skip)(functools.partial(_rwkv_rec_block, *refs))
    else:
        _rwkv_rec_block(*refs)
    sfin_ref[...] = S[...]


def _rwkv_rec_block(xr_ref, xk_ref, xv_ref, wp_ref, ap_ref, kk_ref, ka_ref, rk_ref, gnw_ref, gnb_ref, s0_ref,
                    o_ref, sfin_ref, S, w_s, k_s, a_s, b_s, sa_s, y_s):
    z = -wp_ref[...]
    softplus = jnp.maximum(z, 0.0) + jnp.log(1.0 + jnp.exp(-jnp.abs(z)))
    w_s[...] = jnp.exp(-jnp.exp(-softplus - 0.5))
    a = _sigmoid(ap_ref[...])
    xk = xk_ref[...]
    kk = xk * kk_ref[...]
    nrm = jnp.sqrt(jnp.sum(kk * kk, axis=1, keepdims=True))
    kk = kk / jnp.maximum(nrm, 1e-12)
    k_s[...] = xk * (1.0 + (a - 1.0) * ka_ref[...])
    a_s[...] = -kk
    b_s[...] = kk * a

    for hf in range(B_HEAD // IH):
        i0 = hf * IH
        acc = jnp.zeros((IH, 128), F32)
        for j in range(B_HEAD):
            acc = acc + S[j, i0:i0 + IH, :] * a_s[0, j:j + 1, :]
        sa_s[0, i0:i0 + IH, :] = acc

    def token(t, _):
        cur = t % 2
        tn = jnp.minimum(t + 1, TB - 1)
        for hf in range(B_HEAD // IH):
            i0 = hf * IH
            sa = sa_s[cur, i0:i0 + IH, :]
            v = xv_ref[t, i0:i0 + IH, :]
            y = jnp.zeros((IH, 128), F32)
            san = jnp.zeros((IH, 128), F32)
            for j in range(B_HEAD):
                s_new = (S[j, i0:i0 + IH, :] * w_s[t, j:j + 1, :] + sa * b_s[t, j:j + 1, :]
                         + v * k_s[t, j:j + 1, :])
                S[j, i0:i0 + IH, :] = s_new
                y = y + s_new * xr_ref[t, j:j + 1, :]
                san = san + s_new * a_s[tn, j:j + 1, :]
            y_s[i0:i0 + IH, :] = y
            sa_s[1 - cur, i0:i0 + IH, :] = san
        y = y_s[...]
        mu = jnp.mean(y, axis=0, keepdims=True)
        yc = y - mu
        var = jnp.mean(yc * yc, axis=0, keepdims=True)
        bonus = jnp.sum(xr_ref[t] * k_s[t] * rk_ref[...], axis=0, keepdims=True)
        o_ref[t] = yc * lax.rsqrt(var + GN_EPS) * gnw_ref[...] + gnb_ref[...] + bonus * xv_ref[t]
        return 0

    lax.fori_loop(0, TB, token, 0)


def _rwkv_rec(xr, xk, xv, wp, ap, kk, ka, rk, gnw, gnb, s0, skip=0):
    t, _, c = xr.shape
    tok = pl.BlockSpec((TB, B_HEAD, 128), lambda g, tb: (tb, 0, g))
    par = pl.BlockSpec((B_HEAD, 128), lambda g, tb: (0, 0))
    st = pl.BlockSpec((B_HEAD, B_HEAD, 128), lambda g, tb: (0, 0, g))
    blk = pltpu.VMEM((TB, B_HEAD, 128), F32)
    return pl.pallas_call(
        functools.partial(_rwkv_rec_kernel, skip),
        grid=(c // 128, t // TB),
        in_specs=[tok] * 5 + [par] * 5 + [st],
        out_specs=[tok, st],
        out_shape=[jax.ShapeDtypeStruct((t, B_HEAD, c), F32), jax.ShapeDtypeStruct((B_HEAD, B_HEAD, c), F32)],
        scratch_shapes=[pltpu.VMEM((B_HEAD, B_HEAD, 128), F32), blk, blk, blk, blk,
                        pltpu.VMEM((2, B_HEAD, 128), F32), pltpu.VMEM((B_HEAD, 128), F32)],
        compiler_params=_cparams(("arbitrary", "arbitrary")),
        name="rwkv_rec",
    )(xr, xk, xv, wp, ap, kk, ka, rk, gnw, gnb, s0)


def _to_chains(a, nb, t):
    return a.reshape(nb, t, B_HEADS, B_HEAD).transpose(1, 3, 0, 2).reshape(t, B_HEAD, nb * B_HEADS)


def _from_chains(a, nb, t):
    return a.reshape(t, B_HEAD, nb, B_HEADS).transpose(2, 0, 3, 1).reshape(nb * t, D)


TM5 = 256


def _mix_kernel(yo_p_ref, yo_s_ref, g_p_ref, g_s_ref, oa_p_ref, oa_s_ref, ga_ref, gb_ref, x_ref,
                wo_ref, n2_ref, wr_ref, br_ref, ltri_ref, x1_ref, h2_ref, te_ref, tg_ref, rank_ref, cnt_ref, cnt_scr):
    i = pl.program_id(0)
    is_p = i < ROWS_P // TM5
    tiles = [jnp.concatenate([yo_p_ref[2 * hp], yo_p_ref[2 * hp + 1]], axis=0).T for hp in range(B_HEADS // 2)]
    yo = jnp.where(is_p, jnp.concatenate(tiles, axis=1), yo_s_ref[...])
    g = jnp.where(is_p, g_p_ref[...], g_s_ref[...])
    oa = jnp.where(is_p, oa_p_ref[...], oa_s_ref[...])
    mixed = _sigmoid(ga_ref[...]) * oa + _sigmoid(gb_ref[...]) * (yo * g)
    x1 = x_ref[...] + jnp.dot(mixed.astype(BF16), wo_ref[...], preferred_element_type=F32)
    x1_ref[...] = x1
    ms = jnp.mean(x1 * x1, axis=-1, keepdims=True)
    h2 = (x1 * lax.rsqrt(ms + RMS_EPS)) * n2_ref[...]
    h2_ref[...] = h2
    logits = jnp.dot(h2, wr_ref[...], preferred_element_type=F32, precision=lax.Precision.HIGHEST) + br_ref[...]
    lane = lax.broadcasted_iota(jnp.int32, logits.shape, 1)
    lane_f = lane.astype(F32)
    te = jnp.zeros(logits.shape, jnp.int32)
    tv = jnp.full(logits.shape, NEG, F32)
    chosen = []
    for k in range(TOP_K):
        m = jnp.max(logits, axis=-1, keepdims=True)
        idx = jnp.min(jnp.where(logits == m, lane_f, 128.0), axis=-1, keepdims=True).astype(jnp.int32)
        te = jnp.where(lane == k, idx, te)
        tv = jnp.where(lane == k, m, tv)
        chosen.append(lane == idx)
        logits = jnp.where(chosen[k], NEG, logits)
    e = jnp.where(lane < TOP_K, jnp.exp(tv - jnp.max(tv, axis=-1, keepdims=True)), 0.0)
    te_ref[...] = te
    tg_ref[...] = e / jnp.sum(e, axis=-1, keepdims=True)

    @pl.when(i == 0)
    def _():
        cnt_scr[...] = jnp.zeros(cnt_scr.shape, F32)

    first_row = jnp.where(is_p, (i % (T_PAD // TM5)) * TM5, FRONT)
    routed = (lax.broadcasted_iota(jnp.int32, (TM5, 1), 0) + first_row) >= FRONT
    hits = jnp.zeros(logits.shape, F32)
    for k in range(TOP_K):
        hits = hits + jnp.where(chosen[k], 1.0, 0.0)
    hits = jnp.where(routed, hits, 0.0)
    before = cnt_scr[...] + jnp.dot(ltri_ref[...], hits.astype(BF16), preferred_element_type=F32)
    rank = jnp.zeros(logits.shape, jnp.int32)
    for k in range(TOP_K):
        rk = jnp.sum(jnp.where(chosen[k], before, 0.0), axis=-1, keepdims=True).astype(jnp.int32)
        rank = jnp.where(lane == k, rk, rank)
    rank_ref[...] = rank
    cnt_scr[...] = cnt_scr[...] + jnp.sum(hits, axis=0, keepdims=True)
    cnt_ref[...] = jnp.broadcast_to(cnt_scr[...], cnt_ref.shape)


def _mix(yo_p, yo_s, g_p, g_s, oa_p, oa_s, p_all, x_all, wo, n2, wr, br):
    nbp = ROWS_P // TM5
    rb = lambda i: (i, 0)
    pb = pl.BlockSpec((TM5, D), lambda i: (jnp.minimum(i, nbp - 1), 0))
    sb = pl.BlockSpec((TM5, D), lambda i: (jnp.maximum(i - nbp, 0), 0))
    full = lambda shp: pl.BlockSpec(shp, lambda i: tuple(0 for _ in shp))
    return pl.pallas_call(
        _mix_kernel,
        grid=(ROWS // TM5,),
        in_specs=[pl.BlockSpec((B_HEADS, B_HEAD, TM5),
                               lambda i: (jnp.minimum(i, nbp - 1) // (T_PAD // TM5), 0,
                                          jnp.minimum(i, nbp - 1) % (T_PAD // TM5))),
                  sb, pb, sb, pb, sb,
                  pl.BlockSpec((TM5, D), lambda i: (i, OFF_GA // D)),
                  pl.BlockSpec((TM5, D), lambda i: (i, OFF_GB // D)),
                  pl.BlockSpec((TM5, D), rb),
                  full((D, D)), full((1, D)), full((D, 128)), full((1, 128)), full((TM5, TM5))],
        out_specs=[pl.BlockSpec((TM5, D), rb), pl.BlockSpec((TM5, D), rb),
                   pl.BlockSpec((TM5, 128), rb), pl.BlockSpec((TM5, 128), rb),
                   pl.BlockSpec((TM5, 128), rb), full((8, 128))],
        out_shape=[jax.ShapeDtypeStruct((ROWS, D), F32), jax.ShapeDtypeStruct((ROWS, D), F32),
                   jax.ShapeDtypeStruct((ROWS, 128), jnp.int32), jax.ShapeDtypeStruct((ROWS, 128), F32),
                   jax.ShapeDtypeStruct((ROWS, 128), jnp.int32), jax.ShapeDtypeStruct((8, 128), F32)],
        scratch_shapes=[pltpu.VMEM((1, 128), F32)],
        compiler_params=_cparams(("arbitrary",)),
        name="mix",
    )(yo_p, yo_s, g_p, g_s, oa_p, oa_s, p_all, p_all, x_all, wo, n2, wr, br,
      jnp.asarray(np.tril(np.ones((TM5, TM5), np.float32), -1), BF16))


N_TOK = NB * (N_META + SEQ) + ROWS_S
NK = N_TOK * TOP_K
TMM = 256
N_BLK = -(-NK // TMM) + N_EXPERTS
NSLOT = N_BLK * TMM
TN_G1 = 1024
TN_G2 = 1024
TMC = 128


def _gather_kernel(slot_ref, nused_ref, h_hbm, o_ref, buf, sem):
    m = pl.program_id(0)
    cur = m % 2

    def issue(blk, slot):
        def row(r, _):
            tok = slot_ref[blk * TMM + r]
            pltpu.make_async_copy(h_hbm.at[pl.ds(tok, 1), :], buf.at[slot, pl.ds(r, 1), :], sem.at[slot]).start()
            return 0

        lax.fori_loop(0, TMM, row, 0, unroll=8)

    @pl.when((m == 0) & (m < nused_ref[0]))
    def _():
        issue(m, 0)

    @pl.when(m + 1 < nused_ref[0])
    def _():
        issue(m + 1, 1 - cur)

    @pl.when(m < nused_ref[0])
    def _():
        pltpu.make_async_copy(h_hbm.at[pl.ds(0, TMM), :], buf.at[cur], sem.at[cur]).wait()
        o_ref[...] = buf[cur].astype(BF16)

    @pl.when(m >= nused_ref[0])
    def _():
        o_ref[...] = jnp.zeros(o_ref.shape, BF16)


def _gather(slot_tok, nused, h2):
    return pl.pallas_call(
        _gather_kernel,
        grid_spec=pltpu.PrefetchScalarGridSpec(
            num_scalar_prefetch=2,
            grid=(N_BLK,),
            in_specs=[pl.BlockSpec(memory_space=pl.ANY)],
            out_specs=pl.BlockSpec((TMM, D), lambda m, st, nu: (m, 0)),
            scratch_shapes=[pltpu.VMEM((2, TMM, D), F32), pltpu.SemaphoreType.DMA((2,))]),
        out_shape=jax.ShapeDtypeStruct((NSLOT, D), BF16),
        compiler_params=_cparams(("arbitrary",)),
        name="moe_gather",
    )(slot_tok, nused, h2)


def _new_expert(blk_e_ref, m):
    return (m == 0) | (blk_e_ref[m] != blk_e_ref[jnp.maximum(m - 1, 0)])


def _gm1_kernel(blk_e_ref, nused_ref, x_ref, w_ref, b_ref, sel_ref, o_ref, wb_scr):
    m = pl.program_id(1)
    used = m < nused_ref[0]

    @pl.when(used & _new_expert(blk_e_ref, m))
    def _():
        wb_scr[...] = w_ref[...].astype(BF16)

    @pl.when(used)
    def _():
        hd = jnp.dot(x_ref[...], wb_scr[...], preferred_element_type=F32) + b_ref[...]
        glu = jnp.minimum(hd, SWIGLU_LIMIT)
        lin = jnp.clip(hd, -SWIGLU_LIMIT, SWIGLU_LIMIT) + 1.0
        act = glu * _sigmoid(SWIGLU_ALPHA * glu)
        for c in range(TN_G1 // 512):
            parts = []
            for q in range(4):
                sl = slice(c * 512 + q * 128, c * 512 + (q + 1) * 128)
                parts.append(act[:, sl] * pltpu.roll(lin[:, sl], 127, axis=1))
            z = jnp.concatenate(parts, axis=1).astype(BF16)
            o_ref[:, c * 256:(c + 1) * 256] = jnp.dot(z, sel_ref[...], preferred_element_type=F32).astype(BF16)

    @pl.when(jnp.logical_not(used))
    def _():
        o_ref[...] = jnp.zeros(o_ref.shape, BF16)


def _gm1(blk_e, nused, xs, w1, b1, sel):
    def meff(m, nu):
        return jnp.minimum(m, nu[0] - 1)

    return pl.pallas_call(
        _gm1_kernel,
        grid_spec=pltpu.PrefetchScalarGridSpec(
            num_scalar_prefetch=2,
            grid=(2 * D_FF // TN_G1, N_BLK),
            in_specs=[pl.BlockSpec((TMM, D), lambda f, m, be, nu: (meff(m, nu), 0)),
                      pl.BlockSpec((None, D, TN_G1), lambda f, m, be, nu: (be[meff(m, nu)], 0, f)),
                      pl.BlockSpec((None, 1, TN_G1), lambda f, m, be, nu: (be[meff(m, nu)], 0, f)),
                      pl.BlockSpec((512, 256), lambda f, m, be, nu: (0, 0))],
            out_specs=pl.BlockSpec((TMM, TN_G1 // 2), lambda f, m, be, nu: (m, f)),
            scratch_shapes=[pltpu.VMEM((D, TN_G1), BF16)]),
        out_shape=jax.ShapeDtypeStruct((NSLOT, D_FF), BF16),
        compiler_params=_cparams(("arbitrary", "arbitrary")),
        name="moe_up",
    )(blk_e, nused, xs, w1, b1, sel)


def _gm2_kernel(blk_e_ref, nused_ref, x_ref, w_ref, b_ref, o_ref, wb_scr):
    m = pl.program_id(1)
    used = m < nused_ref[0]

    @pl.when(used & _new_expert(blk_e_ref, m))
    def _():
        wb_scr[...] = w_ref[...].astype(BF16)

    @pl.when(used)
    def _():
        o_ref[...] = jnp.dot(x_ref[...], wb_scr[...], preferred_element_type=F32) + b_ref[...]

    @pl.when(jnp.logical_not(used))
    def _():
        o_ref[...] = jnp.zeros(o_ref.shape, F32)


def _gm2(blk_e, nused, hid, w2, b2):
    def meff(m, nu):
        return jnp.minimum(m, nu[0] - 1)

    return pl.pallas_call(
        _gm2_kernel,
        grid_spec=pltpu.PrefetchScalarGridSpec(
            num_scalar_prefetch=2,
            grid=(D // TN_G2, N_BLK),
            in_specs=[pl.BlockSpec((TMM, D_FF), lambda f, m, be, nu: (meff(m, nu), 0)),
                      pl.BlockSpec((None, D_FF, TN_G2), lambda f, m, be, nu: (be[meff(m, nu)], 0, f)),
                      pl.BlockSpec((None, 1, TN_G2), lambda f, m, be, nu: (be[meff(m, nu)], 0, f))],
            out_specs=pl.BlockSpec((TMM, TN_G2), lambda f, m, be, nu: (m, f)),
            scratch_shapes=[pltpu.VMEM((D_FF, TN_G2), BF16)]),
        out_shape=jax.ShapeDtypeStruct((NSLOT, D), F32),
        compiler_params=_cparams(("arbitrary", "arbitrary")),
        name="moe_down",
    )(blk_e, nused, hid, w2, b2)


def _combine_kernel(dest_ref, x1_ref, tg_ref, fg_ref, y_hbm, o_ref, buf, sem):
    m = pl.program_id(0)
    cur = m % 2

    def issue(blk, slot):
        def row(r, _):
            for k in range(TOP_K):
                d = dest_ref[(blk * TMC + r) * TOP_K + k]
                pltpu.make_async_copy(y_hbm.at[pl.ds(d, 1), :], buf.at[slot, pl.ds(k * TMC + r, 1), :],
                                      sem.at[slot]).start()
            return 0

        lax.fori_loop(0, TMC, row, 0, unroll=4)

    @pl.when(m == 0)
    def _():
        issue(m, 0)

    @pl.when(m + 1 < pl.num_programs(0))
    def _():
        issue(m + 1, 1 - cur)

    pltpu.make_async_copy(y_hbm.at[pl.ds(0, TOP_K * TMC), :], buf.at[cur], sem.at[cur]).wait()
    x2 = x1_ref[...]
    tg = tg_ref[...]
    for k in range(TOP_K):
        x2 = x2 + tg[:, k:k + 1] * buf[cur, k * TMC:(k + 1) * TMC, :]
    ms = jnp.mean(x2 * x2, axis=-1, keepdims=True)
    o_ref[...] = (x2 * lax.rsqrt(ms + RMS_EPS)) * fg_ref[...]


def _combine(dest, x1, tg, fg, y):
    return pl.pallas_call(
        _combine_kernel,
        grid_spec=pltpu.PrefetchScalarGridSpec(
            num_scalar_prefetch=1,
            grid=(ROWS // TMC,),
            in_specs=[pl.BlockSpec((TMC, D), lambda m, d: (m, 0)),
                      pl.BlockSpec((TMC, 128), lambda m, d: (m, 0)),
                      pl.BlockSpec((1, D), lambda m, d: (0, 0)),
                      pl.BlockSpec(memory_space=pl.ANY)],
            out_specs=pl.BlockSpec((TMC, D), lambda m, d: (m, 0)),
            scratch_shapes=[pltpu.VMEM((2, TOP_K * TMC, D), F32), pltpu.SemaphoreType.DMA((2,))]),
        out_shape=jax.ShapeDtypeStruct((ROWS, D), F32),
        compiler_params=_cparams(("arbitrary",)),
        name="moe_combine",
    )(dest, x1, tg, fg, y)


def _t5_bucket(rel):
    nb = N_BUCKETS // 2
    ret = jnp.where(rel > 0, nb, 0)
    n = jnp.abs(rel)
    max_exact = nb // 2
    nf = jnp.maximum(n, 1).astype(jnp.float32)
    large = max_exact + (jnp.log(nf / max_exact) / math.log(MAX_DISTANCE / max_exact) * (nb - max_exact)).astype(jnp.int32)
    large = jnp.minimum(large, nb - 1)
    return ret + jnp.where(n < max_exact, n, large)


def _valid_rows():
    rows = [np.arange(b * T_PAD + FRONT, (b + 1) * T_PAD) for b in range(NB)]
    rows.append(np.arange(ROWS_P, ROWS))
    return np.concatenate(rows).astype(np.int32)


def _slot_tables(te, rank, cnt):
    counts = cnt[0, :N_EXPERTS].astype(jnp.int32)
    padded = (counts + TMM - 1) // TMM * TMM
    pad_end = jnp.cumsum(padded)
    pad_start = pad_end - padded
    experts = jnp.arange(N_EXPERTS, dtype=jnp.int32)
    routed = jnp.asarray(np.isin(np.arange(ROWS), _valid_rows()))[:, None]
    start_of = jnp.sum(jnp.where(te[:, :TOP_K, None] == experts, pad_start, 0), axis=-1)
    dest2 = start_of + rank[:, :TOP_K]
    dest = jnp.where(routed, dest2, 0).reshape(-1)
    rows = jnp.broadcast_to(jnp.arange(ROWS, dtype=jnp.int32)[:, None], (ROWS, TOP_K))
    slot_tok = jnp.zeros((NSLOT,), jnp.int32).at[jnp.where(routed, dest2, NSLOT).reshape(-1)].set(
        rows.reshape(-1), mode='drop')
    blk_start = jnp.arange(N_BLK, dtype=jnp.int32) * TMM
    blk_e = jnp.minimum(jnp.sum(pad_end[None, :] <= blk_start[:, None], axis=-1), N_EXPERTS - 1).astype(jnp.int32)
    nused = (pad_end[-1:] // TMM).astype(jnp.int32)
    return dest, slot_tok, blk_e, nused


def kernel(x_prompt, x_sample, cache_k, cache_v, state_wkv, state_shift, meta_tokens, rel_bias_table, norm1_g, w_in, shift_mu, decay_base, w_decay_up, a_base, w_a_up, w_g_up, k_k, k_a, r_k, gn_w, gn_b, lam_q1, lam_k1, lam_q2, lam_k2, subln_g, w_out, norm2_g, w_router, b_router, w_e1, b_e1, w_e2, b_e2, final_g):
    xp = jnp.concatenate([jnp.zeros((NB, FRONT, D), F32),
                          jnp.broadcast_to(meta_tokens[None], (NB, N_META, D)), x_prompt], axis=1)
    x_all = jnp.concatenate([xp.reshape(ROWS_P, D), x_sample.reshape(ROWS_S, D)], axis=0)
    lora0 = OFF_R + 3 * D
    w_tail = jnp.concatenate([w_in[0][:, lora0 + LORA_W:], w_in[0][:, lora0:lora0 + LORA_W],
                              jnp.zeros((D, LORA_P - LORA_W), F32)], axis=1).astype(BF16)

    p_all = _inproj(x_all, norm1_g, w_in[0].astype(BF16), w_tail)

    kq = jnp.arange(TQ, dtype=jnp.int32)
    bkt = jnp.stack([_t5_bucket(kq[:, None] - kq[None, :]), _t5_bucket(kq[:, None] - kq[None, :] - TQ)])
    lamp = jnp.concatenate([lam_q1, lam_k1, lam_q2, lam_k2], axis=0)
    oa_p = _attn_prompt(p_all, rel_bias_table, bkt, lamp, subln_g)
    qpos = PAST + jnp.arange(DEC_S, dtype=jnp.int32)
    bkt_last = _t5_bucket((PAST - NEAR + jnp.arange(NEAR, dtype=jnp.int32))[None, :] - qpos[:, None])
    bkt_new = _t5_bucket(qpos[None, :] - qpos[:, None])
    oa_s = _attn_sample(p_all, cache_k[0], cache_v[0],
                        rel_bias_table, bkt_last, bkt_new, lamp, subln_g)

    mu = shift_mu[0]
    mu3 = mu[:3 * D][None]
    mul = jnp.pad(mu[3 * D:], (0, LORA_P - LORA_W))[None]
    wd = jnp.zeros((LORA_P, D), F32).at[:DECAY_LORA].set(w_decay_up[0]).astype(BF16)
    wa = jnp.zeros((LORA_P, D), F32).at[DECAY_LORA:DECAY_LORA + AAA_LORA].set(w_a_up[0]).astype(BF16)
    wg = jnp.zeros((LORA_P, D), F32).at[DECAY_LORA + AAA_LORA:LORA_W].set(w_g_up[0]).astype(BF16)
    first_p = jnp.zeros((NB, 1, 3 * D + LORA_P), F32)
    first_s = jnp.pad(state_shift[0], ((0, 0), (0, LORA_P - LORA_W)))[:, None, :]
    prep_p = _rwkv_prep(p_all, first_p, mu3, mul, decay_base, a_base, wd, wa, wg, NB, T_PAD // 256, 256, 0, True)
    prep_s = _rwkv_prep(p_all, first_s, mu3, mul, decay_base, a_base, wd, wa, wg, DEC_B, 1, DEC_S, ROWS_P // DEC_S,
                        False)

    def chain_tile(v):
        return jnp.tile(v.reshape(B_HEADS, B_HEAD).T, (1, 128 // B_HEADS))

    par = [chain_tile(v.reshape(-1)) for v in (k_k[0], k_a[0], r_k[0], gn_w[0], gn_b[0])]
    s0_p = jnp.zeros((B_HEAD, B_HEAD, NB * B_HEADS), F32)
    s0_s = state_wkv[0].transpose(3, 2, 0, 1).reshape(B_HEAD, B_HEAD, DEC_B * B_HEADS)
    yo_p, sfin_p = _rwkv_rec(*[a.transpose(2, 1, 0) for a in prep_p[:5]], *par, s0_p, skip=FRONT // TB)
    yo_s, sfin_s = _rwkv_rec(*[_to_chains(a, DEC_B, DEC_S) for a in prep_s[:5]], *par, s0_s)

    wo = w_out[0].astype(BF16)
    wr = jnp.pad(w_router[0], ((0, 0), (0, 128 - N_EXPERTS)))
    br = jnp.concatenate([b_router[0], jnp.full((128 - N_EXPERTS,), NEG, F32)])[None]
    x1, h2, te, tg, rank, cnt = _mix(yo_p.transpose(2, 1, 0), _from_chains(yo_s, DEC_B, DEC_S), prep_p[5], prep_s[5],
                                     oa_p, oa_s, p_all, x_all, wo, norm2_g, wr, br)

    dest, slot_tok, blk_e, nused = _slot_tables(te, rank, cnt)

    sel_np = np.zeros((512, 256), np.float32)
    sel_np[2 * np.arange(256), np.arange(256)] = 1.0
    xs = _gather(slot_tok, nused, h2)
    hid = _gm1(blk_e, nused, xs, w_e1[0], b_e1[0][:, None, :], jnp.asarray(sel_np, BF16))
    ys = _gm2(blk_e, nused, hid, w_e2[0], b_e2[0][:, None, :])
    y_all = _combine(dest, x1, tg, final_g[None], ys)

    y_prompt = y_all[:ROWS_P].reshape(NB, T_PAD, D)[:, FRONT + N_META:]
    y_sample = y_all[ROWS_P:].reshape(DEC_B, DEC_S, D)
    def prompt_rows(off):
        sec = p_all[:ROWS_P, off:off + D].reshape(NB, T_PAD, D)[:, FRONT:]
        return sec.reshape(1, NB, N_META + SEQ, A_HEADS, A_VD)

    def sample_rows(off):
        return p_all[ROWS_P:, off:off + D].reshape(1, DEC_B, DEC_S, A_HEADS, A_VD)

    def last_ps(last):
        return last[:, 0, :SHIFT_W][None]

    k_prompt, v_prompt = prompt_rows(OFF_K), prompt_rows(OFF_V)
    k_sample, v_sample = sample_rows(OFF_K), sample_rows(OFF_V)
    shift_prompt = last_ps(prep_p[6])
    shift_sample = last_ps(prep_s[6])
    wkv_prompt = sfin_p.reshape(B_HEAD, B_HEAD, NB, B_HEADS).transpose(2, 3, 1, 0)[None]
    wkv_sample = sfin_s.reshape(B_HEAD, B_HEAD, DEC_B, B_HEADS).transpose(2, 3, 1, 0)[None].astype(state_wkv.dtype)
    return (y_prompt, y_sample, k_prompt, v_prompt, wkv_prompt, shift_prompt,
            k_sample, v_sample, wkv_sample, shift_sample)
```

```python
import functools
import math

import numpy as np
import jax
import jax.numpy as jnp
from jax import lax
from jax.experimental import pallas as pl
from jax.experimental.pallas import tpu as pltpu

F32 = jnp.float32
BF16 = jnp.bfloat16

D = 2048
NB = 4
SEQ = 2048
DEC_B = 16
DEC_S = 32
PAST = 2048
CHUNK = 64
N_META = 16
RMS_EPS = 1e-5
A_DH = 64
A_HEADS = 16
A_VD = 128
B_HEAD = 64
B_HEADS = 32
DECAY_LORA = 96
AAA_LORA = 96
GATE_LORA = 256
GN_EPS = 64e-5
N_BUCKETS = 32
MAX_DISTANCE = 128
N_EXPERTS = 32
TOP_K = 4
D_FF = 2048
SWIGLU_ALPHA = 1.702
SWIGLU_LIMIT = 7.0
LAM_INIT = 0.8 - 0.6 * math.exp(-0.3 * 0)
SHIFT_W = 3 * D + DECAY_LORA + AAA_LORA + GATE_LORA
LORA_W = DECAY_LORA + AAA_LORA + GATE_LORA
LORA_P = 512

T_PAD = 2304
FRONT = T_PAD - N_META - SEQ
ROWS_P = NB * T_PAD
ROWS_S = DEC_B * DEC_S
ROWS = ROWS_P + ROWS_S

OFF_Q = 0
OFF_K = 2048
OFF_V = 4096
OFF_R = 6144
OFF_KX = 8192
OFF_VX = 10240
OFF_GA = 12288
OFF_GB = 14336
OFF_LORA = 16384
PW = OFF_LORA + LORA_P

NEG = -1e30
LOG2E = 1.4426950408889634
VMEM_LIMIT = 56 * 1024 * 1024


def _cparams(sem, vmem=VMEM_LIMIT):
    return pltpu.CompilerParams(dimension_semantics=sem, vmem_limit_bytes=vmem)


def _sigmoid(x):
    return 1.0 / (1.0 + jnp.exp(-x))


TM1 = 512
TN1 = 1536
N_MAIN = OFF_GA // TN1


def _inproj_kernel(x_ref, g_ref, wm_ref, wt_ref, o_ref, h_scr):
    j = pl.program_id(1)

    @pl.when(j == 0)
    def _():
        x = x_ref[...]
        ms = jnp.mean(x * x, axis=-1, keepdims=True)
        h_scr[...] = ((x * lax.rsqrt(ms + RMS_EPS)) * g_ref[...]).astype(BF16)

    @pl.when(j < N_MAIN)
    def _():
        o_ref[...] = jnp.dot(h_scr[...], wm_ref[...], preferred_element_type=F32)

    @pl.when(j >= N_MAIN)
    def _():
        o_ref[...] = jnp.dot(h_scr[...], wt_ref[...], preferred_element_type=F32)


def _inproj(x_all, g, w_main, w_tail):
    return pl.pallas_call(
        _inproj_kernel,
        grid=(ROWS // TM1, PW // TN1),
        in_specs=[pl.BlockSpec((TM1, D), lambda i, j: (i, 0)),
                  pl.BlockSpec((1, D), lambda i, j: (0, 0)),
                  pl.BlockSpec((D, TN1), lambda i, j: (0, jnp.minimum(j, N_MAIN - 1))),
                  pl.BlockSpec((D, TN1), lambda i, j: (0, jnp.maximum(j - N_MAIN, 0)))],
        out_specs=pl.BlockSpec((TM1, TN1), lambda i, j: (i, j)),
        out_shape=jax.ShapeDtypeStruct((ROWS, PW), F32),
        scratch_shapes=[pltpu.VMEM((TM1, D), BF16)],
        compiler_params=_cparams(("arbitrary", "arbitrary")),
        name="inproj",
    )(x_all, g, w_main, w_tail)


def _lam_value(lamp_ref):
    lp = lamp_ref[...]
    s1 = jnp.sum(lp[0:1, :] * lp[1:2, :], axis=-1, keepdims=True)
    s2 = jnp.sum(lp[2:3, :] * lp[3:4, :], axis=-1, keepdims=True)
    return jnp.exp(s1) - jnp.exp(s2) + LAM_INIT


def _bias_from_buckets(bkt, table_ref, h):
    out = jnp.zeros(bkt.shape, F32)
    for n in range(N_BUCKETS):
        out = jnp.where(bkt == n, table_ref[n, h], out)
    return out


TQ = 256
NQB = T_PAD // TQ
BIAS_META, BIAS_DIAG, BIAS_SUB, BIAS_SUB_PAD, BIAS_FAR_PAD, BIAS_FAR = range(6)


def _attn_prompt_kernel(table_ref, q_ref, k_ref, v_ref, bkt_ref, lamp_ref, subg_ref, o_ref,
                        kb_scr, vt_scr, bias_scr, s_scr, acc_scr):
    h = pl.program_id(0)
    b = pl.program_id(1)

    @pl.when(b == 0)
    def _():
        kk = lax.broadcasted_iota(jnp.int32, (TQ, TQ), 0)
        qq = lax.broadcasted_iota(jnp.int32, (TQ, TQ), 1)
        chunk_ok = (kk // CHUNK) <= (qq // CHUNK)
        pad = jnp.where(kk < FRONT, NEG, 0.0).astype(F32)
        b_diag = _bias_from_buckets(bkt_ref[0], table_ref, h)
        b_sub = _bias_from_buckets(bkt_ref[1], table_ref, h)
        b_far = jnp.zeros((TQ, TQ), F32) + table_ref[N_BUCKETS // 2 - 1, h]
        tiles = (jnp.where(chunk_ok, b_diag, NEG) + pad,
                 jnp.where(chunk_ok, b_diag, NEG),
                 b_sub,
                 b_sub + pad,
                 b_far + pad,
                 b_far)
        for i, t in enumerate(tiles):
            bias_scr[i] = jnp.concatenate([t, t], axis=1) * LOG2E

    kb_scr[...] = k_ref[...].astype(BF16)
    for i in range(NQB):
        vt_scr[i] = v_ref[i * TQ:(i + 1) * TQ, :].T.astype(BF16)

    lam = _lam_value(lamp_ref)
    drow = lax.broadcasted_iota(jnp.int32, (A_VD, TQ), 0)

    def q_operand(qi):
        q0 = pl.multiple_of(qi * TQ, TQ)
        qt = (q_ref[pl.ds(q0, TQ), :] * (A_DH ** -0.5 * LOG2E)).T
        q1 = jnp.where(drow < A_DH, qt, 0.0)
        q2 = jnp.where(drow >= A_DH, qt, 0.0)
        return jnp.concatenate([q1, q2], axis=1).astype(BF16)

    def scores(kj, qpt, slot):
        k0 = pl.multiple_of(kj * TQ, TQ)
        s_scr[slot] = jnp.dot(kb_scr[pl.ds(k0, TQ), :], qpt, preferred_element_type=F32)

    def accumulate(kj, bias_id, slot, ml):
        m, l = ml
        s = s_scr[slot] + bias_scr[bias_id]
        m_new = jnp.maximum(m, jnp.max(s, axis=0, keepdims=True))
        alpha = jnp.exp2(m - m_new)
        p = jnp.exp2(s - m_new)
        l = l * alpha + jnp.sum(p, axis=0, keepdims=True)
        acc_scr[...] = acc_scr[...] * alpha + jnp.dot(vt_scr[kj], p.astype(BF16), preferred_element_type=F32)
        return m_new, l

    def finish(qi, ml):
        on = acc_scr[...] / ml[1]
        ot = on[:, :TQ] - lam * on[:, TQ:]
        ms = jnp.mean(ot * ot, axis=0, keepdims=True)
        o = (ot * lax.rsqrt(ms + RMS_EPS)).T * subg_ref[...] * (1.0 - LAM_INIT)
        q0 = pl.multiple_of(qi * TQ, TQ)
        o_ref[pl.ds(q0, TQ), :] = o

    def start(qi):
        qpt = q_operand(qi)
        acc_scr[...] = jnp.zeros(acc_scr.shape, F32)
        scores(0, qpt, 0)
        return qpt, (jnp.full((1, 2 * TQ), NEG, F32), jnp.zeros((1, 2 * TQ), F32))

    finish(0, accumulate(0, BIAS_META, 0, start(0)[1]))

    def q_block(qi, _):
        qpt, ml = start(qi)

        def half(kj, cur, ml):
            scores(kj + 1, qpt, 1 - cur)
            first = kj == 0
            bias_id = jnp.where(kj == qi - 1, jnp.where(first, BIAS_SUB_PAD, BIAS_SUB),
                                jnp.where(first, BIAS_FAR_PAD, BIAS_FAR))
            return accumulate(kj, bias_id, cur, ml)

        ml = lax.fori_loop(0, qi // 2, lambda i, ml: half(2 * i + 1, 1, half(2 * i, 0, ml)), ml)
        ml = lax.cond(qi % 2 == 1,
                      lambda ml: accumulate(qi, BIAS_DIAG, 1, half(qi - 1, 0, ml)),
                      lambda ml: accumulate(qi, BIAS_DIAG, 0, ml), ml)
        finish(qi, ml)
        return 0

    lax.fori_loop(1, NQB, q_block, 0)


def _attn_prompt(p_all, table, bkt, lamp, subg):
    cb = D // A_VD
    return pl.pallas_call(
        _attn_prompt_kernel,
        grid_spec=pltpu.PrefetchScalarGridSpec(
            num_scalar_prefetch=0,
            grid=(A_HEADS, NB),
            in_specs=[pl.BlockSpec(memory_space=pltpu.SMEM),
                      pl.BlockSpec((T_PAD, A_VD), lambda h, b: (b, h)),
                      pl.BlockSpec((T_PAD, A_VD), lambda h, b: (b, cb + h)),
                      pl.BlockSpec((T_PAD, A_VD), lambda h, b: (b, 2 * cb + h)),
                      pl.BlockSpec((2, TQ, TQ), lambda h, b: (0, 0, 0)),
                      pl.BlockSpec((4, A_DH), lambda h, b: (0, 0)),
                      pl.BlockSpec((1, A_VD), lambda h, b: (0, 0))],
            out_specs=pl.BlockSpec((T_PAD, A_VD), lambda h, b: (b, h)),
            scratch_shapes=[pltpu.VMEM((T_PAD, A_VD), BF16),
                            pltpu.VMEM((NQB, A_VD, TQ), BF16),
                            pltpu.VMEM((6, TQ, 2 * TQ), F32),
                            pltpu.VMEM((2, TQ, 2 * TQ), F32),
                            pltpu.VMEM((A_VD, 2 * TQ), F32)]),
        out_shape=jax.ShapeDtypeStruct((ROWS_P, D), F32),
        compiler_params=_cparams(("arbitrary", "arbitrary")),
        name="attn_prompt",
    )(table, p_all, p_all, p_all, bkt, lamp, subg)


HG = 4
NEAR = 512


def _attn_sample_kernel(table_ref, q_ref, kn_ref, vn_ref, bl_ref, bn_ref, lamp_ref, subg_ref, ck_hbm, cv_hbm,
                        o_ref, kbuf, vbuf, sem):
    hg = pl.program_id(1)
    ng = pl.num_programs(1)
    step = pl.program_id(0) * ng + hg
    nsteps = pl.num_programs(0) * ng

    def copies(n, slot):
        out = []
        for i in range(HG):
            h = (n % ng) * HG + i
            out.append(pltpu.make_async_copy(ck_hbm.at[n // ng, :, h, :], kbuf.at[slot, i], sem.at[slot, i]))
            out.append(pltpu.make_async_copy(cv_hbm.at[n // ng, :, h, :], vbuf.at[slot, i], sem.at[slot, HG + i]))
        return out

    cur = step % 2

    @pl.when(step == 0)
    def _():
        for c in copies(step, 0):
            c.start()

    @pl.when(step + 1 < nsteps)
    def _():
        for c in copies(step + 1, 1 - cur):
            c.start()

    for c in copies(step, cur):
        c.wait()
    ck_refs = [kbuf.at[cur, i] for i in range(HG)]
    cv_refs = [vbuf.at[cur, i] for i in range(HG)]
    lam = _lam_value(lamp_ref)
    lane = lax.broadcasted_iota(jnp.int32, (DEC_S, A_VD), 1)
    nt = (((1,), (1,)), ((), ()))
    for i in range(HG):
        h = hg * HG + i
        cols = slice(i * A_VD, (i + 1) * A_VD)
        q = q_ref[:, cols] * (A_DH ** -0.5)
        qp = jnp.concatenate([jnp.where(lane < A_DH, q, 0.0), jnp.where(lane >= A_DH, q, 0.0)],
                             axis=0).astype(BF16)
        far = jnp.zeros((DEC_S, PAST - NEAR), F32) + table_ref[N_BUCKETS // 2 - 1, h]
        bl = jnp.concatenate([far, _bias_from_buckets(bl_ref[...], table_ref, h)], axis=1)
        bn = _bias_from_buckets(bn_ref[...], table_ref, h)
        s = (lax.dot_general(qp, ck_refs[i][...].astype(BF16), nt, preferred_element_type=F32)
             + jnp.concatenate([bl, bl], axis=0))
        sn = (lax.dot_general(qp, kn_ref[:, cols].astype(BF16), nt, preferred_element_type=F32)
              + jnp.concatenate([bn, bn], axis=0))
        m = jnp.maximum(jnp.max(s, axis=-1, keepdims=True), jnp.max(sn, axis=-1, keepdims=True))
        p = jnp.exp(s - m)
        pn = jnp.exp(sn - m)
        l = jnp.sum(p, axis=-1, keepdims=True) + jnp.sum(pn, axis=-1, keepdims=True)
        acc = (jnp.dot(p.astype(BF16), cv_refs[i][...].astype(BF16), preferred_element_type=F32)
               + jnp.dot(pn.astype(BF16), vn_ref[:, cols].astype(BF16), preferred_element_type=F32))
        on = acc / l
        o = on[:DEC_S] - lam * on[DEC_S:]
        ms = jnp.mean(o * o, axis=-1, keepdims=True)
        o_ref[:, cols] = (o * lax.rsqrt(ms + RMS_EPS)) * subg_ref[...] * (1.0 - LAM_INIT)


def _attn_sample(p_all, cache_k, cache_v, table, bkt_last, bkt_new, lamp, subg):
    r0 = ROWS_P // DEC_S
    w = HG * A_VD
    new = lambda off: pl.BlockSpec((DEC_S, w), lambda b, g: (r0 + b, off // w + g))
    return pl.pallas_call(
        _attn_sample_kernel,
        grid=(DEC_B, A_HEADS // HG),
        in_specs=[pl.BlockSpec(memory_space=pltpu.SMEM), new(OFF_Q), new(OFF_K), new(OFF_V),
                  pl.BlockSpec((DEC_S, NEAR), lambda b, g: (0, 0)),
                  pl.BlockSpec((DEC_S, DEC_S), lambda b, g: (0, 0)),
                  pl.BlockSpec((4, A_DH), lambda b, g: (0, 0)),
                  pl.BlockSpec((1, A_VD), lambda b, g: (0, 0)),
                  pl.BlockSpec(memory_space=pl.ANY), pl.BlockSpec(memory_space=pl.ANY)],
        out_specs=pl.BlockSpec((DEC_S, w), lambda b, g: (b, g)),
        out_shape=jax.ShapeDtypeStruct((ROWS_S, D), F32),
        scratch_shapes=[pltpu.VMEM((2, HG, PAST, A_VD), F32), pltpu.VMEM((2, HG, PAST, A_VD), F32),
                        pltpu.SemaphoreType.DMA((2, 2 * HG))],
        compiler_params=_cparams(("arbitrary", "arbitrary")),
        name="attn_sample",
    )(table, p_all, p_all, p_all, bkt_last, bkt_new, lamp, subg, cache_k, cache_v)


def _rwkv_prep_kernel(chain_out, r_ref, k_ref, v_ref, lo_ref, first_ref, mu_ref, mul_ref, dbase_ref, abase_ref,
                      wd_ref, wa_ref, wg_ref,
                      xr_ref, xk_ref, xv_ref, wp_ref, ap_ref, g_ref, last_ref, carry_scr, carryl_scr):
    i = pl.program_id(1)
    tm = r_ref.shape[0]
    row = lax.broadcasted_iota(jnp.int32, (tm, 1), 0)

    def shift(x, first, mu):
        prev = jnp.where(row == 0, first, pltpu.roll(x, 1, axis=0))
        return x + (prev - x) * mu

    def put(dst, val):
        if chain_out:
            for hp in range(B_HEADS // 2):
                t = val[:, hp * 128:(hp + 1) * 128].T
                dst[2 * hp] = t[:B_HEAD]
                dst[2 * hp + 1] = t[B_HEAD:]
        else:
            dst[...] = val

    for s, (src, dst) in enumerate(((r_ref, xr_ref), (k_ref, xk_ref), (v_ref, xv_ref))):
        x = src[...]
        first = jnp.where(i == 0, first_ref[:, s * D:(s + 1) * D], carry_scr[:, s * D:(s + 1) * D])
        put(dst, shift(x, first, mu_ref[:, s * D:(s + 1) * D]))
        carry_scr[:, s * D:(s + 1) * D] = x[tm - 1:tm, :]
        last_ref[:, s * D:(s + 1) * D] = x[tm - 1:tm, :]

    xl = lo_ref[...]
    first = jnp.where(i == 0, first_ref[:, 3 * D:], carryl_scr[...])
    xs = shift(xl, first, mul_ref[...])
    carryl_scr[...] = xl[tm - 1:tm, :]
    last_ref[:, 3 * D:] = xl[tm - 1:tm, :]
    put(wp_ref, dbase_ref[...] + jnp.dot(jnp.tanh(xs).astype(BF16), wd_ref[...], preferred_element_type=F32))
    put(ap_ref, abase_ref[...] + jnp.dot(xs.astype(BF16), wa_ref[...], preferred_element_type=F32))
    g_ref[...] = jnp.dot(_sigmoid(xs).astype(BF16), wg_ref[...], preferred_element_type=F32)


def _rwkv_prep(p_all, first, mu, mul, dbase, abase, wd, wa, wg, nseq, nblk, tm, row_blk0, chain_out):
    rows = nseq * nblk * tm
    cs = lambda off: (lambda s, i: (row_blk0 + s * nblk + i, off))
    full = lambda shp: pl.BlockSpec(shp, lambda s, i: tuple(0 for _ in shp))
    out = jax.ShapeDtypeStruct((rows, D), F32)
    ospec = pl.BlockSpec((tm, D), lambda s, i: (s * nblk + i, 0))
    if chain_out:
        cout = jax.ShapeDtypeStruct((nseq * B_HEADS, B_HEAD, nblk * tm), F32)
        cspec = pl.BlockSpec((B_HEADS, B_HEAD, tm), lambda s, i: (s, 0, i))
    else:
        cout, cspec = out, ospec
    return pl.pallas_call(
        functools.partial(_rwkv_prep_kernel, chain_out),
        grid=(nseq, nblk),
        in_specs=[pl.BlockSpec((tm, D), cs(OFF_R // D)),
                  pl.BlockSpec((tm, D), cs(OFF_KX // D)),
                  pl.BlockSpec((tm, D), cs(OFF_VX // D)),
                  pl.BlockSpec((tm, LORA_P), cs(OFF_LORA // LORA_P)),
                  pl.BlockSpec((None, 1, 3 * D + LORA_P), lambda s, i: (s, 0, 0)),
                  full((1, 3 * D)), full((1, LORA_P)), full((1, D)), full((1, D)),
                  full((LORA_P, D)), full((LORA_P, D)), full((LORA_P, D))],
        out_specs=[cspec] * 5 + [ospec, pl.BlockSpec((None, 1, 3 * D + LORA_P), lambda s, i: (s, 0, 0))],
        out_shape=[cout] * 5 + [out, jax.ShapeDtypeStruct((nseq, 1, 3 * D + LORA_P), F32)],
        scratch_shapes=[pltpu.VMEM((1, 3 * D), F32), pltpu.VMEM((1, LORA_P), F32)],
        compiler_params=_cparams(("arbitrary", "arbitrary")),
        name="rwkv_prep",
    )(p_all, p_all, p_all, p_all, first, mu, mul, dbase, abase, wd, wa, wg)


TB = 32
IH = 32


def _rwkv_rec_kernel(skip, *refs):
    tb = pl.program_id(1)
    s0_ref, o_ref, sfin_ref, S = refs[10], refs[11], refs[12], refs[13]

    @pl.when(tb == 0)
    def _():
        S[...] = s0_ref[...]

    if skip:
        @pl.when(tb < skip)
        def _():
            o_ref[...] = jnp.zeros(o_ref.shape, F32)

        pl.when(tb >= skip)(functools.partial(_rwkv_rec_block, *refs))
    else:
        _rwkv_rec_block(*refs)
    sfin_ref[...] = S[...]


def _rwkv_rec_block(xr_ref, xk_ref, xv_ref, wp_ref, ap_ref, kk_ref, ka_ref, rk_ref, gnw_ref, gnb_ref, s0_ref,
                    o_ref, sfin_ref, S, w_s, k_s, a_s, b_s, sa_s, y_s):
    z = -wp_ref[...]
    softplus = jnp.maximum(z, 0.0) + jnp.log(1.0 + jnp.exp(-jnp.abs(z)))
    w_s[...] = jnp.exp(-jnp.exp(-softplus - 0.5))
    a = _sigmoid(ap_ref[...])
    xk = xk_ref[...]
    kk = xk * kk_ref[...]
    nrm = jnp.sqrt(jnp.sum(kk * kk, axis=1, keepdims=True))
    kk = kk / jnp.maximum(nrm, 1e-12)
    k_s[...] = xk * (1.0 + (a - 1.0) * ka_ref[...])
    a_s[...] = -kk
    b_s[...] = kk * a

    for hf in range(B_HEAD // IH):
        i0 = hf * IH
        acc = jnp.zeros((IH, 128), F32)
        for j in range(B_HEAD):
            acc = acc + S[j, i0:i0 + IH, :] * a_s[0, j:j + 1, :]
        sa_s[0, i0:i0 + IH, :] = acc

    def token(t, _):
        cur = t % 2
        tn = jnp.minimum(t + 1, TB - 1)
        for hf in range(B_HEAD // IH):
            i0 = hf * IH
            sa = sa_s[cur, i0:i0 + IH, :]
            v = xv_ref[t, i0:i0 + IH, :]
            y = jnp.zeros((IH, 128), F32)
            san = jnp.zeros((IH, 128), F32)
            for j in range(B_HEAD):
                s_new = (S[j, i0:i0 + IH, :] * w_s[t, j:j + 1, :] + sa * b_s[t, j:j + 1, :]
                         + v * k_s[t, j:j + 1, :])
                S[j, i0:i0 + IH, :] = s_new
                y = y + s_new * xr_ref[t, j:j + 1, :]
                san = san + s_new * a_s[tn, j:j + 1, :]
            y_s[i0:i0 + IH, :] = y
            sa_s[1 - cur, i0:i0 + IH, :] = san
        y = y_s[...]
        mu = jnp.mean(y, axis=0, keepdims=True)
        yc = y - mu
        var = jnp.mean(yc * yc, axis=0, keepdims=True)
        bonus = jnp.sum(xr_ref[t] * k_s[t] * rk_ref[...], axis=0, keepdims=True)
        o_ref[t] = yc * lax.rsqrt(var + GN_EPS) * gnw_ref[...] + gnb_ref[...] + bonus * xv_ref[t]
        return 0

    lax.fori_loop(0, TB, token, 0)


def _rwkv_rec(xr, xk, xv, wp, ap, kk, ka, rk, gnw, gnb, s0, skip=0):
    t, _, c = xr.shape
    tok = pl.BlockSpec((TB, B_HEAD, 128), lambda g, tb: (tb, 0, g))
    par = pl.BlockSpec((B_HEAD, 128), lambda g, tb: (0, 0))
    st = pl.BlockSpec((B_HEAD, B_HEAD, 128), lambda g, tb: (0, 0, g))
    blk = pltpu.VMEM((TB, B_HEAD, 128), F32)
    return pl.pallas_call(
        functools.partial(_rwkv_rec_kernel, skip),
        grid=(c // 128, t // TB),
        in_specs=[tok] * 5 + [par] * 5 + [st],
        out_specs=[tok, st],
        out_shape=[jax.ShapeDtypeStruct((t, B_HEAD, c), F32), jax.ShapeDtypeStruct((B_HEAD, B_HEAD, c), F32)],
        scratch_shapes=[pltpu.VMEM((B_HEAD, B_HEAD, 128), F32), blk, blk, blk, blk,
                        pltpu.VMEM((2, B_HEAD, 128), F32), pltpu.VMEM((B_HEAD, 128), F32)],
        compiler_params=_cparams(("arbitrary", "arbitrary")),
        name="rwkv_rec",
    )(xr, xk, xv, wp, ap, kk, ka, rk, gnw, gnb, s0)


def _to_chains(a, nb, t):
    return a.reshape(nb, t, B_HEADS, B_HEAD).transpose(1, 3, 0, 2).reshape(t, B_HEAD, nb * B_HEADS)


def _from_chains(a, nb, t):
    return a.reshape(t, B_HEAD, nb, B_HEADS).transpose(2, 0, 3, 1).reshape(nb * t, D)


TM5 = 256


def _mix_kernel(yo_p_ref, yo_s_ref, g_p_ref, g_s_ref, oa_p_ref, oa_s_ref, ga_ref, gb_ref, x_ref,
                wo_ref, n2_ref, wr_ref, br_ref, ltri_ref, x1_ref, h2_ref, te_ref, tg_ref, rank_ref, cnt_ref, cnt_scr):
    i = pl.program_id(0)
    is_p = i < ROWS_P // TM5
    tiles = [jnp.concatenate([yo_p_ref[2 * hp], yo_p_ref[2 * hp + 1]], axis=0).T for hp in range(B_HEADS // 2)]
    yo = jnp.where(is_p, jnp.concatenate(tiles, axis=1), yo_s_ref[...])
    g = jnp.where(is_p, g_p_ref[...], g_s_ref[...])
    oa = jnp.where(is_p, oa_p_ref[...], oa_s_ref[...])
    mixed = _sigmoid(ga_ref[...]) * oa + _sigmoid(gb_ref[...]) * (yo * g)
    x1 = x_ref[...] + jnp.dot(mixed.astype(BF16), wo_ref[...], preferred_element_type=F32)
    x1_ref[...] = x1
    ms = jnp.mean(x1 * x1, axis=-1, keepdims=True)
    h2 = (x1 * lax.rsqrt(ms + RMS_EPS)) * n2_ref[...]
    h2_ref[...] = h2
    logits = jnp.dot(h2, wr_ref[...], preferred_element_type=F32, precision=lax.Precision.HIGHEST) + br_ref[...]
    lane = lax.broadcasted_iota(jnp.int32, logits.shape, 1)
    lane_f = lane.astype(F32)
    te = jnp.zeros(logits.shape, jnp.int32)
    tv = jnp.full(logits.shape, NEG, F32)
    chosen = []
    for k in range(TOP_K):
        m = jnp.max(logits, axis=-1, keepdims=True)
        idx = jnp.min(jnp.where(logits == m, lane_f, 128.0), axis=-1, keepdims=True).astype(jnp.int32)
        te = jnp.where(lane == k, idx, te)
        tv = jnp.where(lane == k, m, tv)
        chosen.append(lane == idx)
        logits = jnp.where(chosen[k], NEG, logits)
    e = jnp.where(lane < TOP_K, jnp.exp(tv - jnp.max(tv, axis=-1, keepdims=True)), 0.0)
    te_ref[...] = te
    tg_ref[...] = e / jnp.sum(e, axis=-1, keepdims=True)

    @pl.when(i == 0)
    def _():
        cnt_scr[...] = jnp.zeros(cnt_scr.shape, F32)

    first_row = jnp.where(is_p, (i % (T_PAD // TM5)) * TM5, FRONT)
    routed = (lax.broadcasted_iota(jnp.int32, (TM5, 1), 0) + first_row) >= FRONT
    hits = jnp.zeros(logits.shape, F32)
    for k in range(TOP_K):
        hits = hits + jnp.where(chosen[k], 1.0, 0.0)
    hits = jnp.where(routed, hits, 0.0)
    before = cnt_scr[...] + jnp.dot(ltri_ref[...], hits.astype(BF16), preferred_element_type=F32)
    rank = jnp.zeros(logits.shape, jnp.int32)
    for k in range(TOP_K):
        rk = jnp.sum(jnp.where(chosen[k], before, 0.0), axis=-1, keepdims=True).astype(jnp.int32)
        rank = jnp.where(lane == k, rk, rank)
    rank_ref[...] = rank
    cnt_scr[...] = cnt_scr[...] + jnp.sum(hits, axis=0, keepdims=True)
    cnt_ref[...] = jnp.broadcast_to(cnt_scr[...], cnt_ref.shape)


def _mix(yo_p, yo_s, g_p, g_s, oa_p, oa_s, p_all, x_all, wo, n2, wr, br):
    nbp = ROWS_P // TM5
    rb = lambda i: (i, 0)
    pb = pl.BlockSpec((TM5, D), lambda i: (jnp.minimum(i, nbp - 1), 0))
    sb = pl.BlockSpec((TM5, D), lambda i: (jnp.maximum(i - nbp, 0), 0))
    full = lambda shp: pl.BlockSpec(shp, lambda i: tuple(0 for _ in shp))
    return pl.pallas_call(
        _mix_kernel,
        grid=(ROWS // TM5,),
        in_specs=[pl.BlockSpec((B_HEADS, B_HEAD, TM5),
                               lambda i: (jnp.minimum(i, nbp - 1) // (T_PAD // TM5), 0,
                                          jnp.minimum(i, nbp - 1) % (T_PAD // TM5))),
                  sb, pb, sb, pb, sb,
                  pl.BlockSpec((TM5, D), lambda i: (i, OFF_GA // D)),
                  pl.BlockSpec((TM5, D), lambda i: (i, OFF_GB // D)),
                  pl.BlockSpec((TM5, D), rb),
                  full((D, D)), full((1, D)), full((D, 128)), full((1, 128)), full((TM5, TM5))],
        out_specs=[pl.BlockSpec((TM5, D), rb), pl.BlockSpec((TM5, D), rb),
                   pl.BlockSpec((TM5, 128), rb), pl.BlockSpec((TM5, 128), rb),
                   pl.BlockSpec((TM5, 128), rb), full((8, 128))],
        out_shape=[jax.ShapeDtypeStruct((ROWS, D), F32), jax.ShapeDtypeStruct((ROWS, D), F32),
                   jax.ShapeDtypeStruct((ROWS, 128), jnp.int32), jax.ShapeDtypeStruct((ROWS, 128), F32),
                   jax.ShapeDtypeStruct((ROWS, 128), jnp.int32), jax.ShapeDtypeStruct((8, 128), F32)],
        scratch_shapes=[pltpu.VMEM((1, 128), F32)],
        compiler_params=_cparams(("arbitrary",)),
        name="mix",
    )(yo_p, yo_s, g_p, g_s, oa_p, oa_s, p_all, p_all, x_all, wo, n2, wr, br,
      jnp.asarray(np.tril(np.ones((TM5, TM5), np.float32), -1), BF16))


N_TOK = NB * (N_META + SEQ) + ROWS_S
NK = N_TOK * TOP_K
TMM = 256
N_BLK = -(-NK // TMM) + N_EXPERTS
NSLOT = N_BLK * TMM
TN_G1 = 1024
TN_G2 = 1024
TMC = 128


def _gather_kernel(slot_ref, nused_ref, h_hbm, o_ref, buf, sem):
    m = pl.program_id(0)
    cur = m % 2

    def issue(blk, slot):
        def row(r, _):
            tok = slot_ref[blk * TMM + r]
            pltpu.make_async_copy(h_hbm.at[pl.ds(tok, 1), :], buf.at[slot, pl.ds(r, 1), :], sem.at[slot]).start()
            return 0

        lax.fori_loop(0, TMM, row, 0, unroll=8)

    @pl.when((m == 0) & (m < nused_ref[0]))
    def _():
        issue(m, 0)

    @pl.when(m + 1 < nused_ref[0])
    def _():
        issue(m + 1, 1 - cur)

    @pl.when(m < nused_ref[0])
    def _():
        pltpu.make_async_copy(h_hbm.at[pl.ds(0, TMM), :], buf.at[cur], sem.at[cur]).wait()
        o_ref[...] = buf[cur].astype(BF16)

    @pl.when(m >= nused_ref[0])
    def _():
        o_ref[...] = jnp.zeros(o_ref.shape, BF16)


def _gather(slot_tok, nused, h2):
    return pl.pallas_call(
        _gather_kernel,
        grid_spec=pltpu.PrefetchScalarGridSpec(
            num_scalar_prefetch=2,
            grid=(N_BLK,),
            in_specs=[pl.BlockSpec(memory_space=pl.ANY)],
            out_specs=pl.BlockSpec((TMM, D), lambda m, st, nu: (m, 0)),
            scratch_shapes=[pltpu.VMEM((2, TMM, D), F32), pltpu.SemaphoreType.DMA((2,))]),
        out_shape=jax.ShapeDtypeStruct((NSLOT, D), BF16),
        compiler_params=_cparams(("arbitrary",)),
        name="moe_gather",
    )(slot_tok, nused, h2)


def _new_expert(blk_e_ref, m):
    return (m == 0) | (blk_e_ref[m] != blk_e_ref[jnp.maximum(m - 1, 0)])


def _gm1_kernel(blk_e_ref, nused_ref, x_ref, w_ref, b_ref, sel_ref, o_ref, wb_scr):
    m = pl.program_id(1)
    used = m < nused_ref[0]

    @pl.when(used & _new_expert(blk_e_ref, m))
    def _():
        wb_scr[...] = w_ref[...].astype(BF16)

    @pl.when(used)
    def _():
        hd = jnp.dot(x_ref[...], wb_scr[...], preferred_element_type=F32) + b_ref[...]
        glu = jnp.minimum(hd, SWIGLU_LIMIT)
        lin = jnp.clip(hd, -SWIGLU_LIMIT, SWIGLU_LIMIT) + 1.0
        act = glu * _sigmoid(SWIGLU_ALPHA * glu)
        for c in range(TN_G1 // 512):
            parts = []
            for q in range(4):
                sl = slice(c * 512 + q * 128, c * 512 + (q + 1) * 128)
                parts.append(act[:, sl] * pltpu.roll(lin[:, sl], 127, axis=1))
            z = jnp.concatenate(parts, axis=1).astype(BF16)
            o_ref[:, c * 256:(c + 1) * 256] = jnp.dot(z, sel_ref[...], preferred_element_type=F32).astype(BF16)

    @pl.when(jnp.logical_not(used))
    def _():
        o_ref[...] = jnp.zeros(o_ref.shape, BF16)


def _gm1(blk_e, nused, xs, w1, b1, sel):
    def meff(m, nu):
        return jnp.maximum(jnp.minimum(m, nu[0] - 1), 0)

    return pl.pallas_call(
        _gm1_kernel,
        grid_spec=pltpu.PrefetchScalarGridSpec(
            num_scalar_prefetch=2,
            grid=(2 * D_FF // TN_G1, N_BLK),
            in_specs=[pl.BlockSpec((TMM, D), lambda f, m, be, nu: (meff(m, nu), 0)),
                      pl.BlockSpec((None, D, TN_G1), lambda f, m, be, nu: (be[meff(m, nu)], 0, f)),
                      pl.BlockSpec((None, 1, TN_G1), lambda f, m, be, nu: (be[meff(m, nu)], 0, f)),
                      pl.BlockSpec((512, 256), lambda f, m, be, nu: (0, 0))],
            out_specs=pl.BlockSpec((TMM, TN_G1 // 2), lambda f, m, be, nu: (m, f)),
            scratch_shapes=[pltpu.VMEM((D, TN_G1), BF16)]),
        out_shape=jax.ShapeDtypeStruct((NSLOT, D_FF), BF16),
        compiler_params=_cparams(("arbitrary", "arbitrary")),
        name="moe_up",
    )(blk_e, nused, xs, w1, b1, sel)


def _gm2_kernel(blk_e_ref, nused_ref, x_ref, w_ref, b_ref, o_ref, wb_scr):
    m = pl.program_id(1)
    used = m < nused_ref[0]

    @pl.when(used & _new_expert(blk_e_ref, m))
    def _():
        wb_scr[...] = w_ref[...].astype(BF16)

    @pl.when(used)
    def _():
        o_ref[...] = jnp.dot(x_ref[...], wb_scr[...], preferred_element_type=F32) + b_ref[...]

    @pl.when(jnp.logical_not(used))
    def _():
        o_ref[...] = jnp.zeros(o_ref.shape, F32)


def _gm2(blk_e, nused, hid, w2, b2):
    def meff(m, nu):
        return jnp.maximum(jnp.minimum(m, nu[0] - 1), 0)

    return pl.pallas_call(
        _gm2_kernel,
        grid_spec=pltpu.PrefetchScalarGridSpec(
            num_scalar_prefetch=2,
            grid=(D // TN_G2, N_BLK),
            in_specs=[pl.BlockSpec((TMM, D_FF), lambda f, m, be, nu: (meff(m, nu), 0)),
                      pl.BlockSpec((None, D_FF, TN_G2), lambda f, m, be, nu: (be[meff(m, nu)], 0, f)),
                      pl.BlockSpec((None, 1, TN_G2), lambda f, m, be, nu: (be[meff(m, nu)], 0, f))],
            out_specs=pl.BlockSpec((TMM, TN_G2), lambda f, m, be, nu: (m, f)),
            scratch_shapes=[pltpu.VMEM((D_FF, TN_G2), BF16)]),
        out_shape=jax.ShapeDtypeStruct((NSLOT, D), F32),
        compiler_params=_cparams(("arbitrary", "arbitrary")),
        name="moe_down",
    )(blk_e, nused, hid, w2, b2)


def _combine_kernel(dest_ref, x1_ref, tg_ref, fg_ref, y_hbm, yp_ref, ys_ref, buf, sem):
    m = pl.program_id(0)
    cur = m % 2

    def issue(blk, slot):
        def row(r, _):
            for k in range(TOP_K):
                d = dest_ref[(blk * TMC + r) * TOP_K + k]
                pltpu.make_async_copy(y_hbm.at[pl.ds(d, 1), :], buf.at[slot, pl.ds(k * TMC + r, 1), :],
                                      sem.at[slot]).start()
            return 0

        lax.fori_loop(0, TMC, row, 0, unroll=4)

    @pl.when(m == 0)
    def _():
        issue(m, 0)

    @pl.when(m + 1 < pl.num_programs(0))
    def _():
        issue(m + 1, 1 - cur)

    pltpu.make_async_copy(y_hbm.at[pl.ds(0, TOP_K * TMC), :], buf.at[cur], sem.at[cur]).wait()
    x2 = x1_ref[...]
    tg = tg_ref[...]
    for k in range(TOP_K):
        x2 = x2 + tg[:, k:k + 1] * buf[cur, k * TMC:(k + 1) * TMC, :]
    ms = jnp.mean(x2 * x2, axis=-1, keepdims=True)
    y = (x2 * lax.rsqrt(ms + RMS_EPS)) * fg_ref[...]

    @pl.when((m < ROWS_P // TMC) & (m % (T_PAD // TMC) >= CB_SKIP))
    def _():
        yp_ref[...] = y

    @pl.when(m >= ROWS_P // TMC)
    def _():
        ys_ref[...] = y


CB_SKIP = (FRONT + N_META) // TMC


def _combine(dest, x1, tg, fg, y):
    per_b = T_PAD // TMC

    def prompt_blk(m, d):
        mp = jnp.minimum(m, ROWS_P // TMC - 1)
        return ((mp // per_b) * (SEQ // TMC) + jnp.maximum(mp % per_b - CB_SKIP, 0), 0)

    return pl.pallas_call(
        _combine_kernel,
        grid_spec=pltpu.PrefetchScalarGridSpec(
            num_scalar_prefetch=1,
            grid=(ROWS // TMC,),
            in_specs=[pl.BlockSpec((TMC, D), lambda m, d: (m, 0)),
                      pl.BlockSpec((TMC, 128), lambda m, d: (m, 0)),
                      pl.BlockSpec((1, D), lambda m, d: (0, 0)),
                      pl.BlockSpec(memory_space=pl.ANY)],
            out_specs=[pl.BlockSpec((TMC, D), prompt_blk),
                       pl.BlockSpec((TMC, D), lambda m, d: (jnp.maximum(m - ROWS_P // TMC, 0), 0))],
            scratch_shapes=[pltpu.VMEM((2, TOP_K * TMC, D), F32), pltpu.SemaphoreType.DMA((2,))]),
        out_shape=[jax.ShapeDtypeStruct((NB * SEQ, D), F32), jax.ShapeDtypeStruct((ROWS_S, D), F32)],
        compiler_params=_cparams(("arbitrary",)),
        name="moe_combine",
    )(dest, x1, tg, fg, y)


def _t5_bucket(rel):
    nb = N_BUCKETS // 2
    ret = jnp.where(rel > 0, nb, 0)
    n = jnp.abs(rel)
    max_exact = nb // 2
    nf = jnp.maximum(n, 1).astype(jnp.float32)
    large = max_exact + (jnp.log(nf / max_exact) / math.log(MAX_DISTANCE / max_exact) * (nb - max_exact)).astype(jnp.int32)
    large = jnp.minimum(large, nb - 1)
    return ret + jnp.where(n < max_exact, n, large)


def _valid_rows():
    rows = [np.arange(b * T_PAD + FRONT, (b + 1) * T_PAD) for b in range(NB)]
    rows.append(np.arange(ROWS_P, ROWS))
    return np.concatenate(rows).astype(np.int32)


def _kv_out_kernel(p_hbm, kp, vp, ks, vs, sem):
    copies = []
    for h in range(A_HEADS):
        ck = pl.ds(OFF_K + h * A_VD, A_VD)
        cv = pl.ds(OFF_V + h * A_VD, A_VD)
        for b in range(NB):
            rows = pl.ds(b * T_PAD + FRONT, N_META + SEQ)
            copies.append((p_hbm.at[rows, ck], kp.at[b, :, h, :]))
            copies.append((p_hbm.at[rows, cv], vp.at[b, :, h, :]))
        copies.append((p_hbm.at[pl.ds(ROWS_P, ROWS_S), ck], ks.at[:, h, :]))
        copies.append((p_hbm.at[pl.ds(ROWS_P, ROWS_S), cv], vs.at[:, h, :]))
    dmas = [pltpu.make_async_copy(src, dst, sem.at[i]) for i, (src, dst) in enumerate(copies)]
    for c in dmas:
        c.start()
    for c in dmas:
        c.wait()


def _kv_out(p_all):
    anyspec = pl.BlockSpec(memory_space=pl.ANY)
    return pl.pallas_call(
        _kv_out_kernel,
        in_specs=[anyspec],
        out_specs=[anyspec] * 4,
        out_shape=[jax.ShapeDtypeStruct((NB, N_META + SEQ, A_HEADS, A_VD), F32)] * 2
                  + [jax.ShapeDtypeStruct((ROWS_S, A_HEADS, A_VD), F32)] * 2,
        scratch_shapes=[pltpu.SemaphoreType.DMA((A_HEADS * (2 * NB + 2),))],
        name="kv_out",
    )(p_all)


def _slot_tables(te, rank, cnt):
    counts = cnt[0, :N_EXPERTS].astype(jnp.int32)
    padded = (counts + TMM - 1) // TMM * TMM
    pad_end = jnp.cumsum(padded)
    pad_start = pad_end - padded
    experts = jnp.arange(N_EXPERTS, dtype=jnp.int32)
    routed = jnp.asarray(np.isin(np.arange(ROWS), _valid_rows()))[:, None]
    start_of = jnp.sum(jnp.where(te[:, :TOP_K, None] == experts, pad_start, 0), axis=-1)
    dest2 = start_of + rank[:, :TOP_K]
    dest = jnp.where(routed, dest2, 0).reshape(-1)
    rows = jnp.broadcast_to(jnp.arange(ROWS, dtype=jnp.int32)[:, None], (ROWS, TOP_K))
    slot_tok = jnp.zeros((NSLOT,), jnp.int32).at[jnp.where(routed, dest2, NSLOT).reshape(-1)].set(
        rows.reshape(-1), mode='drop')
    blk_start = jnp.arange(N_BLK, dtype=jnp.int32) * TMM
    blk_e = jnp.minimum(jnp.sum(pad_end[None, :] <= blk_start[:, None], axis=-1), N_EXPERTS - 1).astype(jnp.int32)
    nused = (pad_end[-1:] // TMM).astype(jnp.int32)
    return dest, slot_tok, blk_e, nused


def kernel(x_prompt, x_sample, cache_k, cache_v, state_wkv, state_shift, meta_tokens, rel_bias_table, norm1_g, w_in, shift_mu, decay_base, w_decay_up, a_base, w_a_up, w_g_up, k_k, k_a, r_k, gn_w, gn_b, lam_q1, lam_k1, lam_q2, lam_k2, subln_g, w_out, norm2_g, w_router, b_router, w_e1, b_e1, w_e2, b_e2, final_g):
    xp = jnp.concatenate([jnp.zeros((NB, FRONT, D), F32),
                          jnp.broadcast_to(meta_tokens[None], (NB, N_META, D)), x_prompt], axis=1)
    x_all = jnp.concatenate([xp.reshape(ROWS_P, D), x_sample.reshape(ROWS_S, D)], axis=0)
    lora0 = OFF_R + 3 * D
    w_main = w_in[0].astype(BF16)
    w_tail = jnp.concatenate([w_main[:, lora0 + LORA_W:], w_main[:, lora0:lora0 + LORA_W],
                              jnp.zeros((D, LORA_P - LORA_W), BF16)], axis=1)

    p_all = _inproj(x_all, norm1_g, w_main, w_tail)

    kq = jnp.arange(TQ, dtype=jnp.int32)
    bkt = jnp.stack([_t5_bucket(kq[:, None] - kq[None, :]), _t5_bucket(kq[:, None] - kq[None, :] - TQ)])
    lamp = jnp.concatenate([lam_q1, lam_k1, lam_q2, lam_k2], axis=0)
    oa_p = _attn_prompt(p_all, rel_bias_table, bkt, lamp, subln_g)
    qpos = PAST + jnp.arange(DEC_S, dtype=jnp.int32)
    bkt_last = _t5_bucket((PAST - NEAR + jnp.arange(NEAR, dtype=jnp.int32))[None, :] - qpos[:, None])
    bkt_new = _t5_bucket(qpos[None, :] - qpos[:, None])
    oa_s = _attn_sample(p_all, cache_k[0], cache_v[0],
                        rel_bias_table, bkt_last, bkt_new, lamp, subln_g)

    mu = shift_mu[0]
    mu3 = mu[:3 * D][None]
    mul = jnp.pad(mu[3 * D:], (0, LORA_P - LORA_W))[None]
    wd = jnp.zeros((LORA_P, D), F32).at[:DECAY_LORA].set(w_decay_up[0]).astype(BF16)
    wa = jnp.zeros((LORA_P, D), F32).at[DECAY_LORA:DECAY_LORA + AAA_LORA].set(w_a_up[0]).astype(BF16)
    wg = jnp.zeros((LORA_P, D), F32).at[DECAY_LORA + AAA_LORA:LORA_W].set(w_g_up[0]).astype(BF16)
    first_p = jnp.zeros((NB, 1, 3 * D + LORA_P), F32)
    first_s = jnp.pad(state_shift[0], ((0, 0), (0, LORA_P - LORA_W)))[:, None, :]
    prep_p = _rwkv_prep(p_all, first_p, mu3, mul, decay_base, a_base, wd, wa, wg, NB, T_PAD // 256, 256, 0, True)
    prep_s = _rwkv_prep(p_all, first_s, mu3, mul, decay_base, a_base, wd, wa, wg, DEC_B, 1, DEC_S, ROWS_P // DEC_S,
                        False)

    def chain_tile(v):
        return jnp.tile(v.reshape(B_HEADS, B_HEAD).T, (1, 128 // B_HEADS))

    par = [chain_tile(v.reshape(-1)) for v in (k_k[0], k_a[0], r_k[0], gn_w[0], gn_b[0])]
    s0_p = jnp.zeros((B_HEAD, B_HEAD, NB * B_HEADS), F32)
    s0_s = state_wkv[0].transpose(3, 2, 0, 1).reshape(B_HEAD, B_HEAD, DEC_B * B_HEADS)
    yo_p, sfin_p = _rwkv_rec(*[a.transpose(2, 1, 0) for a in prep_p[:5]], *par, s0_p, skip=FRONT // TB)
    yo_s, sfin_s = _rwkv_rec(*[_to_chains(a, DEC_B, DEC_S) for a in prep_s[:5]], *par, s0_s)

    wo = w_out[0].astype(BF16)
    wr = jnp.pad(w_router[0], ((0, 0), (0, 128 - N_EXPERTS)))
    br = jnp.concatenate([b_router[0], jnp.full((128 - N_EXPERTS,), NEG, F32)])[None]
    x1, h2, te, tg, rank, cnt = _mix(yo_p.transpose(2, 1, 0), _from_chains(yo_s, DEC_B, DEC_S), prep_p[5], prep_s[5],
                                     oa_p, oa_s, p_all, x_all, wo, norm2_g, wr, br)

    dest, slot_tok, blk_e, nused = _slot_tables(te, rank, cnt)

    sel_np = np.zeros((512, 256), np.float32)
    sel_np[2 * np.arange(256), np.arange(256)] = 1.0
    xs = _gather(slot_tok, nused, h2)
    hid = _gm1(blk_e, nused, xs, w_e1[0], b_e1[0][:, None, :], jnp.asarray(sel_np, BF16))
    ys = _gm2(blk_e, nused, hid, w_e2[0], b_e2[0][:, None, :])
    y_prompt, y_sample = _combine(dest, x1, tg, final_g[None], ys)

    y_prompt = y_prompt.reshape(NB, SEQ, D)
    y_sample = y_sample.reshape(DEC_B, DEC_S, D)
    def last_ps(last):
        return last[:, 0, :SHIFT_W][None]

    kp, vp, ksm, vsm = _kv_out(p_all)
    k_prompt, v_prompt = kp[None], vp[None]
    k_sample = ksm.reshape(1, DEC_B, DEC_S, A_HEADS, A_VD)
    v_sample = vsm.reshape(1, DEC_B, DEC_S, A_HEADS, A_VD)
    shift_prompt = last_ps(prep_p[6])
    shift_sample = last_ps(prep_s[6])
    wkv_prompt = sfin_p.reshape(B_HEAD, B_HEAD, NB, B_HEADS).transpose(2, 3, 1, 0)[None]
    wkv_sample = sfin_s.reshape(B_HEAD, B_HEAD, DEC_B, B_HEADS).transpose(2, 3, 1, 0)[None].astype(state_wkv.dtype)
    return (y_prompt, y_sample, k_prompt, v_prompt, wkv_prompt, shift_prompt,
            k_sample, v_sample, wkv_sample, shift_sample)
```

```python
import functools
import math

import numpy as np
import jax
import jax.numpy as jnp
from jax import lax
from jax.experimental import pallas as pl
from jax.experimental.pallas import tpu as pltpu

F32 = jnp.float32
BF16 = jnp.bfloat16

D = 2048
NB = 4
SEQ = 2048
DEC_B = 16
DEC_S = 32
PAST = 2048
CHUNK = 64
N_META = 16
RMS_EPS = 1e-5
A_DH = 64
A_HEADS = 16
A_VD = 128
B_HEAD = 64
B_HEADS = 32
DECAY_LORA = 96
AAA_LORA = 96
GATE_LORA = 256
GN_EPS = 64e-5
N_BUCKETS = 32
MAX_DISTANCE = 128
N_EXPERTS = 32
TOP_K = 4
D_FF = 2048
SWIGLU_ALPHA = 1.702
SWIGLU_LIMIT = 7.0
LAM_INIT = 0.8 - 0.6 * math.exp(-0.3 * 0)
SHIFT_W = 3 * D + DECAY_LORA + AAA_LORA + GATE_LORA
LORA_W = DECAY_LORA + AAA_LORA + GATE_LORA
LORA_P = 512

T_PAD = 2304
FRONT = T_PAD - N_META - SEQ
ROWS_P = NB * T_PAD
ROWS_S = DEC_B * DEC_S
ROWS = ROWS_P + ROWS_S

OFF_Q = 0
OFF_K = 2048
OFF_V = 4096
OFF_R = 6144
OFF_KX = 8192
OFF_VX = 10240
OFF_GA = 12288
OFF_GB = 14336
OFF_LORA = 16384
PW = OFF_LORA + LORA_P

NEG = -1e30
LOG2E = 1.4426950408889634
VMEM_LIMIT = 56 * 1024 * 1024


def _cparams(sem, vmem=VMEM_LIMIT):
    return pltpu.CompilerParams(dimension_semantics=sem, vmem_limit_bytes=vmem)


def _sigmoid(x):
    return 1.0 / (1.0 + jnp.exp(-x))


TM1 = 512
TN1 = 1536
N_MAIN = OFF_GA // TN1


def _inproj_kernel(x_ref, g_ref, wm_ref, wt_ref, o_ref, h_scr):
    j = pl.program_id(1)

    @pl.when(j == 0)
    def _():
        x = x_ref[...]
        ms = jnp.mean(x * x, axis=-1, keepdims=True)
        h_scr[...] = ((x * lax.rsqrt(ms + RMS_EPS)) * g_ref[...]).astype(BF16)

    @pl.when(j < N_MAIN)
    def _():
        o_ref[...] = jnp.dot(h_scr[...], wm_ref[...], preferred_element_type=F32)

    @pl.when(j >= N_MAIN)
    def _():
        o_ref[...] = jnp.dot(h_scr[...], wt_ref[...], preferred_element_type=F32)


def _inproj(x_all, g, w_main, w_tail):
    return pl.pallas_call(
        _inproj_kernel,
        grid=(ROWS // TM1, PW // TN1),
        in_specs=[pl.BlockSpec((TM1, D), lambda i, j: (i, 0)),
                  pl.BlockSpec((1, D), lambda i, j: (0, 0)),
                  pl.BlockSpec((D, TN1), lambda i, j: (0, jnp.minimum(j, N_MAIN - 1))),
                  pl.BlockSpec((D, TN1), lambda i, j: (0, jnp.maximum(j - N_MAIN, 0)))],
        out_specs=pl.BlockSpec((TM1, TN1), lambda i, j: (i, j)),
        out_shape=jax.ShapeDtypeStruct((ROWS, PW), F32),
        scratch_shapes=[pltpu.VMEM((TM1, D), BF16)],
        compiler_params=_cparams(("arbitrary", "arbitrary")),
        name="inproj",
    )(x_all, g, w_main, w_tail)


def _lam_value(lamp_ref):
    lp = lamp_ref[...]
    s1 = jnp.sum(lp[0:1, :] * lp[1:2, :], axis=-1, keepdims=True)
    s2 = jnp.sum(lp[2:3, :] * lp[3:4, :], axis=-1, keepdims=True)
    return jnp.exp(s1) - jnp.exp(s2) + LAM_INIT


def _bias_from_buckets(bkt, table_ref, h):
    out = jnp.zeros(bkt.shape, F32)
    for n in range(N_BUCKETS):
        out = jnp.where(bkt == n, table_ref[n, h], out)
    return out


TQ = 256
NQB = T_PAD // TQ
BIAS_META, BIAS_DIAG, BIAS_SUB, BIAS_SUB_PAD, BIAS_FAR_PAD, BIAS_FAR = range(6)


def _attn_prompt_kernel(table_ref, q_ref, k_ref, v_ref, bkt_ref, lamp_ref, subg_ref, o_ref,
                        kb_scr, vt_scr, bias_scr, s_scr, acc_scr):
    h = pl.program_id(0)
    b = pl.program_id(1)

    @pl.when(b == 0)
    def _():
        kk = lax.broadcasted_iota(jnp.int32, (TQ, TQ), 0)
        qq = lax.broadcasted_iota(jnp.int32, (TQ, TQ), 1)
        chunk_ok = (kk // CHUNK) <= (qq // CHUNK)
        pad = jnp.where(kk < FRONT, NEG, 0.0).astype(F32)
        b_diag = _bias_from_buckets(bkt_ref[0], table_ref, h)
        b_sub = _bias_from_buckets(bkt_ref[1], table_ref, h)
        b_far = jnp.zeros((TQ, TQ), F32) + table_ref[N_BUCKETS // 2 - 1, h]
        tiles = (jnp.where(chunk_ok, b_diag, NEG) + pad,
                 jnp.where(chunk_ok, b_diag, NEG),
                 b_sub,
                 b_sub + pad,
                 b_far + pad,
                 b_far)
        for i, t in enumerate(tiles):
            bias_scr[i] = jnp.concatenate([t, t], axis=1) * LOG2E

    kb_scr[...] = k_ref[...].astype(BF16)
    for i in range(NQB):
        vt_scr[i] = v_ref[i * TQ:(i + 1) * TQ, :].T.astype(BF16)

    lam = _lam_value(lamp_ref)
    drow = lax.broadcasted_iota(jnp.int32, (A_VD, TQ), 0)

    def q_operand(qi):
        q0 = pl.multiple_of(qi * TQ, TQ)
        qt = (q_ref[pl.ds(q0, TQ), :] * (A_DH ** -0.5 * LOG2E)).T
        q1 = jnp.where(drow < A_DH, qt, 0.0)
        q2 = jnp.where(drow >= A_DH, qt, 0.0)
        return jnp.concatenate([q1, q2], axis=1).astype(BF16)

    def scores(kj, qpt, slot):
        k0 = pl.multiple_of(kj * TQ, TQ)
        s_scr[slot] = jnp.dot(kb_scr[pl.ds(k0, TQ), :], qpt, preferred_element_type=F32)

    def accumulate(kj, bias_id, slot, ml):
        m, l = ml
        s = s_scr[slot] + bias_scr[bias_id]
        m_new = jnp.maximum(m, jnp.max(s, axis=0, keepdims=True))
        alpha = jnp.exp2(m - m_new)
        p = jnp.exp2(s - m_new)
        l = l * alpha + jnp.sum(p, axis=0, keepdims=True)
        acc_scr[...] = acc_scr[...] * alpha + jnp.dot(vt_scr[kj], p.astype(BF16), preferred_element_type=F32)
        return m_new, l

    def finish(qi, ml):
        on = acc_scr[...] / ml[1]
        ot = on[:, :TQ] - lam * on[:, TQ:]
        ms = jnp.mean(ot * ot, axis=0, keepdims=True)
        o = (ot * lax.rsqrt(ms + RMS_EPS)).T * subg_ref[...] * (1.0 - LAM_INIT)
        q0 = pl.multiple_of(qi * TQ, TQ)
        o_ref[pl.ds(q0, TQ), :] = o

    def start(qi):
        qpt = q_operand(qi)
        acc_scr[...] = jnp.zeros(acc_scr.shape, F32)
        scores(0, qpt, 0)
        return qpt, (jnp.full((1, 2 * TQ), NEG, F32), jnp.zeros((1, 2 * TQ), F32))

    finish(0, accumulate(0, BIAS_META, 0, start(0)[1]))

    def q_block(qi, _):
        qpt, ml = start(qi)

        def half(kj, cur, ml):
            scores(kj + 1, qpt, 1 - cur)
            first = kj == 0
            bias_id = jnp.where(kj == qi - 1, jnp.where(first, BIAS_SUB_PAD, BIAS_SUB),
                                jnp.where(first, BIAS_FAR_PAD, BIAS_FAR))
            return accumulate(kj, bias_id, cur, ml)

        ml = lax.fori_loop(0, qi // 2, lambda i, ml: half(2 * i + 1, 1, half(2 * i, 0, ml)), ml)
        ml = lax.cond(qi % 2 == 1,
                      lambda ml: accumulate(qi, BIAS_DIAG, 1, half(qi - 1, 0, ml)),
                      lambda ml: accumulate(qi, BIAS_DIAG, 0, ml), ml)
        finish(qi, ml)
        return 0

    lax.fori_loop(1, NQB, q_block, 0)


def _attn_prompt(p_all, table, bkt, lamp, subg):
    cb = D // A_VD
    return pl.pallas_call(
        _attn_prompt_kernel,
        grid_spec=pltpu.PrefetchScalarGridSpec(
            num_scalar_prefetch=0,
            grid=(A_HEADS, NB),
            in_specs=[pl.BlockSpec(memory_space=pltpu.SMEM),
                      pl.BlockSpec((T_PAD, A_VD), lambda h, b: (b, h)),
                      pl.BlockSpec((T_PAD, A_VD), lambda h, b: (b, cb + h)),
                      pl.BlockSpec((T_PAD, A_VD), lambda h, b: (b, 2 * cb + h)),
                      pl.BlockSpec((2, TQ, TQ), lambda h, b: (0, 0, 0)),
                      pl.BlockSpec((4, A_DH), lambda h, b: (0, 0)),
                      pl.BlockSpec((1, A_VD), lambda h, b: (0, 0))],
            out_specs=pl.BlockSpec((T_PAD, A_VD), lambda h, b: (b, h)),
            scratch_shapes=[pltpu.VMEM((T_PAD, A_VD), BF16),
                            pltpu.VMEM((NQB, A_VD, TQ), BF16),
                            pltpu.VMEM((6, TQ, 2 * TQ), F32),
                            pltpu.VMEM((2, TQ, 2 * TQ), F32),
                            pltpu.VMEM((A_VD, 2 * TQ), F32)]),
        out_shape=jax.ShapeDtypeStruct((ROWS_P, D), F32),
        compiler_params=_cparams(("arbitrary", "arbitrary")),
        name="attn_prompt",
    )(table, p_all, p_all, p_all, bkt, lamp, subg)


HG = 4
NEAR = 512


def _attn_sample_kernel(table_ref, q_ref, kn_ref, vn_ref, bl_ref, bn_ref, lamp_ref, subg_ref, ck_hbm, cv_hbm,
                        o_ref, kbuf, vbuf, sem):
    hg = pl.program_id(1)
    ng = pl.num_programs(1)
    step = pl.program_id(0) * ng + hg
    nsteps = pl.num_programs(0) * ng

    def copies(n, slot):
        out = []
        for i in range(HG):
            h = (n % ng) * HG + i
            out.append(pltpu.make_async_copy(ck_hbm.at[n // ng, :, h, :], kbuf.at[slot, i], sem.at[slot, i]))
            out.append(pltpu.make_async_copy(cv_hbm.at[n // ng, :, h, :], vbuf.at[slot, i], sem.at[slot, HG + i]))
        return out

    cur = step % 2

    @pl.when(step == 0)
    def _():
        for c in copies(step, 0):
            c.start()

    @pl.when(step + 1 < nsteps)
    def _():
        for c in copies(step + 1, 1 - cur):
            c.start()

    for c in copies(step, cur):
        c.wait()
    ck_refs = [kbuf.at[cur, i] for i in range(HG)]
    cv_refs = [vbuf.at[cur, i] for i in range(HG)]
    lam = _lam_value(lamp_ref)
    lane = lax.broadcasted_iota(jnp.int32, (DEC_S, A_VD), 1)
    nt = (((1,), (1,)), ((), ()))
    for i in range(HG):
        h = hg * HG + i
        cols = slice(i * A_VD, (i + 1) * A_VD)
        q = q_ref[:, cols] * (A_DH ** -0.5)
        qp = jnp.concatenate([jnp.where(lane < A_DH, q, 0.0), jnp.where(lane >= A_DH, q, 0.0)],
                             axis=0).astype(BF16)
        far = jnp.zeros((DEC_S, PAST - NEAR), F32) + table_ref[N_BUCKETS // 2 - 1, h]
        bl = jnp.concatenate([far, _bias_from_buckets(bl_ref[...], table_ref, h)], axis=1)
        bn = _bias_from_buckets(bn_ref[...], table_ref, h)
        s = (lax.dot_general(qp, ck_refs[i][...].astype(BF16), nt, preferred_element_type=F32)
             + jnp.concatenate([bl, bl], axis=0))
        sn = (lax.dot_general(qp, kn_ref[:, cols].astype(BF16), nt, preferred_element_type=F32)
              + jnp.concatenate([bn, bn], axis=0))
        m = jnp.maximum(jnp.max(s, axis=-1, keepdims=True), jnp.max(sn, axis=-1, keepdims=True))
        p = jnp.exp(s - m)
        pn = jnp.exp(sn - m)
        l = jnp.sum(p, axis=-1, keepdims=True) + jnp.sum(pn, axis=-1, keepdims=True)
        acc = (jnp.dot(p.astype(BF16), cv_refs[i][...].astype(BF16), preferred_element_type=F32)
               + jnp.dot(pn.astype(BF16), vn_ref[:, cols].astype(BF16), preferred_element_type=F32))
        on = acc / l
        o = on[:DEC_S] - lam * on[DEC_S:]
        ms = jnp.mean(o * o, axis=-1, keepdims=True)
        o_ref[:, cols] = (o * lax.rsqrt(ms + RMS_EPS)) * subg_ref[...] * (1.0 - LAM_INIT)


def _attn_sample(p_all, cache_k, cache_v, table, bkt_last, bkt_new, lamp, subg):
    r0 = ROWS_P // DEC_S
    w = HG * A_VD
    new = lambda off: pl.BlockSpec((DEC_S, w), lambda b, g: (r0 + b, off // w + g))
    return pl.pallas_call(
        _attn_sample_kernel,
        grid=(DEC_B, A_HEADS // HG),
        in_specs=[pl.BlockSpec(memory_space=pltpu.SMEM), new(OFF_Q), new(OFF_K), new(OFF_V),
                  pl.BlockSpec((DEC_S, NEAR), lambda b, g: (0, 0)),
                  pl.BlockSpec((DEC_S, DEC_S), lambda b, g: (0, 0)),
                  pl.BlockSpec((4, A_DH), lambda b, g: (0, 0)),
                  pl.BlockSpec((1, A_VD), lambda b, g: (0, 0)),
                  pl.BlockSpec(memory_space=pl.ANY), pl.BlockSpec(memory_space=pl.ANY)],
        out_specs=pl.BlockSpec((DEC_S, w), lambda b, g: (b, g)),
        out_shape=jax.ShapeDtypeStruct((ROWS_S, D), F32),
        scratch_shapes=[pltpu.VMEM((2, HG, PAST, A_VD), F32), pltpu.VMEM((2, HG, PAST, A_VD), F32),
                        pltpu.SemaphoreType.DMA((2, 2 * HG))],
        compiler_params=_cparams(("arbitrary", "arbitrary")),
        name="attn_sample",
    )(table, p_all, p_all, p_all, bkt_last, bkt_new, lamp, subg, cache_k, cache_v)


def _rwkv_prep_kernel(chain_out, r_ref, k_ref, v_ref, lo_ref, first_ref, mu_ref, mul_ref, dbase_ref, abase_ref,
                      wd_ref, wa_ref, wg_ref,
                      xr_ref, xk_ref, xv_ref, wp_ref, ap_ref, g_ref, last_ref, carry_scr, carryl_scr):
    i = pl.program_id(1)
    tm = r_ref.shape[0]
    row = lax.broadcasted_iota(jnp.int32, (tm, 1), 0)

    def shift(x, first, mu):
        prev = jnp.where(row == 0, first, pltpu.roll(x, 1, axis=0))
        return x + (prev - x) * mu

    def put(dst, val):
        if chain_out:
            for hp in range(B_HEADS // 2):
                t = val[:, hp * 128:(hp + 1) * 128].T
                dst[2 * hp] = t[:B_HEAD]
                dst[2 * hp + 1] = t[B_HEAD:]
        else:
            dst[...] = val

    for s, (src, dst) in enumerate(((r_ref, xr_ref), (k_ref, xk_ref), (v_ref, xv_ref))):
        x = src[...]
        first = jnp.where(i == 0, first_ref[:, s * D:(s + 1) * D], carry_scr[:, s * D:(s + 1) * D])
        put(dst, shift(x, first, mu_ref[:, s * D:(s + 1) * D]))
        carry_scr[:, s * D:(s + 1) * D] = x[tm - 1:tm, :]
        last_ref[:, s * D:(s + 1) * D] = x[tm - 1:tm, :]

    xl = lo_ref[...]
    first = jnp.where(i == 0, first_ref[:, 3 * D:], carryl_scr[...])
    xs = shift(xl, first, mul_ref[...])
    carryl_scr[...] = xl[tm - 1:tm, :]
    last_ref[:, 3 * D:] = xl[tm - 1:tm, :]
    put(wp_ref, dbase_ref[...] + jnp.dot(jnp.tanh(xs).astype(BF16), wd_ref[...], preferred_element_type=F32))
    put(ap_ref, abase_ref[...] + jnp.dot(xs.astype(BF16), wa_ref[...], preferred_element_type=F32))
    g_ref[...] = jnp.dot(_sigmoid(xs).astype(BF16), wg_ref[...], preferred_element_type=F32)


def _rwkv_prep(p_all, first, mu, mul, dbase, abase, wd, wa, wg, nseq, nblk, tm, row_blk0, chain_out):
    rows = nseq * nblk * tm
    cs = lambda off: (lambda s, i: (row_blk0 + s * nblk + i, off))
    full = lambda shp: pl.BlockSpec(shp, lambda s, i: tuple(0 for _ in shp))
    out = jax.ShapeDtypeStruct((rows, D), F32)
    ospec = pl.BlockSpec((tm, D), lambda s, i: (s * nblk + i, 0))
    if chain_out:
        cout = jax.ShapeDtypeStruct((nseq * B_HEADS, B_HEAD, nblk * tm), F32)
        cspec = pl.BlockSpec((B_HEADS, B_HEAD, tm), lambda s, i: (s, 0, i))
    else:
        cout, cspec = out, ospec
    return pl.pallas_call(
        functools.partial(_rwkv_prep_kernel, chain_out),
        grid=(nseq, nblk),
        in_specs=[pl.BlockSpec((tm, D), cs(OFF_R // D)),
                  pl.BlockSpec((tm, D), cs(OFF_KX // D)),
                  pl.BlockSpec((tm, D), cs(OFF_VX // D)),
                  pl.BlockSpec((tm, LORA_P), cs(OFF_LORA // LORA_P)),
                  pl.BlockSpec((None, 1, 3 * D + LORA_P), lambda s, i: (s, 0, 0)),
                  full((1, 3 * D)), full((1, LORA_P)), full((1, D)), full((1, D)),
                  full((LORA_P, D)), full((LORA_P, D)), full((LORA_P, D))],
        out_specs=[cspec] * 5 + [ospec, pl.BlockSpec((None, 1, 3 * D + LORA_P), lambda s, i: (s, 0, 0))],
        out_shape=[cout] * 5 + [out, jax.ShapeDtypeStruct((nseq, 1, 3 * D + LORA_P), F32)],
        scratch_shapes=[pltpu.VMEM((1, 3 * D), F32), pltpu.VMEM((1, LORA_P), F32)],
        compiler_params=_cparams(("arbitrary", "arbitrary")),
        name="rwkv_prep",
    )(p_all, p_all, p_all, p_all, first, mu, mul, dbase, abase, wd, wa, wg)


TB = 32
IH = 32


def _rwkv_rec_kernel(skip, *refs):
    tb = pl.program_id(1)
    s0_ref, o_ref, sfin_ref, S = refs[10], refs[11], refs[12], refs[13]

    @pl.when(tb == 0)
    def _():
        S[...] = s0_ref[...]

    if skip:
        @pl.when(tb < skip)
        def _():
            o_ref[...] = jnp.zeros(o_ref.shape, F32)

        pl.when(tb >= skip)(functools.partial(_rwkv_rec_block, *refs))
    else:
        _rwkv_rec_block(*refs)
    sfin_ref[...] = S[...]


def _rwkv_rec_block(xr_ref, xk_ref, xv_ref, wp_ref, ap_ref, kk_ref, ka_ref, rk_ref, gnw_ref, gnb_ref, s0_ref,
                    o_ref, sfin_ref, S, w_s, k_s, a_s, b_s, sa_s, y_s):
    z = -wp_ref[...]
    softplus = jnp.maximum(z, 0.0) + jnp.log(1.0 + jnp.exp(-jnp.abs(z)))
    w_s[...] = jnp.exp(-jnp.exp(-softplus - 0.5))
    a = _sigmoid(ap_ref[...])
    xk = xk_ref[...]
    kk = xk * kk_ref[...]
    nrm = jnp.sqrt(jnp.sum(kk * kk, axis=1, keepdims=True))
    kk = kk / jnp.maximum(nrm, 1e-12)
    k_s[...] = xk * (1.0 + (a - 1.0) * ka_ref[...])
    a_s[...] = -kk
    b_s[...] = kk * a

    for hf in range(B_HEAD // IH):
        i0 = hf * IH
        acc = jnp.zeros((IH, 128), F32)
        for j in range(B_HEAD):
            acc = acc + S[j, i0:i0 + IH, :] * a_s[0, j:j + 1, :]
        sa_s[0, i0:i0 + IH, :] = acc

    def token(t, _):
        cur = t % 2
        tn = jnp.minimum(t + 1, TB - 1)
        for hf in range(B_HEAD // IH):
            i0 = hf * IH
            sa = sa_s[cur, i0:i0 + IH, :]
            v = xv_ref[t, i0:i0 + IH, :]
            y = jnp.zeros((IH, 128), F32)
            san = jnp.zeros((IH, 128), F32)
            for j in range(B_HEAD):
                s_new = (S[j, i0:i0 + IH, :] * w_s[t, j:j + 1, :] + sa * b_s[t, j:j + 1, :]
                         + v * k_s[t, j:j + 1, :])
                S[j, i0:i0 + IH, :] = s_new
                y = y + s_new * xr_ref[t, j:j + 1, :]
                san = san + s_new * a_s[tn, j:j + 1, :]
            y_s[i0:i0 + IH, :] = y
            sa_s[1 - cur, i0:i0 + IH, :] = san
        y = y_s[...]
        mu = jnp.mean(y, axis=0, keepdims=True)
        yc = y - mu
        var = jnp.mean(yc * yc, axis=0, keepdims=True)
        bonus = jnp.sum(xr_ref[t] * k_s[t] * rk_ref[...], axis=0, keepdims=True)
        o_ref[t] = yc * lax.rsqrt(var + GN_EPS) * gnw_ref[...] + gnb_ref[...] + bonus * xv_ref[t]
        return 0

    lax.fori_loop(0, TB, token, 0)


def _rwkv_rec(xr, xk, xv, wp, ap, kk, ka, rk, gnw, gnb, s0, skip=0):
    t, _, c = xr.shape
    tok = pl.BlockSpec((TB, B_HEAD, 128), lambda g, tb: (tb, 0, g))
    par = pl.BlockSpec((B_HEAD, 128), lambda g, tb: (0, 0))
    st = pl.BlockSpec((B_HEAD, B_HEAD, 128), lambda g, tb: (0, 0, g))
    blk = pltpu.VMEM((TB, B_HEAD, 128), F32)
    return pl.pallas_call(
        functools.partial(_rwkv_rec_kernel, skip),
        grid=(c // 128, t // TB),
        in_specs=[tok] * 5 + [par] * 5 + [st],
        out_specs=[tok, st],
        out_shape=[jax.ShapeDtypeStruct((t, B_HEAD, c), F32), jax.ShapeDtypeStruct((B_HEAD, B_HEAD, c), F32)],
        scratch_shapes=[pltpu.VMEM((B_HEAD, B_HEAD, 128), F32), blk, blk, blk, blk,
                        pltpu.VMEM((2, B_HEAD, 128), F32), pltpu.VMEM((B_HEAD, 128), F32)],
        compiler_params=_cparams(("arbitrary", "arbitrary")),
        name="rwkv_rec",
    )(xr, xk, xv, wp, ap, kk, ka, rk, gnw, gnb, s0)


def _to_chains(a, nb, t):
    return a.reshape(nb, t, B_HEADS, B_HEAD).transpose(1, 3, 0, 2).reshape(t, B_HEAD, nb * B_HEADS)


def _from_chains(a, nb, t):
    return a.reshape(t, B_HEAD, nb, B_HEADS).transpose(2, 0, 3, 1).reshape(nb * t, D)


TM5 = 256


def _mix_kernel(yo_p_ref, yo_s_ref, g_p_ref, g_s_ref, oa_p_ref, oa_s_ref, ga_ref, gb_ref, x_ref,
                wo_ref, n2_ref, wr_ref, br_ref, ltri_ref, x1_ref, h2_ref, te_ref, tg_ref, rank_ref, cnt_ref, cnt_scr):
    i = pl.program_id(0)
    is_p = i < ROWS_P // TM5
    tiles = [jnp.concatenate([yo_p_ref[2 * hp], yo_p_ref[2 * hp + 1]], axis=0).T for hp in range(B_HEADS // 2)]
    yo = jnp.where(is_p, jnp.concatenate(tiles, axis=1), yo_s_ref[...])
    g = jnp.where(is_p, g_p_ref[...], g_s_ref[...])
    oa = jnp.where(is_p, oa_p_ref[...], oa_s_ref[...])
    mixed = _sigmoid(ga_ref[...]) * oa + _sigmoid(gb_ref[...]) * (yo * g)
    x1 = x_ref[...] + jnp.dot(mixed.astype(BF16), wo_ref[...], preferred_element_type=F32)
    x1_ref[...] = x1
    ms = jnp.mean(x1 * x1, axis=-1, keepdims=True)
    h2 = (x1 * lax.rsqrt(ms + RMS_EPS)) * n2_ref[...]
    h2_ref[...] = h2
    logits = jnp.dot(h2, wr_ref[...], preferred_element_type=F32, precision=lax.Precision.HIGHEST) + br_ref[...]
    lane = lax.broadcasted_iota(jnp.int32, logits.shape, 1)
    lane_f = lane.astype(F32)
    te = jnp.zeros(logits.shape, jnp.int32)
    tv = jnp.full(logits.shape, NEG, F32)
    chosen = []
    for k in range(TOP_K):
        m = jnp.max(logits, axis=-1, keepdims=True)
        idx = jnp.min(jnp.where(logits == m, lane_f, 128.0), axis=-1, keepdims=True).astype(jnp.int32)
        te = jnp.where(lane == k, idx, te)
        tv = jnp.where(lane == k, m, tv)
        chosen.append(lane == idx)
        logits = jnp.where(chosen[k], NEG, logits)
    e = jnp.where(lane < TOP_K, jnp.exp(tv - jnp.max(tv, axis=-1, keepdims=True)), 0.0)
    te_ref[...] = te
    tg_ref[...] = e / jnp.sum(e, axis=-1, keepdims=True)

    @pl.when(i == 0)
    def _():
        cnt_scr[...] = jnp.zeros(cnt_scr.shape, F32)

    first_row = jnp.where(is_p, (i % (T_PAD // TM5)) * TM5, FRONT)
    routed = (lax.broadcasted_iota(jnp.int32, (TM5, 1), 0) + first_row) >= FRONT
    hits = jnp.zeros(logits.shape, F32)
    for k in range(TOP_K):
        hits = hits + jnp.where(chosen[k], 1.0, 0.0)
    hits = jnp.where(routed, hits, 0.0)
    before = cnt_scr[...] + jnp.dot(ltri_ref[...], hits.astype(BF16), preferred_element_type=F32)
    rank = jnp.zeros(logits.shape, jnp.int32)
    for k in range(TOP_K):
        rk = jnp.sum(jnp.where(chosen[k], before, 0.0), axis=-1, keepdims=True).astype(jnp.int32)
        rank = jnp.where(lane == k, rk, rank)
    rank_ref[...] = rank
    cnt_scr[...] = cnt_scr[...] + jnp.sum(hits, axis=0, keepdims=True)
    cnt_ref[...] = jnp.broadcast_to(cnt_scr[...], cnt_ref.shape)


def _mix(yo_p, yo_s, g_p, g_s, oa_p, oa_s, p_all, x_all, wo, n2, wr, br):
    nbp = ROWS_P // TM5
    rb = lambda i: (i, 0)
    pb = pl.BlockSpec((TM5, D), lambda i: (jnp.minimum(i, nbp - 1), 0))
    sb = pl.BlockSpec((TM5, D), lambda i: (jnp.maximum(i - nbp, 0), 0))
    full = lambda shp: pl.BlockSpec(shp, lambda i: tuple(0 for _ in shp))
    return pl.pallas_call(
        _mix_kernel,
        grid=(ROWS // TM5,),
        in_specs=[pl.BlockSpec((B_HEADS, B_HEAD, TM5),
                               lambda i: (jnp.minimum(i, nbp - 1) // (T_PAD // TM5), 0,
                                          jnp.minimum(i, nbp - 1) % (T_PAD // TM5))),
                  sb, pb, sb, pb, sb,
                  pl.BlockSpec((TM5, D), lambda i: (i, OFF_GA // D)),
                  pl.BlockSpec((TM5, D), lambda i: (i, OFF_GB // D)),
                  pl.BlockSpec((TM5, D), rb),
                  full((D, D)), full((1, D)), full((D, 128)), full((1, 128)), full((TM5, TM5))],
        out_specs=[pl.BlockSpec((TM5, D), rb), pl.BlockSpec((TM5, D), rb),
                   pl.BlockSpec((TM5, 128), rb), pl.BlockSpec((TM5, 128), rb),
                   pl.BlockSpec((TM5, 128), rb), full((8, 128))],
        out_shape=[jax.ShapeDtypeStruct((ROWS, D), F32), jax.ShapeDtypeStruct((ROWS, D), F32),
                   jax.ShapeDtypeStruct((ROWS, 128), jnp.int32), jax.ShapeDtypeStruct((ROWS, 128), F32),
                   jax.ShapeDtypeStruct((ROWS, 128), jnp.int32), jax.ShapeDtypeStruct((8, 128), F32)],
        scratch_shapes=[pltpu.VMEM((1, 128), F32)],
        compiler_params=_cparams(("arbitrary",)),
        name="mix",
    )(yo_p, yo_s, g_p, g_s, oa_p, oa_s, p_all, p_all, x_all, wo, n2, wr, br,
      jnp.asarray(np.tril(np.ones((TM5, TM5), np.float32), -1), BF16))


N_TOK = NB * (N_META + SEQ) + ROWS_S
NK = N_TOK * TOP_K
TMM = 256
N_BLK = -(-NK // TMM) + N_EXPERTS
NSLOT = N_BLK * TMM
TN_G1 = 1024
TN_G2 = 1024
TMC = 128


def _gather_kernel(slot_ref, nused_ref, h_hbm, o_ref, buf, sem):
    m = pl.program_id(0)
    cur = m % 2

    def issue(blk, slot):
        def row(r, _):
            tok = slot_ref[blk * TMM + r]
            pltpu.make_async_copy(h_hbm.at[pl.ds(tok, 1), :], buf.at[slot, pl.ds(r, 1), :], sem.at[slot]).start()
            return 0

        lax.fori_loop(0, TMM, row, 0, unroll=8)

    @pl.when((m == 0) & (m < nused_ref[0]))
    def _():
        issue(m, 0)

    @pl.when(m + 1 < nused_ref[0])
    def _():
        issue(m + 1, 1 - cur)

    @pl.when(m < nused_ref[0])
    def _():
        pltpu.make_async_copy(h_hbm.at[pl.ds(0, TMM), :], buf.at[cur], sem.at[cur]).wait()
        o_ref[...] = buf[cur].astype(BF16)

    @pl.when(m >= nused_ref[0])
    def _():
        o_ref[...] = jnp.zeros(o_ref.shape, BF16)


def _gather(slot_tok, nused, h2):
    return pl.pallas_call(
        _gather_kernel,
        grid_spec=pltpu.PrefetchScalarGridSpec(
            num_scalar_prefetch=2,
            grid=(N_BLK,),
            in_specs=[pl.BlockSpec(memory_space=pl.ANY)],
            out_specs=pl.BlockSpec((TMM, D), lambda m, st, nu: (m, 0)),
            scratch_shapes=[pltpu.VMEM((2, TMM, D), F32), pltpu.SemaphoreType.DMA((2,))]),
        out_shape=jax.ShapeDtypeStruct((NSLOT, D), BF16),
        compiler_params=_cparams(("arbitrary",)),
        name="moe_gather",
    )(slot_tok, nused, h2)


def _load_expert_weights(sched, w_hbm, wstage, wsem, wb_scr, tn):
    blk_e_ref, first_ref, runid_ref, nexte_ref, meta_ref = sched
    f, m = pl.program_id(0), pl.program_id(1)
    n_runs = meta_ref[1]

    def fetch(e, ff, slot):
        cols = pl.ds(pl.multiple_of(ff * tn, tn), tn)
        return pltpu.make_async_copy(w_hbm.at[e, :, cols], wstage.at[slot], wsem.at[slot])

    @pl.when((m < meta_ref[0]) & (first_ref[m] == 1))
    def _():
        r = runid_ref[m]
        g = f * n_runs + r
        slot = g % 2

        @pl.when(g == 0)
        def _():
            fetch(blk_e_ref[m], f, 0).start()

        nf = f + jnp.where(r == n_runs - 1, 1, 0)

        @pl.when(nf < pl.num_programs(0))
        def _():
            fetch(nexte_ref[m], nf, 1 - slot).start()

        fetch(blk_e_ref[m], f, slot).wait()
        wb_scr[...] = wstage[slot].astype(BF16)


def _gm1_kernel(blk_e_ref, first_ref, runid_ref, nexte_ref, meta_ref, x_ref, w_hbm, b_ref, sel_ref, o_ref,
                wb_scr, wstage, wsem):
    m = pl.program_id(1)
    used = m < meta_ref[0]
    _load_expert_weights((blk_e_ref, first_ref, runid_ref, nexte_ref, meta_ref), w_hbm, wstage, wsem, wb_scr, TN_G1)

    @pl.when(used)
    def _():
        hd = jnp.dot(x_ref[...], wb_scr[...], preferred_element_type=F32) + b_ref[...]
        glu = jnp.minimum(hd, SWIGLU_LIMIT)
        lin = jnp.clip(hd, -SWIGLU_LIMIT, SWIGLU_LIMIT) + 1.0
        act = glu * _sigmoid(SWIGLU_ALPHA * glu)
        for c in range(TN_G1 // 512):
            parts = []
            for q in range(4):
                sl = slice(c * 512 + q * 128, c * 512 + (q + 1) * 128)
                parts.append(act[:, sl] * pltpu.roll(lin[:, sl], 127, axis=1))
            z = jnp.concatenate(parts, axis=1).astype(BF16)
            o_ref[:, c * 256:(c + 1) * 256] = jnp.dot(z, sel_ref[...], preferred_element_type=F32).astype(BF16)

    @pl.when(jnp.logical_not(used))
    def _():
        o_ref[...] = jnp.zeros(o_ref.shape, BF16)


def _meff(m, meta):
    return jnp.maximum(jnp.minimum(m, meta[0] - 1), 0)


def _gm1(sched, xs, w1, b1, sel):
    return pl.pallas_call(
        _gm1_kernel,
        grid_spec=pltpu.PrefetchScalarGridSpec(
            num_scalar_prefetch=5,
            grid=(2 * D_FF // TN_G1, N_BLK),
            in_specs=[pl.BlockSpec((TMM, D), lambda f, m, be, fi, ri, ne, mt: (_meff(m, mt), 0)),
                      pl.BlockSpec(memory_space=pl.ANY),
                      pl.BlockSpec((None, 1, TN_G1), lambda f, m, be, fi, ri, ne, mt: (be[_meff(m, mt)], 0, f)),
                      pl.BlockSpec((512, 256), lambda f, m, be, fi, ri, ne, mt: (0, 0))],
            out_specs=pl.BlockSpec((TMM, TN_G1 // 2), lambda f, m, be, fi, ri, ne, mt: (m, f)),
            scratch_shapes=[pltpu.VMEM((D, TN_G1), BF16), pltpu.VMEM((2, D, TN_G1), F32),
                            pltpu.SemaphoreType.DMA((2,))]),
        out_shape=jax.ShapeDtypeStruct((NSLOT, D_FF), BF16),
        compiler_params=_cparams(("arbitrary", "arbitrary")),
        name="moe_up",
    )(*sched, xs, w1, b1, sel)


def _gm2_kernel(blk_e_ref, first_ref, runid_ref, nexte_ref, meta_ref, x_ref, w_hbm, b_ref, o_ref,
                wb_scr, wstage, wsem):
    m = pl.program_id(1)
    used = m < meta_ref[0]
    _load_expert_weights((blk_e_ref, first_ref, runid_ref, nexte_ref, meta_ref), w_hbm, wstage, wsem, wb_scr, TN_G2)

    @pl.when(used)
    def _():
        o_ref[...] = jnp.dot(x_ref[...], wb_scr[...], preferred_element_type=F32) + b_ref[...]

    @pl.when(jnp.logical_not(used))
    def _():
        o_ref[...] = jnp.zeros(o_ref.shape, F32)


def _gm2(sched, hid, w2, b2):
    return pl.pallas_call(
        _gm2_kernel,
        grid_spec=pltpu.PrefetchScalarGridSpec(
            num_scalar_prefetch=5,
            grid=(D // TN_G2, N_BLK),
            in_specs=[pl.BlockSpec((TMM, D_FF), lambda f, m, be, fi, ri, ne, mt: (_meff(m, mt), 0)),
                      pl.BlockSpec(memory_space=pl.ANY),
                      pl.BlockSpec((None, 1, TN_G2), lambda f, m, be, fi, ri, ne, mt: (be[_meff(m, mt)], 0, f))],
            out_specs=pl.BlockSpec((TMM, TN_G2), lambda f, m, be, fi, ri, ne, mt: (m, f)),
            scratch_shapes=[pltpu.VMEM((D_FF, TN_G2), BF16), pltpu.VMEM((2, D_FF, TN_G2), F32),
                            pltpu.SemaphoreType.DMA((2,))]),
        out_shape=jax.ShapeDtypeStruct((NSLOT, D), F32),
        compiler_params=_cparams(("arbitrary", "arbitrary")),
        name="moe_down",
    )(*sched, hid, w2, b2)


def _combine_kernel(dest_ref, x1_ref, tg_ref, fg_ref, y_hbm, yp_ref, ys_ref, buf, sem):
    m = pl.program_id(0)
    cur = m % 2

    def issue(blk, slot):
        def row(r, _):
            for k in range(TOP_K):
                d = dest_ref[(blk * TMC + r) * TOP_K + k]
                pltpu.make_async_copy(y_hbm.at[pl.ds(d, 1), :], buf.at[slot, pl.ds(k * TMC + r, 1), :],
                                      sem.at[slot]).start()
            return 0

        lax.fori_loop(0, TMC, row, 0, unroll=4)

    @pl.when(m == 0)
    def _():
        issue(m, 0)

    @pl.when(m + 1 < pl.num_programs(0))
    def _():
        issue(m + 1, 1 - cur)

    pltpu.make_async_copy(y_hbm.at[pl.ds(0, TOP_K * TMC), :], buf.at[cur], sem.at[cur]).wait()
    x2 = x1_ref[...]
    tg = tg_ref[...]
    for k in range(TOP_K):
        x2 = x2 + tg[:, k:k + 1] * buf[cur, k * TMC:(k + 1) * TMC, :]
    ms = jnp.mean(x2 * x2, axis=-1, keepdims=True)
    y = (x2 * lax.rsqrt(ms + RMS_EPS)) * fg_ref[...]

    @pl.when((m < ROWS_P // TMC) & (m % (T_PAD // TMC) >= CB_SKIP))
    def _():
        yp_ref[...] = y

    @pl.when(m >= ROWS_P // TMC)
    def _():
        ys_ref[...] = y


CB_SKIP = (FRONT + N_META) // TMC


def _combine(dest, x1, tg, fg, y):
    per_b = T_PAD // TMC

    def prompt_blk(m, d):
        mp = jnp.minimum(m, ROWS_P // TMC - 1)
        return ((mp // per_b) * (SEQ // TMC) + jnp.maximum(mp % per_b - CB_SKIP, 0), 0)

    return pl.pallas_call(
        _combine_kernel,
        grid_spec=pltpu.PrefetchScalarGridSpec(
            num_scalar_prefetch=1,
            grid=(ROWS // TMC,),
            in_specs=[pl.BlockSpec((TMC, D), lambda m, d: (m, 0)),
                      pl.BlockSpec((TMC, 128), lambda m, d: (m, 0)),
                      pl.BlockSpec((1, D), lambda m, d: (0, 0)),
                      pl.BlockSpec(memory_space=pl.ANY)],
            out_specs=[pl.BlockSpec((TMC, D), prompt_blk),
                       pl.BlockSpec((TMC, D), lambda m, d: (jnp.maximum(m - ROWS_P // TMC, 0), 0))],
            scratch_shapes=[pltpu.VMEM((2, TOP_K * TMC, D), F32), pltpu.SemaphoreType.DMA((2,))]),
        out_shape=[jax.ShapeDtypeStruct((NB * SEQ, D), F32), jax.ShapeDtypeStruct((ROWS_S, D), F32)],
        compiler_params=_cparams(("arbitrary",)),
        name="moe_combine",
    )(dest, x1, tg, fg, y)


def _t5_bucket(rel):
    nb = N_BUCKETS // 2
    ret = jnp.where(rel > 0, nb, 0)
    n = jnp.abs(rel)
    max_exact = nb // 2
    nf = jnp.maximum(n, 1).astype(jnp.float32)
    large = max_exact + (jnp.log(nf / max_exact) / math.log(MAX_DISTANCE / max_exact) * (nb - max_exact)).astype(jnp.int32)
    large = jnp.minimum(large, nb - 1)
    return ret + jnp.where(n < max_exact, n, large)


def _valid_rows():
    rows = [np.arange(b * T_PAD + FRONT, (b + 1) * T_PAD) for b in range(NB)]
    rows.append(np.arange(ROWS_P, ROWS))
    return np.concatenate(rows).astype(np.int32)


TR_KV = math.gcd(FRONT, N_META + SEQ)


def _kv_out_kernel(k_ref, v_ref, ko_ref, vo_ref):
    for h in range(A_HEADS):
        ko_ref[:, h, :] = k_ref[:, h * A_VD:(h + 1) * A_VD]
        vo_ref[:, h, :] = v_ref[:, h * A_VD:(h + 1) * A_VD]


def _kv_out(p_all):
    per_b = T_PAD // TR_KV
    src = lambda off: pl.BlockSpec((TR_KV, D), lambda b, i: (b * per_b + FRONT // TR_KV + i, off // D))
    dst = pl.BlockSpec((None, TR_KV, A_HEADS, A_VD), lambda b, i: (b, i, 0, 0))
    out = jax.ShapeDtypeStruct((NB, N_META + SEQ, A_HEADS, A_VD), F32)
    return pl.pallas_call(
        _kv_out_kernel,
        grid=(NB, (N_META + SEQ) // TR_KV),
        in_specs=[src(OFF_K), src(OFF_V)],
        out_specs=[dst, dst],
        out_shape=[out, out],
        compiler_params=_cparams(("arbitrary", "arbitrary")),
        name="kv_out",
    )(p_all, p_all)


def _slot_tables(te, rank, cnt):
    counts = cnt[0, :N_EXPERTS].astype(jnp.int32)
    padded = (counts + TMM - 1) // TMM * TMM
    pad_end = jnp.cumsum(padded)
    pad_start = pad_end - padded
    experts = jnp.arange(N_EXPERTS, dtype=jnp.int32)
    routed = jnp.asarray(np.isin(np.arange(ROWS), _valid_rows()))[:, None]
    start_of = jnp.sum(jnp.where(te[:, :TOP_K, None] == experts, pad_start, 0), axis=-1)
    dest2 = start_of + rank[:, :TOP_K]
    dest = jnp.where(routed, dest2, 0).reshape(-1)
    rows = jnp.broadcast_to(jnp.arange(ROWS, dtype=jnp.int32)[:, None], (ROWS, TOP_K))
    slot_tok = jnp.zeros((NSLOT,), jnp.int32).at[jnp.where(routed, dest2, NSLOT).reshape(-1)].set(
        rows.reshape(-1), mode='drop')
    blk_start = jnp.arange(N_BLK, dtype=jnp.int32) * TMM
    blk_e = jnp.minimum(jnp.sum(pad_end[None, :] <= blk_start[:, None], axis=-1), N_EXPERTS - 1).astype(jnp.int32)
    nused = (pad_end[-1:] // TMM).astype(jnp.int32)
    blk = jnp.arange(N_BLK, dtype=jnp.int32)
    first = (blk < nused[0]) & ((blk == 0) | (blk_e != jnp.roll(blk_e, 1)))
    run_id = jnp.cumsum(first.astype(jnp.int32)) - 1
    n_runs = jnp.sum(first.astype(jnp.int32))
    run_e = jnp.zeros((N_BLK,), jnp.int32).at[jnp.where(first, run_id, N_BLK)].set(blk_e, mode='drop')
    next_e = run_e[(run_id + 1) % jnp.maximum(n_runs, 1)]
    sched = (blk_e, first.astype(jnp.int32), run_id, next_e, jnp.stack([nused[0], n_runs]))
    return dest, slot_tok, nused, sched


def kernel(x_prompt, x_sample, cache_k, cache_v, state_wkv, state_shift, meta_tokens, rel_bias_table, norm1_g, w_in, shift_mu, decay_base, w_decay_up, a_base, w_a_up, w_g_up, k_k, k_a, r_k, gn_w, gn_b, lam_q1, lam_k1, lam_q2, lam_k2, subln_g, w_out, norm2_g, w_router, b_router, w_e1, b_e1, w_e2, b_e2, final_g):
    xp = jnp.concatenate([jnp.zeros((NB, FRONT, D), F32),
                          jnp.broadcast_to(meta_tokens[None], (NB, N_META, D)), x_prompt], axis=1)
    x_all = jnp.concatenate([xp.reshape(ROWS_P, D), x_sample.reshape(ROWS_S, D)], axis=0)
    lora0 = OFF_R + 3 * D
    w_main = w_in[0].astype(BF16)
    w_tail = jnp.concatenate([w_main[:, lora0 + LORA_W:], w_main[:, lora0:lora0 + LORA_W],
                              jnp.zeros((D, LORA_P - LORA_W), BF16)], axis=1)

    p_all = _inproj(x_all, norm1_g, w_main, w_tail)

    kq = jnp.arange(TQ, dtype=jnp.int32)
    bkt = jnp.stack([_t5_bucket(kq[:, None] - kq[None, :]), _t5_bucket(kq[:, None] - kq[None, :] - TQ)])
    lamp = jnp.concatenate([lam_q1, lam_k1, lam_q2, lam_k2], axis=0)
    oa_p = _attn_prompt(p_all, rel_bias_table, bkt, lamp, subln_g)
    qpos = PAST + jnp.arange(DEC_S, dtype=jnp.int32)
    bkt_last = _t5_bucket((PAST - NEAR + jnp.arange(NEAR, dtype=jnp.int32))[None, :] - qpos[:, None])
    bkt_new = _t5_bucket(qpos[None, :] - qpos[:, None])
    oa_s = _attn_sample(p_all, cache_k[0], cache_v[0],
                        rel_bias_table, bkt_last, bkt_new, lamp, subln_g)

    mu = shift_mu[0]
    mu3 = mu[:3 * D][None]
    mul = jnp.pad(mu[3 * D:], (0, LORA_P - LORA_W))[None]
    wd = jnp.zeros((LORA_P, D), F32).at[:DECAY_LORA].set(w_decay_up[0]).astype(BF16)
    wa = jnp.zeros((LORA_P, D), F32).at[DECAY_LORA:DECAY_LORA + AAA_LORA].set(w_a_up[0]).astype(BF16)
    wg = jnp.zeros((LORA_P, D), F32).at[DECAY_LORA + AAA_LORA:LORA_W].set(w_g_up[0]).astype(BF16)
    first_p = jnp.zeros((NB, 1, 3 * D + LORA_P), F32)
    first_s = jnp.pad(state_shift[0], ((0, 0), (0, LORA_P - LORA_W)))[:, None, :]
    prep_p = _rwkv_prep(p_all, first_p, mu3, mul, decay_base, a_base, wd, wa, wg, NB, T_PAD // 256, 256, 0, True)
    prep_s = _rwkv_prep(p_all, first_s, mu3, mul, decay_base, a_base, wd, wa, wg, DEC_B, 1, DEC_S, ROWS_P // DEC_S,
                        False)

    def chain_tile(v):
        return jnp.tile(v.reshape(B_HEADS, B_HEAD).T, (1, 128 // B_HEADS))

    par = [chain_tile(v.reshape(-1)) for v in (k_k[0], k_a[0], r_k[0], gn_w[0], gn_b[0])]
    s0_p = jnp.zeros((B_HEAD, B_HEAD, NB * B_HEADS), F32)
    s0_s = state_wkv[0].transpose(3, 2, 0, 1).reshape(B_HEAD, B_HEAD, DEC_B * B_HEADS)
    yo_p, sfin_p = _rwkv_rec(*[a.transpose(2, 1, 0) for a in prep_p[:5]], *par, s0_p, skip=FRONT // TB)
    yo_s, sfin_s = _rwkv_rec(*[_to_chains(a, DEC_B, DEC_S) for a in prep_s[:5]], *par, s0_s)

    wo = w_out[0].astype(BF16)
    wr = jnp.pad(w_router[0], ((0, 0), (0, 128 - N_EXPERTS)))
    br = jnp.concatenate([b_router[0], jnp.full((128 - N_EXPERTS,), NEG, F32)])[None]
    x1, h2, te, tg, rank, cnt = _mix(yo_p.transpose(2, 1, 0), _from_chains(yo_s, DEC_B, DEC_S), prep_p[5], prep_s[5],
                                     oa_p, oa_s, p_all, x_all, wo, norm2_g, wr, br)

    dest, slot_tok, nused, sched = _slot_tables(te, rank, cnt)

    sel_np = np.zeros((512, 256), np.float32)
    sel_np[2 * np.arange(256), np.arange(256)] = 1.0
    xs = _gather(slot_tok, nused, h2)
    hid = _gm1(sched, xs, w_e1[0], b_e1[0][:, None, :], jnp.asarray(sel_np, BF16))
    ys = _gm2(sched, hid, w_e2[0], b_e2[0][:, None, :])
    y_prompt, y_sample = _combine(dest, x1, tg, final_g[None], ys)

    y_prompt = y_prompt.reshape(NB, SEQ, D)
    y_sample = y_sample.reshape(DEC_B, DEC_S, D)
    def last_ps(last):
        return last[:, 0, :SHIFT_W][None]

    kp, vp = _kv_out(p_all)
    k_prompt, v_prompt = kp[None], vp[None]
    k_sample = p_all[ROWS_P:, OFF_K:OFF_K + D].reshape(1, DEC_B, DEC_S, A_HEADS, A_VD)
    v_sample = p_all[ROWS_P:, OFF_V:OFF_V + D].reshape(1, DEC_B, DEC_S, A_HEADS, A_VD)
    shift_prompt = last_ps(prep_p[6])
    shift_sample = last_ps(prep_s[6])
    wkv_prompt = sfin_p.reshape(B_HEAD, B_HEAD, NB, B_HEADS).transpose(2, 3, 1, 0)[None]
    wkv_sample = sfin_s.reshape(B_HEAD, B_HEAD, DEC_B, B_HEADS).transpose(2, 3, 1, 0)[None].astype(state_wkv.dtype)
    return (y_prompt, y_sample, k_prompt, v_prompt, wkv_prompt, shift_prompt,
            k_sample, v_sample, wkv_sample, shift_sample)
```

```python
import functools
import math

import numpy as np
import jax
import jax.numpy as jnp
from jax import lax
from jax.experimental import pallas as pl
from jax.experimental.pallas import tpu as pltpu

F32 = jnp.float32
BF16 = jnp.bfloat16

D = 2048
NB = 4
SEQ = 2048
DEC_B = 16
DEC_S = 32
PAST = 2048
CHUNK = 64
N_META = 16
RMS_EPS = 1e-5
A_DH = 64
A_HEADS = 16
A_VD = 128
B_HEAD = 64
B_HEADS = 32
DECAY_LORA = 96
AAA_LORA = 96
GATE_LORA = 256
GN_EPS = 64e-5
N_BUCKETS = 32
MAX_DISTANCE = 128
N_EXPERTS = 32
TOP_K = 4
D_FF = 2048
SWIGLU_ALPHA = 1.702
SWIGLU_LIMIT = 7.0
LAM_INIT = 0.8 - 0.6 * math.exp(-0.3 * 0)
SHIFT_W = 3 * D + DECAY_LORA + AAA_LORA + GATE_LORA
LORA_W = DECAY_LORA + AAA_LORA + GATE_LORA
LORA_P = 512

T_PAD = 2304
FRONT = T_PAD - N_META - SEQ
ROWS_P = NB * T_PAD
ROWS_S = DEC_B * DEC_S
ROWS = ROWS_P + ROWS_S

OFF_Q = 0
OFF_K = 2048
OFF_V = 4096
OFF_R = 6144
OFF_KX = 8192
OFF_VX = 10240
OFF_GA = 12288
OFF_GB = 14336
OFF_LORA = 16384
PW = OFF_LORA + LORA_P

NEG = -1e30
LOG2E = 1.4426950408889634
VMEM_LIMIT = 56 * 1024 * 1024


def _cparams(sem, vmem=VMEM_LIMIT):
    return pltpu.CompilerParams(dimension_semantics=sem, vmem_limit_bytes=vmem)


def _sigmoid(x):
    return 1.0 / (1.0 + jnp.exp(-x))


TM1 = 512
TN1 = 1536
N_MAIN = OFF_GA // TN1


def _inproj_kernel(x_ref, g_ref, wm_ref, wt_ref, o_ref, h_scr):
    j = pl.program_id(1)

    @pl.when(j == 0)
    def _():
        x = x_ref[...]
        ms = jnp.mean(x * x, axis=-1, keepdims=True)
        h_scr[...] = ((x * lax.rsqrt(ms + RMS_EPS)) * g_ref[...]).astype(BF16)

    @pl.when(j < N_MAIN)
    def _():
        o_ref[...] = jnp.dot(h_scr[...], wm_ref[...], preferred_element_type=F32)

    @pl.when(j >= N_MAIN)
    def _():
        o_ref[...] = jnp.dot(h_scr[...], wt_ref[...], preferred_element_type=F32)


def _inproj(x_all, g, w_main, w_tail):
    return pl.pallas_call(
        _inproj_kernel,
        grid=(ROWS // TM1, PW // TN1),
        in_specs=[pl.BlockSpec((TM1, D), lambda i, j: (i, 0)),
                  pl.BlockSpec((1, D), lambda i, j: (0, 0)),
                  pl.BlockSpec((D, TN1), lambda i, j: (0, jnp.minimum(j, N_MAIN - 1))),
                  pl.BlockSpec((D, TN1), lambda i, j: (0, jnp.maximum(j - N_MAIN, 0)))],
        out_specs=pl.BlockSpec((TM1, TN1), lambda i, j: (i, j)),
        out_shape=jax.ShapeDtypeStruct((ROWS, PW), F32),
        scratch_shapes=[pltpu.VMEM((TM1, D), BF16)],
        compiler_params=_cparams(("arbitrary", "arbitrary")),
        name="inproj",
    )(x_all, g, w_main, w_tail)


def _lam_value(lamp_ref):
    lp = lamp_ref[...]
    s1 = jnp.sum(lp[0:1, :] * lp[1:2, :], axis=-1, keepdims=True)
    s2 = jnp.sum(lp[2:3, :] * lp[3:4, :], axis=-1, keepdims=True)
    return jnp.exp(s1) - jnp.exp(s2) + LAM_INIT


def _bias_from_buckets(bkt, table_ref, h):
    out = jnp.zeros(bkt.shape, F32)
    for n in range(N_BUCKETS):
        out = jnp.where(bkt == n, table_ref[n, h], out)
    return out


TQ = 256
NQB = T_PAD // TQ
BIAS_META, BIAS_DIAG, BIAS_SUB, BIAS_SUB_PAD, BIAS_FAR_PAD, BIAS_FAR = range(6)


def _attn_prompt_kernel(table_ref, q_ref, k_ref, v_ref, bkt_ref, lamp_ref, subg_ref, o_ref,
                        kb_scr, vt_scr, bias_scr, s_scr, acc_scr):
    h = pl.program_id(0)
    b = pl.program_id(1)

    @pl.when(b == 0)
    def _():
        kk = lax.broadcasted_iota(jnp.int32, (TQ, TQ), 0)
        qq = lax.broadcasted_iota(jnp.int32, (TQ, TQ), 1)
        chunk_ok = (kk // CHUNK) <= (qq // CHUNK)
        pad = jnp.where(kk < FRONT, NEG, 0.0).astype(F32)
        b_diag = _bias_from_buckets(bkt_ref[0], table_ref, h)
        b_sub = _bias_from_buckets(bkt_ref[1], table_ref, h)
        b_far = jnp.zeros((TQ, TQ), F32) + table_ref[N_BUCKETS // 2 - 1, h]
        tiles = (jnp.where(chunk_ok, b_diag, NEG) + pad,
                 jnp.where(chunk_ok, b_diag, NEG),
                 b_sub,
                 b_sub + pad,
                 b_far + pad,
                 b_far)
        for i, t in enumerate(tiles):
            bias_scr[i] = jnp.concatenate([t, t], axis=1) * LOG2E

    kb_scr[...] = k_ref[...].astype(BF16)
    for i in range(NQB):
        vt_scr[i] = v_ref[i * TQ:(i + 1) * TQ, :].T.astype(BF16)

    lam = _lam_value(lamp_ref)
    drow = lax.broadcasted_iota(jnp.int32, (A_VD, TQ), 0)

    def q_operand(qi):
        q0 = pl.multiple_of(qi * TQ, TQ)
        qt = (q_ref[pl.ds(q0, TQ), :] * (A_DH ** -0.5 * LOG2E)).T
        q1 = jnp.where(drow < A_DH, qt, 0.0)
        q2 = jnp.where(drow >= A_DH, qt, 0.0)
        return jnp.concatenate([q1, q2], axis=1).astype(BF16)

    def scores(kj, qpt, slot):
        k0 = pl.multiple_of(kj * TQ, TQ)
        s_scr[slot] = jnp.dot(kb_scr[pl.ds(k0, TQ), :], qpt, preferred_element_type=F32)

    def accumulate(kj, bias_id, slot, ml):
        m, l = ml
        s = s_scr[slot] + bias_scr[bias_id]
        m_new = jnp.maximum(m, jnp.max(s, axis=0, keepdims=True))
        alpha = jnp.exp2(m - m_new)
        p = jnp.exp2(s - m_new)
        l = l * alpha + jnp.sum(p, axis=0, keepdims=True)
        acc_scr[...] = acc_scr[...] * alpha + jnp.dot(vt_scr[kj], p.astype(BF16), preferred_element_type=F32)
        return m_new, l

    def finish(qi, ml):
        on = acc_scr[...] / ml[1]
        ot = on[:, :TQ] - lam * on[:, TQ:]
        ms = jnp.mean(ot * ot, axis=0, keepdims=True)
        o = (ot * lax.rsqrt(ms + RMS_EPS)).T * subg_ref[...] * (1.0 - LAM_INIT)
        q0 = pl.multiple_of(qi * TQ, TQ)
        o_ref[pl.ds(q0, TQ), :] = o

    def start(qi):
        qpt = q_operand(qi)
        acc_scr[...] = jnp.zeros(acc_scr.shape, F32)
        scores(0, qpt, 0)
        return qpt, (jnp.full((1, 2 * TQ), NEG, F32), jnp.zeros((1, 2 * TQ), F32))

    finish(0, accumulate(0, BIAS_META, 0, start(0)[1]))

    def q_block(qi, _):
        qpt, ml = start(qi)

        def half(kj, cur, ml):
            scores(kj + 1, qpt, 1 - cur)
            first = kj == 0
            bias_id = jnp.where(kj == qi - 1, jnp.where(first, BIAS_SUB_PAD, BIAS_SUB),
                                jnp.where(first, BIAS_FAR_PAD, BIAS_FAR))
            return accumulate(kj, bias_id, cur, ml)

        ml = lax.fori_loop(0, qi // 2, lambda i, ml: half(2 * i + 1, 1, half(2 * i, 0, ml)), ml)
        ml = lax.cond(qi % 2 == 1,
                      lambda ml: accumulate(qi, BIAS_DIAG, 1, half(qi - 1, 0, ml)),
                      lambda ml: accumulate(qi, BIAS_DIAG, 0, ml), ml)
        finish(qi, ml)
        return 0

    lax.fori_loop(1, NQB, q_block, 0)


def _attn_prompt(p_all, table, bkt, lamp, subg):
    cb = D // A_VD
    return pl.pallas_call(
        _attn_prompt_kernel,
        grid_spec=pltpu.PrefetchScalarGridSpec(
            num_scalar_prefetch=0,
            grid=(A_HEADS, NB),
            in_specs=[pl.BlockSpec(memory_space=pltpu.SMEM),
                      pl.BlockSpec((T_PAD, A_VD), lambda h, b: (b, h)),
                      pl.BlockSpec((T_PAD, A_VD), lambda h, b: (b, cb + h)),
                      pl.BlockSpec((T_PAD, A_VD), lambda h, b: (b, 2 * cb + h)),
                      pl.BlockSpec((2, TQ, TQ), lambda h, b: (0, 0, 0)),
                      pl.BlockSpec((4, A_DH), lambda h, b: (0, 0)),
                      pl.BlockSpec((1, A_VD), lambda h, b: (0, 0))],
            out_specs=pl.BlockSpec((T_PAD, A_VD), lambda h, b: (b, h)),
            scratch_shapes=[pltpu.VMEM((T_PAD, A_VD), BF16),
                            pltpu.VMEM((NQB, A_VD, TQ), BF16),
                            pltpu.VMEM((6, TQ, 2 * TQ), F32),
                            pltpu.VMEM((2, TQ, 2 * TQ), F32),
                            pltpu.VMEM((A_VD, 2 * TQ), F32)]),
        out_shape=jax.ShapeDtypeStruct((ROWS_P, D), F32),
        compiler_params=_cparams(("arbitrary", "arbitrary")),
        name="attn_prompt",
    )(table, p_all, p_all, p_all, bkt, lamp, subg)


HG = 4
NEAR = 512


def _attn_sample_kernel(table_ref, q_ref, kn_ref, vn_ref, bl_ref, bn_ref, lamp_ref, subg_ref, ck_hbm, cv_hbm,
                        o_ref, kbuf, vbuf, sem):
    hg = pl.program_id(1)
    ng = pl.num_programs(1)
    step = pl.program_id(0) * ng + hg
    nsteps = pl.num_programs(0) * ng

    def copies(n, slot):
        out = []
        for i in range(HG):
            h = (n % ng) * HG + i
            out.append(pltpu.make_async_copy(ck_hbm.at[n // ng, :, h, :], kbuf.at[slot, i], sem.at[slot, i]))
            out.append(pltpu.make_async_copy(cv_hbm.at[n // ng, :, h, :], vbuf.at[slot, i], sem.at[slot, HG + i]))
        return out

    cur = step % 2

    @pl.when(step == 0)
    def _():
        for c in copies(step, 0):
            c.start()

    @pl.when(step + 1 < nsteps)
    def _():
        for c in copies(step + 1, 1 - cur):
            c.start()

    for c in copies(step, cur):
        c.wait()
    ck_refs = [kbuf.at[cur, i] for i in range(HG)]
    cv_refs = [vbuf.at[cur, i] for i in range(HG)]
    lam = _lam_value(lamp_ref)
    lane = lax.broadcasted_iota(jnp.int32, (DEC_S, A_VD), 1)
    nt = (((1,), (1,)), ((), ()))
    for i in range(HG):
        h = hg * HG + i
        cols = slice(i * A_VD, (i + 1) * A_VD)
        q = q_ref[:, cols] * (A_DH ** -0.5)
        qp = jnp.concatenate([jnp.where(lane < A_DH, q, 0.0), jnp.where(lane >= A_DH, q, 0.0)],
                             axis=0).astype(BF16)
        far = jnp.zeros((DEC_S, PAST - NEAR), F32) + table_ref[N_BUCKETS // 2 - 1, h]
        bl = jnp.concatenate([far, _bias_from_buckets(bl_ref[...], table_ref, h)], axis=1)
        bn = _bias_from_buckets(bn_ref[...], table_ref, h)
        s = (lax.dot_general(qp, ck_refs[i][...].astype(BF16), nt, preferred_element_type=F32)
             + jnp.concatenate([bl, bl], axis=0))
        sn = (lax.dot_general(qp, kn_ref[:, cols].astype(BF16), nt, preferred_element_type=F32)
              + jnp.concatenate([bn, bn], axis=0))
        m = jnp.maximum(jnp.max(s, axis=-1, keepdims=True), jnp.max(sn, axis=-1, keepdims=True))
        p = jnp.exp(s - m)
        pn = jnp.exp(sn - m)
        l = jnp.sum(p, axis=-1, keepdims=True) + jnp.sum(pn, axis=-1, keepdims=True)
        acc = (jnp.dot(p.astype(BF16), cv_refs[i][...].astype(BF16), preferred_element_type=F32)
               + jnp.dot(pn.astype(BF16), vn_ref[:, cols].astype(BF16), preferred_element_type=F32))
        on = acc / l
        o = on[:DEC_S] - lam * on[DEC_S:]
        ms = jnp.mean(o * o, axis=-1, keepdims=True)
        o_ref[:, cols] = (o * lax.rsqrt(ms + RMS_EPS)) * subg_ref[...] * (1.0 - LAM_INIT)


def _attn_sample(p_all, cache_k, cache_v, table, bkt_last, bkt_new, lamp, subg):
    r0 = ROWS_P // DEC_S
    w = HG * A_VD
    new = lambda off: pl.BlockSpec((DEC_S, w), lambda b, g: (r0 + b, off // w + g))
    return pl.pallas_call(
        _attn_sample_kernel,
        grid=(DEC_B, A_HEADS // HG),
        in_specs=[pl.BlockSpec(memory_space=pltpu.SMEM), new(OFF_Q), new(OFF_K), new(OFF_V),
                  pl.BlockSpec((DEC_S, NEAR), lambda b, g: (0, 0)),
                  pl.BlockSpec((DEC_S, DEC_S), lambda b, g: (0, 0)),
                  pl.BlockSpec((4, A_DH), lambda b, g: (0, 0)),
                  pl.BlockSpec((1, A_VD), lambda b, g: (0, 0)),
                  pl.BlockSpec(memory_space=pl.ANY), pl.BlockSpec(memory_space=pl.ANY)],
        out_specs=pl.BlockSpec((DEC_S, w), lambda b, g: (b, g)),
        out_shape=jax.ShapeDtypeStruct((ROWS_S, D), F32),
        scratch_shapes=[pltpu.VMEM((2, HG, PAST, A_VD), F32), pltpu.VMEM((2, HG, PAST, A_VD), F32),
                        pltpu.SemaphoreType.DMA((2, 2 * HG))],
        compiler_params=_cparams(("arbitrary", "arbitrary")),
        name="attn_sample",
    )(table, p_all, p_all, p_all, bkt_last, bkt_new, lamp, subg, cache_k, cache_v)


def _rwkv_prep_kernel(chain_out, r_ref, k_ref, v_ref, lo_ref, first_ref, mu_ref, mul_ref, dbase_ref, abase_ref,
                      wd_ref, wa_ref, wg_ref,
                      xr_ref, xk_ref, xv_ref, wp_ref, ap_ref, g_ref, last_ref, carry_scr, carryl_scr):
    i = pl.program_id(1)
    tm = r_ref.shape[0]
    row = lax.broadcasted_iota(jnp.int32, (tm, 1), 0)

    def shift(x, first, mu):
        prev = jnp.where(row == 0, first, pltpu.roll(x, 1, axis=0))
        return x + (prev - x) * mu

    def put(dst, val):
        if chain_out:
            for hp in range(B_HEADS // 2):
                t = val[:, hp * 128:(hp + 1) * 128].T
                dst[2 * hp] = t[:B_HEAD]
                dst[2 * hp + 1] = t[B_HEAD:]
        else:
            dst[...] = val

    for s, (src, dst) in enumerate(((r_ref, xr_ref), (k_ref, xk_ref), (v_ref, xv_ref))):
        x = src[...]
        first = jnp.where(i == 0, first_ref[:, s * D:(s + 1) * D], carry_scr[:, s * D:(s + 1) * D])
        put(dst, shift(x, first, mu_ref[:, s * D:(s + 1) * D]))
        carry_scr[:, s * D:(s + 1) * D] = x[tm - 1:tm, :]
        last_ref[:, s * D:(s + 1) * D] = x[tm - 1:tm, :]

    xl = lo_ref[...]
    first = jnp.where(i == 0, first_ref[:, 3 * D:], carryl_scr[...])
    xs = shift(xl, first, mul_ref[...])
    carryl_scr[...] = xl[tm - 1:tm, :]
    last_ref[:, 3 * D:] = xl[tm - 1:tm, :]
    put(wp_ref, dbase_ref[...] + jnp.dot(jnp.tanh(xs).astype(BF16), wd_ref[...], preferred_element_type=F32))
    put(ap_ref, abase_ref[...] + jnp.dot(xs.astype(BF16), wa_ref[...], preferred_element_type=F32))
    g_ref[...] = jnp.dot(_sigmoid(xs).astype(BF16), wg_ref[...], preferred_element_type=F32)


def _rwkv_prep(p_all, first, mu, mul, dbase, abase, wd, wa, wg, nseq, nblk, tm, row_blk0, chain_out):
    rows = nseq * nblk * tm
    cs = lambda off: (lambda s, i: (row_blk0 + s * nblk + i, off))
    full = lambda shp: pl.BlockSpec(shp, lambda s, i: tuple(0 for _ in shp))
    out = jax.ShapeDtypeStruct((rows, D), F32)
    ospec = pl.BlockSpec((tm, D), lambda s, i: (s * nblk + i, 0))
    if chain_out:
        cout = jax.ShapeDtypeStruct((nseq * B_HEADS, B_HEAD, nblk * tm), F32)
        cspec = pl.BlockSpec((B_HEADS, B_HEAD, tm), lambda s, i: (s, 0, i))
    else:
        cout, cspec = out, ospec
    return pl.pallas_call(
        functools.partial(_rwkv_prep_kernel, chain_out),
        grid=(nseq, nblk),
        in_specs=[pl.BlockSpec((tm, D), cs(OFF_R // D)),
                  pl.BlockSpec((tm, D), cs(OFF_KX // D)),
                  pl.BlockSpec((tm, D), cs(OFF_VX // D)),
                  pl.BlockSpec((tm, LORA_P), cs(OFF_LORA // LORA_P)),
                  pl.BlockSpec((None, 1, 3 * D + LORA_P), lambda s, i: (s, 0, 0)),
                  full((1, 3 * D)), full((1, LORA_P)), full((1, D)), full((1, D)),
                  full((LORA_P, D)), full((LORA_P, D)), full((LORA_P, D))],
        out_specs=[cspec] * 5 + [ospec, pl.BlockSpec((None, 1, 3 * D + LORA_P), lambda s, i: (s, 0, 0))],
        out_shape=[cout] * 5 + [out, jax.ShapeDtypeStruct((nseq, 1, 3 * D + LORA_P), F32)],
        scratch_shapes=[pltpu.VMEM((1, 3 * D), F32), pltpu.VMEM((1, LORA_P), F32)],
        compiler_params=_cparams(("arbitrary", "arbitrary")),
        name="rwkv_prep",
    )(p_all, p_all, p_all, p_all, first, mu, mul, dbase, abase, wd, wa, wg)


TB = 32
IH = 32


def _rwkv_rec_kernel(skip, *refs):
    tb = pl.program_id(1)
    s0_ref, o_ref, sfin_ref, S = refs[10], refs[11], refs[12], refs[13]

    @pl.when(tb == 0)
    def _():
        S[...] = s0_ref[...]

    if skip:
        @pl.when(tb < skip)
        def _():
            o_ref[...] = jnp.zeros(o_ref.shape, F32)

        pl.when(tb >= skip)(functools.partial(_rwkv_rec_block, *refs))
    else:
        _rwkv_rec_block(*refs)
    sfin_ref[...] = S[...]


def _rwkv_rec_block(xr_ref, xk_ref, xv_ref, wp_ref, ap_ref, kk_ref, ka_ref, rk_ref, gnw_ref, gnb_ref, s0_ref,
                    o_ref, sfin_ref, S, w_s, k_s, a_s, b_s, sa_s, y_s):
    z = -wp_ref[...]
    softplus = jnp.maximum(z, 0.0) + jnp.log(1.0 + jnp.exp(-jnp.abs(z)))
    w_s[...] = jnp.exp(-jnp.exp(-softplus - 0.5))
    a = _sigmoid(ap_ref[...])
    xk = xk_ref[...]
    kk = xk * kk_ref[...]
    nrm = jnp.sqrt(jnp.sum(kk * kk, axis=1, keepdims=True))
    kk = kk / jnp.maximum(nrm, 1e-12)
    k_s[...] = xk * (1.0 + (a - 1.0) * ka_ref[...])
    a_s[...] = -kk
    b_s[...] = kk * a

    for hf in range(B_HEAD // IH):
        i0 = hf * IH
        acc = jnp.zeros((IH, 128), F32)
        for j in range(B_HEAD):
            acc = acc + S[j, i0:i0 + IH, :] * a_s[0, j:j + 1, :]
        sa_s[0, i0:i0 + IH, :] = acc

    def token(t, _):
        cur = t % 2
        tn = jnp.minimum(t + 1, TB - 1)
        for hf in range(B_HEAD // IH):
            i0 = hf * IH
            sa = sa_s[cur, i0:i0 + IH, :]
            v = xv_ref[t, i0:i0 + IH, :]
            y = jnp.zeros((IH, 128), F32)
            san = jnp.zeros((IH, 128), F32)
            for j in range(B_HEAD):
                s_new = (S[j, i0:i0 + IH, :] * w_s[t, j:j + 1, :] + sa * b_s[t, j:j + 1, :]
                         + v * k_s[t, j:j + 1, :])
                S[j, i0:i0 + IH, :] = s_new
                y = y + s_new * xr_ref[t, j:j + 1, :]
                san = san + s_new * a_s[tn, j:j + 1, :]
            y_s[i0:i0 + IH, :] = y
            sa_s[1 - cur, i0:i0 + IH, :] = san
        y = y_s[...]
        mu = jnp.mean(y, axis=0, keepdims=True)
        yc = y - mu
        var = jnp.mean(yc * yc, axis=0, keepdims=True)
        bonus = jnp.sum(xr_ref[t] * k_s[t] * rk_ref[...], axis=0, keepdims=True)
        o_ref[t] = yc * lax.rsqrt(var + GN_EPS) * gnw_ref[...] + gnb_ref[...] + bonus * xv_ref[t]
        return 0

    lax.fori_loop(0, TB, token, 0)


def _rwkv_rec(xr, xk, xv, wp, ap, kk, ka, rk, gnw, gnb, s0, skip=0):
    t, _, c = xr.shape
    tok = pl.BlockSpec((TB, B_HEAD, 128), lambda g, tb: (tb, 0, g))
    par = pl.BlockSpec((B_HEAD, 128), lambda g, tb: (0, 0))
    st = pl.BlockSpec((B_HEAD, B_HEAD, 128), lambda g, tb: (0, 0, g))
    blk = pltpu.VMEM((TB, B_HEAD, 128), F32)
    return pl.pallas_call(
        functools.partial(_rwkv_rec_kernel, skip),
        grid=(c // 128, t // TB),
        in_specs=[tok] * 5 + [par] * 5 + [st],
        out_specs=[tok, st],
        out_shape=[jax.ShapeDtypeStruct((t, B_HEAD, c), F32), jax.ShapeDtypeStruct((B_HEAD, B_HEAD, c), F32)],
        scratch_shapes=[pltpu.VMEM((B_HEAD, B_HEAD, 128), F32), blk, blk, blk, blk,
                        pltpu.VMEM((2, B_HEAD, 128), F32), pltpu.VMEM((B_HEAD, 128), F32)],
        compiler_params=_cparams(("arbitrary", "arbitrary")),
        name="rwkv_rec",
    )(xr, xk, xv, wp, ap, kk, ka, rk, gnw, gnb, s0)


def _to_chains(a, nb, t):
    return a.reshape(nb, t, B_HEADS, B_HEAD).transpose(1, 3, 0, 2).reshape(t, B_HEAD, nb * B_HEADS)


def _from_chains(a, nb, t):
    return a.reshape(t, B_HEAD, nb, B_HEADS).transpose(2, 0, 3, 1).reshape(nb * t, D)


TM5 = 256


def _mix_kernel(yo_p_ref, yo_s_ref, g_p_ref, g_s_ref, oa_p_ref, oa_s_ref, ga_ref, gb_ref, x_ref,
                wo_ref, n2_ref, wr_ref, br_ref, ltri_ref, x1_ref, h2_ref, te_ref, tg_ref, rank_ref, cnt_ref, cnt_scr):
    i = pl.program_id(0)
    is_p = i < ROWS_P // TM5
    tiles = [jnp.concatenate([yo_p_ref[2 * hp], yo_p_ref[2 * hp + 1]], axis=0).T for hp in range(B_HEADS // 2)]
    yo = jnp.where(is_p, jnp.concatenate(tiles, axis=1), yo_s_ref[...])
    g = jnp.where(is_p, g_p_ref[...], g_s_ref[...])
    oa = jnp.where(is_p, oa_p_ref[...], oa_s_ref[...])
    mixed = _sigmoid(ga_ref[...]) * oa + _sigmoid(gb_ref[...]) * (yo * g)
    x1 = x_ref[...] + jnp.dot(mixed.astype(BF16), wo_ref[...], preferred_element_type=F32)
    x1_ref[...] = x1
    ms = jnp.mean(x1 * x1, axis=-1, keepdims=True)
    h2 = (x1 * lax.rsqrt(ms + RMS_EPS)) * n2_ref[...]
    h2_ref[...] = h2
    logits = jnp.dot(h2, wr_ref[...], preferred_element_type=F32, precision=lax.Precision.HIGHEST) + br_ref[...]
    lane = lax.broadcasted_iota(jnp.int32, logits.shape, 1)
    lane_f = lane.astype(F32)
    te = jnp.zeros(logits.shape, jnp.int32)
    tv = jnp.full(logits.shape, NEG, F32)
    chosen = []
    for k in range(TOP_K):
        m = jnp.max(logits, axis=-1, keepdims=True)
        idx = jnp.min(jnp.where(logits == m, lane_f, 128.0), axis=-1, keepdims=True).astype(jnp.int32)
        te = jnp.where(lane == k, idx, te)
        tv = jnp.where(lane == k, m, tv)
        chosen.append(lane == idx)
        logits = jnp.where(chosen[k], NEG, logits)
    e = jnp.where(lane < TOP_K, jnp.exp(tv - jnp.max(tv, axis=-1, keepdims=True)), 0.0)
    te_ref[...] = te
    tg_ref[...] = e / jnp.sum(e, axis=-1, keepdims=True)

    @pl.when(i == 0)
    def _():
        cnt_scr[...] = jnp.zeros(cnt_scr.shape, F32)

    first_row = jnp.where(is_p, (i % (T_PAD // TM5)) * TM5, FRONT)
    routed = (lax.broadcasted_iota(jnp.int32, (TM5, 1), 0) + first_row) >= FRONT
    hits = jnp.zeros(logits.shape, F32)
    for k in range(TOP_K):
        hits = hits + jnp.where(chosen[k], 1.0, 0.0)
    hits = jnp.where(routed, hits, 0.0)
    before = cnt_scr[...] + jnp.dot(ltri_ref[...], hits.astype(BF16), preferred_element_type=F32)
    rank = jnp.zeros(logits.shape, jnp.int32)
    for k in range(TOP_K):
        rk = jnp.sum(jnp.where(chosen[k], before, 0.0), axis=-1, keepdims=True).astype(jnp.int32)
        rank = jnp.where(lane == k, rk, rank)
    rank_ref[...] = rank
    cnt_scr[...] = cnt_scr[...] + jnp.sum(hits, axis=0, keepdims=True)
    cnt_ref[...] = jnp.broadcast_to(cnt_scr[...], cnt_ref.shape)


def _mix(yo_p, yo_s, g_p, g_s, oa_p, oa_s, p_all, x_all, wo, n2, wr, br):
    nbp = ROWS_P // TM5
    rb = lambda i: (i, 0)
    pb = pl.BlockSpec((TM5, D), lambda i: (jnp.minimum(i, nbp - 1), 0))
    sb = pl.BlockSpec((TM5, D), lambda i: (jnp.maximum(i - nbp, 0), 0))
    full = lambda shp: pl.BlockSpec(shp, lambda i: tuple(0 for _ in shp))
    return pl.pallas_call(
        _mix_kernel,
        grid=(ROWS // TM5,),
        in_specs=[pl.BlockSpec((B_HEADS, B_HEAD, TM5),
                               lambda i: (jnp.minimum(i, nbp - 1) // (T_PAD // TM5), 0,
                                          jnp.minimum(i, nbp - 1) % (T_PAD // TM5))),
                  sb, pb, sb, pb, sb,
                  pl.BlockSpec((TM5, D), lambda i: (i, OFF_GA // D)),
                  pl.BlockSpec((TM5, D), lambda i: (i, OFF_GB // D)),
                  pl.BlockSpec((TM5, D), rb),
                  full((D, D)), full((1, D)), full((D, 128)), full((1, 128)), full((TM5, TM5))],
        out_specs=[pl.BlockSpec((TM5, D), rb), pl.BlockSpec((TM5, D), rb),
                   pl.BlockSpec((TM5, 128), rb), pl.BlockSpec((TM5, 128), rb),
                   pl.BlockSpec((TM5, 128), rb), full((8, 128))],
        out_shape=[jax.ShapeDtypeStruct((ROWS, D), F32), jax.ShapeDtypeStruct((ROWS, D), F32),
                   jax.ShapeDtypeStruct((ROWS, 128), jnp.int32), jax.ShapeDtypeStruct((ROWS, 128), F32),
                   jax.ShapeDtypeStruct((ROWS, 128), jnp.int32), jax.ShapeDtypeStruct((8, 128), F32)],
        scratch_shapes=[pltpu.VMEM((1, 128), F32)],
        compiler_params=_cparams(("arbitrary",)),
        name="mix",
    )(yo_p, yo_s, g_p, g_s, oa_p, oa_s, p_all, p_all, x_all, wo, n2, wr, br,
      jnp.asarray(np.tril(np.ones((TM5, TM5), np.float32), -1), BF16))


N_TOK = NB * (N_META + SEQ) + ROWS_S
NK = N_TOK * TOP_K
TMM = 256
N_BLK = -(-NK // TMM) + N_EXPERTS
NSLOT = N_BLK * TMM
TN_G1 = 2048
TN_G2 = 2048
TMC = 128


def _gather_kernel(slot_ref, nused_ref, h_hbm, o_ref, buf, sem):
    m = pl.program_id(0)
    cur = m % 2

    def issue(blk, slot):
        def row(r, _):
            tok = slot_ref[blk * TMM + r]
            pltpu.make_async_copy(h_hbm.at[pl.ds(tok, 1), :], buf.at[slot, pl.ds(r, 1), :], sem.at[slot]).start()
            return 0

        lax.fori_loop(0, TMM, row, 0, unroll=8)

    @pl.when((m == 0) & (m < nused_ref[0]))
    def _():
        issue(m, 0)

    @pl.when(m + 1 < nused_ref[0])
    def _():
        issue(m + 1, 1 - cur)

    @pl.when(m < nused_ref[0])
    def _():
        pltpu.make_async_copy(h_hbm.at[pl.ds(0, TMM), :], buf.at[cur], sem.at[cur]).wait()
        o_ref[...] = buf[cur].astype(BF16)

    @pl.when(m >= nused_ref[0])
    def _():
        o_ref[...] = jnp.zeros(o_ref.shape, BF16)


def _gather(slot_tok, nused, h2):
    return pl.pallas_call(
        _gather_kernel,
        grid_spec=pltpu.PrefetchScalarGridSpec(
            num_scalar_prefetch=2,
            grid=(N_BLK,),
            in_specs=[pl.BlockSpec(memory_space=pl.ANY)],
            out_specs=pl.BlockSpec((TMM, D), lambda m, st, nu: (m, 0)),
            scratch_shapes=[pltpu.VMEM((2, TMM, D), F32), pltpu.SemaphoreType.DMA((2,))]),
        out_shape=jax.ShapeDtypeStruct((NSLOT, D), BF16),
        compiler_params=_cparams(("arbitrary",)),
        name="moe_gather",
    )(slot_tok, nused, h2)


def _load_expert_weights(sched, w_hbm, wstage, wsem, wb_scr, tn):
    blk_e_ref, first_ref, runid_ref, nexte_ref, meta_ref = sched
    f, m = pl.program_id(0), pl.program_id(1)
    n_runs = meta_ref[1]

    def fetch(e, ff, slot):
        cols = pl.ds(pl.multiple_of(ff * tn, tn), tn)
        return pltpu.make_async_copy(w_hbm.at[e, :, cols], wstage.at[slot], wsem.at[slot])

    @pl.when((m < meta_ref[0]) & (first_ref[m] == 1))
    def _():
        r = runid_ref[m]
        g = f * n_runs + r
        slot = g % 2

        @pl.when(g == 0)
        def _():
            fetch(blk_e_ref[m], f, 0).start()

        nf = f + jnp.where(r == n_runs - 1, 1, 0)

        @pl.when(nf < pl.num_programs(0))
        def _():
            fetch(nexte_ref[m], nf, 1 - slot).start()

        fetch(blk_e_ref[m], f, slot).wait()
        wb_scr[...] = wstage[slot].astype(BF16)


def _gm1_kernel(blk_e_ref, first_ref, runid_ref, nexte_ref, meta_ref, x_ref, w_hbm, b_ref, sel_ref, o_ref,
                wb_scr, wstage, wsem):
    m = pl.program_id(1)
    used = m < meta_ref[0]
    _load_expert_weights((blk_e_ref, first_ref, runid_ref, nexte_ref, meta_ref), w_hbm, wstage, wsem, wb_scr, TN_G1)

    @pl.when(used)
    def _():
        hd = jnp.dot(x_ref[...], wb_scr[...], preferred_element_type=F32) + b_ref[...]
        glu = jnp.minimum(hd, SWIGLU_LIMIT)
        lin = jnp.clip(hd, -SWIGLU_LIMIT, SWIGLU_LIMIT) + 1.0
        act = glu * _sigmoid(SWIGLU_ALPHA * glu)
        for c in range(TN_G1 // 512):
            parts = []
            for q in range(4):
                sl = slice(c * 512 + q * 128, c * 512 + (q + 1) * 128)
                parts.append(act[:, sl] * pltpu.roll(lin[:, sl], 127, axis=1))
            z = jnp.concatenate(parts, axis=1).astype(BF16)
            o_ref[:, c * 256:(c + 1) * 256] = jnp.dot(z, sel_ref[...], preferred_element_type=F32).astype(BF16)

    @pl.when(jnp.logical_not(used))
    def _():
        o_ref[...] = jnp.zeros(o_ref.shape, BF16)


def _meff(m, meta):
    return jnp.maximum(jnp.minimum(m, meta[0] - 1), 0)


def _gm1(sched, xs, w1, b1, sel):
    return pl.pallas_call(
        _gm1_kernel,
        grid_spec=pltpu.PrefetchScalarGridSpec(
            num_scalar_prefetch=5,
            grid=(2 * D_FF // TN_G1, N_BLK),
            in_specs=[pl.BlockSpec((TMM, D), lambda f, m, be, fi, ri, ne, mt: (_meff(m, mt), 0)),
                      pl.BlockSpec(memory_space=pl.ANY),
                      pl.BlockSpec((None, 1, TN_G1), lambda f, m, be, fi, ri, ne, mt: (be[_meff(m, mt)], 0, f)),
                      pl.BlockSpec((512, 256), lambda f, m, be, fi, ri, ne, mt: (0, 0))],
            out_specs=pl.BlockSpec((TMM, TN_G1 // 2), lambda f, m, be, fi, ri, ne, mt: (m, f)),
            scratch_shapes=[pltpu.VMEM((D, TN_G1), BF16), pltpu.VMEM((2, D, TN_G1), F32),
                            pltpu.SemaphoreType.DMA((2,))]),
        out_shape=jax.ShapeDtypeStruct((NSLOT, D_FF), BF16),
        compiler_params=_cparams(("arbitrary", "arbitrary")),
        name="moe_up",
    )(*sched, xs, w1, b1, sel)


def _gm2_kernel(blk_e_ref, first_ref, runid_ref, nexte_ref, meta_ref, x_ref, w_hbm, b_ref, o_ref,
                wb_scr, wstage, wsem):
    m = pl.program_id(1)
    used = m < meta_ref[0]
    _load_expert_weights((blk_e_ref, first_ref, runid_ref, nexte_ref, meta_ref), w_hbm, wstage, wsem, wb_scr, TN_G2)

    @pl.when(used)
    def _():
        o_ref[...] = jnp.dot(x_ref[...], wb_scr[...], preferred_element_type=F32) + b_ref[...]

    @pl.when(jnp.logical_not(used))
    def _():
        o_ref[...] = jnp.zeros(o_ref.shape, F32)


def _gm2(sched, hid, w2, b2):
    return pl.pallas_call(
        _gm2_kernel,
        grid_spec=pltpu.PrefetchScalarGridSpec(
            num_scalar_prefetch=5,
            grid=(D // TN_G2, N_BLK),
            in_specs=[pl.BlockSpec((TMM, D_FF), lambda f, m, be, fi, ri, ne, mt: (_meff(m, mt), 0)),
                      pl.BlockSpec(memory_space=pl.ANY),
                      pl.BlockSpec((None, 1, TN_G2), lambda f, m, be, fi, ri, ne, mt: (be[_meff(m, mt)], 0, f))],
            out_specs=pl.BlockSpec((TMM, TN_G2), lambda f, m, be, fi, ri, ne, mt: (m, f)),
            scratch_shapes=[pltpu.VMEM((D_FF, TN_G2), BF16), pltpu.VMEM((2, D_FF, TN_G2), F32),
                            pltpu.SemaphoreType.DMA((2,))]),
        out_shape=jax.ShapeDtypeStruct((NSLOT, D), F32),
        compiler_params=_cparams(("arbitrary", "arbitrary")),
        name="moe_down",
    )(*sched, hid, w2, b2)


def _combine_kernel(dest_ref, x1_ref, tg_ref, fg_ref, y_hbm, yp_ref, ys_ref, buf, sem):
    m = pl.program_id(0)
    cur = m % 2

    def issue(blk, slot):
        def row(r, _):
            for k in range(TOP_K):
                d = dest_ref[(blk * TMC + r) * TOP_K + k]
                pltpu.make_async_copy(y_hbm.at[pl.ds(d, 1), :], buf.at[slot, pl.ds(k * TMC + r, 1), :],
                                      sem.at[slot]).start()
            return 0

        lax.fori_loop(0, TMC, row, 0, unroll=4)

    @pl.when(m == 0)
    def _():
        issue(m, 0)

    @pl.when(m + 1 < pl.num_programs(0))
    def _():
        issue(m + 1, 1 - cur)

    pltpu.make_async_copy(y_hbm.at[pl.ds(0, TOP_K * TMC), :], buf.at[cur], sem.at[cur]).wait()
    x2 = x1_ref[...]
    tg = tg_ref[...]
    for k in range(TOP_K):
        x2 = x2 + tg[:, k:k + 1] * buf[cur, k * TMC:(k + 1) * TMC, :]
    ms = jnp.mean(x2 * x2, axis=-1, keepdims=True)
    y = (x2 * lax.rsqrt(ms + RMS_EPS)) * fg_ref[...]

    @pl.when((m < ROWS_P // TMC) & (m % (T_PAD // TMC) >= CB_SKIP))
    def _():
        yp_ref[...] = y

    @pl.when(m >= ROWS_P // TMC)
    def _():
        ys_ref[...] = y


CB_SKIP = (FRONT + N_META) // TMC


def _combine(dest, x1, tg, fg, y):
    per_b = T_PAD // TMC

    def prompt_blk(m, d):
        mp = jnp.minimum(m, ROWS_P // TMC - 1)
        return ((mp // per_b) * (SEQ // TMC) + jnp.maximum(mp % per_b - CB_SKIP, 0), 0)

    return pl.pallas_call(
        _combine_kernel,
        grid_spec=pltpu.PrefetchScalarGridSpec(
            num_scalar_prefetch=1,
            grid=(ROWS // TMC,),
            in_specs=[pl.BlockSpec((TMC, D), lambda m, d: (m, 0)),
                      pl.BlockSpec((TMC, 128), lambda m, d: (m, 0)),
                      pl.BlockSpec((1, D), lambda m, d: (0, 0)),
                      pl.BlockSpec(memory_space=pl.ANY)],
            out_specs=[pl.BlockSpec((TMC, D), prompt_blk),
                       pl.BlockSpec((TMC, D), lambda m, d: (jnp.maximum(m - ROWS_P // TMC, 0), 0))],
            scratch_shapes=[pltpu.VMEM((2, TOP_K * TMC, D), F32), pltpu.SemaphoreType.DMA((2,))]),
        out_shape=[jax.ShapeDtypeStruct((NB * SEQ, D), F32), jax.ShapeDtypeStruct((ROWS_S, D), F32)],
        compiler_params=_cparams(("arbitrary",)),
        name="moe_combine",
    )(dest, x1, tg, fg, y)


def _t5_bucket(rel):
    nb = N_BUCKETS // 2
    ret = jnp.where(rel > 0, nb, 0)
    n = jnp.abs(rel)
    max_exact = nb // 2
    nf = jnp.maximum(n, 1).astype(jnp.float32)
    large = max_exact + (jnp.log(nf / max_exact) / math.log(MAX_DISTANCE / max_exact) * (nb - max_exact)).astype(jnp.int32)
    large = jnp.minimum(large, nb - 1)
    return ret + jnp.where(n < max_exact, n, large)


def _valid_rows():
    rows = [np.arange(b * T_PAD + FRONT, (b + 1) * T_PAD) for b in range(NB)]
    rows.append(np.arange(ROWS_P, ROWS))
    return np.concatenate(rows).astype(np.int32)


TR_KV = math.gcd(FRONT, N_META + SEQ)


def _kv_out_kernel(k_ref, v_ref, ko_ref, vo_ref):
    for h in range(A_HEADS):
        ko_ref[:, h, :] = k_ref[:, h * A_VD:(h + 1) * A_VD]
        vo_ref[:, h, :] = v_ref[:, h * A_VD:(h + 1) * A_VD]


def _kv_out(p_all):
    per_b = T_PAD // TR_KV
    src = lambda off: pl.BlockSpec((TR_KV, D), lambda b, i: (b * per_b + FRONT // TR_KV + i, off // D))
    dst = pl.BlockSpec((None, TR_KV, A_HEADS, A_VD), lambda b, i: (b, i, 0, 0))
    out = jax.ShapeDtypeStruct((NB, N_META + SEQ, A_HEADS, A_VD), F32)
    return pl.pallas_call(
        _kv_out_kernel,
        grid=(NB, (N_META + SEQ) // TR_KV),
        in_specs=[src(OFF_K), src(OFF_V)],
        out_specs=[dst, dst],
        out_shape=[out, out],
        compiler_params=_cparams(("arbitrary", "arbitrary")),
        name="kv_out",
    )(p_all, p_all)


def _slot_tables(te, rank, cnt):
    counts = cnt[0, :N_EXPERTS].astype(jnp.int32)
    padded = (counts + TMM - 1) // TMM * TMM
    pad_end = jnp.cumsum(padded)
    pad_start = pad_end - padded
    experts = jnp.arange(N_EXPERTS, dtype=jnp.int32)
    routed = jnp.asarray(np.isin(np.arange(ROWS), _valid_rows()))[:, None]
    start_of = jnp.sum(jnp.where(te[:, :TOP_K, None] == experts, pad_start, 0), axis=-1)
    dest2 = start_of + rank[:, :TOP_K]
    dest = jnp.where(routed, dest2, 0).reshape(-1)
    rows = jnp.broadcast_to(jnp.arange(ROWS, dtype=jnp.int32)[:, None], (ROWS, TOP_K))
    slot_tok = jnp.zeros((NSLOT,), jnp.int32).at[jnp.where(routed, dest2, NSLOT).reshape(-1)].set(
        rows.reshape(-1), mode='drop')
    blk_start = jnp.arange(N_BLK, dtype=jnp.int32) * TMM
    blk_e = jnp.minimum(jnp.sum(pad_end[None, :] <= blk_start[:, None], axis=-1), N_EXPERTS - 1).astype(jnp.int32)
    nused = (pad_end[-1:] // TMM).astype(jnp.int32)
    blk = jnp.arange(N_BLK, dtype=jnp.int32)
    first = (blk < nused[0]) & ((blk == 0) | (blk_e != jnp.roll(blk_e, 1)))
    run_id = jnp.cumsum(first.astype(jnp.int32)) - 1
    n_runs = jnp.sum(first.astype(jnp.int32))
    run_e = jnp.zeros((N_BLK,), jnp.int32).at[jnp.where(first, run_id, N_BLK)].set(blk_e, mode='drop')
    next_e = run_e[(run_id + 1) % jnp.maximum(n_runs, 1)]
    sched = (blk_e, first.astype(jnp.int32), run_id, next_e, jnp.stack([nused[0], n_runs]))
    return dest, slot_tok, nused, sched


def kernel(x_prompt, x_sample, cache_k, cache_v, state_wkv, state_shift, meta_tokens, rel_bias_table, norm1_g, w_in, shift_mu, decay_base, w_decay_up, a_base, w_a_up, w_g_up, k_k, k_a, r_k, gn_w, gn_b, lam_q1, lam_k1, lam_q2, lam_k2, subln_g, w_out, norm2_g, w_router, b_router, w_e1, b_e1, w_e2, b_e2, final_g):
    xp = jnp.concatenate([jnp.zeros((NB, FRONT, D), F32),
                          jnp.broadcast_to(meta_tokens[None], (NB, N_META, D)), x_prompt], axis=1)
    x_all = jnp.concatenate([xp.reshape(ROWS_P, D), x_sample.reshape(ROWS_S, D)], axis=0)
    lora0 = OFF_R + 3 * D
    w_main = w_in[0].astype(BF16)
    w_tail = jnp.concatenate([w_main[:, lora0 + LORA_W:], w_main[:, lora0:lora0 + LORA_W],
                              jnp.zeros((D, LORA_P - LORA_W), BF16)], axis=1)

    p_all = _inproj(x_all, norm1_g, w_main, w_tail)

    kq = jnp.arange(TQ, dtype=jnp.int32)
    bkt = jnp.stack([_t5_bucket(kq[:, None] - kq[None, :]), _t5_bucket(kq[:, None] - kq[None, :] - TQ)])
    lamp = jnp.concatenate([lam_q1, lam_k1, lam_q2, lam_k2], axis=0)
    oa_p = _attn_prompt(p_all, rel_bias_table, bkt, lamp, subln_g)
    qpos = PAST + jnp.arange(DEC_S, dtype=jnp.int32)
    bkt_last = _t5_bucket((PAST - NEAR + jnp.arange(NEAR, dtype=jnp.int32))[None, :] - qpos[:, None])
    bkt_new = _t5_bucket(qpos[None, :] - qpos[:, None])
    oa_s = _attn_sample(p_all, cache_k[0], cache_v[0],
                        rel_bias_table, bkt_last, bkt_new, lamp, subln_g)

    mu = shift_mu[0]
    mu3 = mu[:3 * D][None]
    mul = jnp.pad(mu[3 * D:], (0, LORA_P - LORA_W))[None]
    wd = jnp.zeros((LORA_P, D), F32).at[:DECAY_LORA].set(w_decay_up[0]).astype(BF16)
    wa = jnp.zeros((LORA_P, D), F32).at[DECAY_LORA:DECAY_LORA + AAA_LORA].set(w_a_up[0]).astype(BF16)
    wg = jnp.zeros((LORA_P, D), F32).at[DECAY_LORA + AAA_LORA:LORA_W].set(w_g_up[0]).astype(BF16)
    first_p = jnp.zeros((NB, 1, 3 * D + LORA_P), F32)
    first_s = jnp.pad(state_shift[0], ((0, 0), (0, LORA_P - LORA_W)))[:, None, :]
    prep_p = _rwkv_prep(p_all, first_p, mu3, mul, decay_base, a_base, wd, wa, wg, NB, T_PAD // 256, 256, 0, True)
    prep_s = _rwkv_prep(p_all, first_s, mu3, mul, decay_base, a_base, wd, wa, wg, DEC_B, 1, DEC_S, ROWS_P // DEC_S,
                        False)

    def chain_tile(v):
        return jnp.tile(v.reshape(B_HEADS, B_HEAD).T, (1, 128 // B_HEADS))

    par = [chain_tile(v.reshape(-1)) for v in (k_k[0], k_a[0], r_k[0], gn_w[0], gn_b[0])]
    s0_p = jnp.zeros((B_HEAD, B_HEAD, NB * B_HEADS), F32)
    s0_s = state_wkv[0].transpose(3, 2, 0, 1).reshape(B_HEAD, B_HEAD, DEC_B * B_HEADS)
    yo_p, sfin_p = _rwkv_rec(*[a.transpose(2, 1, 0) for a in prep_p[:5]], *par, s0_p, skip=FRONT // TB)
    yo_s, sfin_s = _rwkv_rec(*[_to_chains(a, DEC_B, DEC_S) for a in prep_s[:5]], *par, s0_s)

    wo = w_out[0].astype(BF16)
    wr = jnp.pad(w_router[0], ((0, 0), (0, 128 - N_EXPERTS)))
    br = jnp.concatenate([b_router[0], jnp.full((128 - N_EXPERTS,), NEG, F32)])[None]
    x1, h2, te, tg, rank, cnt = _mix(yo_p.transpose(2, 1, 0), _from_chains(yo_s, DEC_B, DEC_S), prep_p[5], prep_s[5],
                                     oa_p, oa_s, p_all, x_all, wo, norm2_g, wr, br)

    dest, slot_tok, nused, sched = _slot_tables(te, rank, cnt)

    sel_np = np.zeros((512, 256), np.float32)
    sel_np[2 * np.arange(256), np.arange(256)] = 1.0
    xs = _gather(slot_tok, nused, h2)
    hid = _gm1(sched, xs, w_e1[0], b_e1[0][:, None, :], jnp.asarray(sel_np, BF16))
    ys = _gm2(sched, hid, w_e2[0], b_e2[0][:, None, :])
    y_prompt, y_sample = _combine(dest, x1, tg, final_g[None], ys)

    y_prompt = y_prompt.reshape(NB, SEQ, D)
    y_sample = y_sample.reshape(DEC_B, DEC_S, D)
    def last_ps(last):
        return last[:, 0, :SHIFT_W][None]

    kp, vp = _kv_out(p_all)
    k_prompt, v_prompt = kp[None], vp[None]
    k_sample = p_all[ROWS_P:, OFF_K:OFF_K + D].reshape(1, DEC_B, DEC_S, A_HEADS, A_VD)
    v_sample = p_all[ROWS_P:, OFF_V:OFF_V + D].reshape(1, DEC_B, DEC_S, A_HEADS, A_VD)
    shift_prompt = last_ps(prep_p[6])
    shift_sample = last_ps(prep_s[6])
    wkv_prompt = sfin_p.reshape(B_HEAD, B_HEAD, NB, B_HEADS).transpose(2, 3, 1, 0)[None]
    wkv_sample = sfin_s.reshape(B_HEAD, B_HEAD, DEC_B, B_HEADS).transpose(2, 3, 1, 0)[None].astype(state_wkv.dtype)
    return (y_prompt, y_sample, k_prompt, v_prompt, wkv_prompt, shift_prompt,
            k_sample, v_sample, wkv_sample, shift_sample)
```

```python
import functools
import math

import numpy as np
import jax
import jax.numpy as jnp
from jax import lax
from jax.experimental import pallas as pl
from jax.experimental.pallas import tpu as pltpu

F32 = jnp.float32
BF16 = jnp.bfloat16

D = 2048
NB = 4
SEQ = 2048
DEC_B = 16
DEC_S = 32
PAST = 2048
CHUNK = 64
N_META = 16
RMS_EPS = 1e-5
A_DH = 64
A_HEADS = 16
A_VD = 128
B_HEAD = 64
B_HEADS = 32
DECAY_LORA = 96
AAA_LORA = 96
GATE_LORA = 256
GN_EPS = 64e-5
N_BUCKETS = 32
MAX_DISTANCE = 128
N_EXPERTS = 32
TOP_K = 4
D_FF = 2048
SWIGLU_ALPHA = 1.702
SWIGLU_LIMIT = 7.0
LAM_INIT = 0.8 - 0.6 * math.exp(-0.3 * 0)
SHIFT_W = 3 * D + DECAY_LORA + AAA_LORA + GATE_LORA
LORA_W = DECAY_LORA + AAA_LORA + GATE_LORA
LORA_P = 512

T_PAD = 2304
FRONT = T_PAD - N_META - SEQ
ROWS_P = NB * T_PAD
ROWS_S = DEC_B * DEC_S
ROWS = ROWS_P + ROWS_S

OFF_Q = 0
OFF_K = 2048
OFF_V = 4096
OFF_R = 6144
OFF_KX = 8192
OFF_VX = 10240
OFF_GA = 12288
OFF_GB = 14336
OFF_LORA = 16384
PW = OFF_LORA + LORA_P

NEG = -1e30
LOG2E = 1.4426950408889634
VMEM_LIMIT = 56 * 1024 * 1024


def _cparams(sem, vmem=VMEM_LIMIT):
    return pltpu.CompilerParams(dimension_semantics=sem, vmem_limit_bytes=vmem)


def _sigmoid(x):
    return 1.0 / (1.0 + jnp.exp(-x))


TM1 = 512
TN1 = 1536
N_MAIN = OFF_GA // TN1


TMN = 256


def _norm_kernel(xp_ref, xs_ref, fb_ref, g_ref, x_ref, h_ref):
    i = pl.program_id(0)
    is_p = i < ROWS_P // TMN
    first = is_p & (i % (T_PAD // TMN) == 0)
    x = jnp.where(first, fb_ref[...], jnp.where(is_p, xp_ref[...], xs_ref[...]))
    x_ref[...] = x
    ms = jnp.mean(x * x, axis=-1, keepdims=True)
    h_ref[...] = ((x * lax.rsqrt(ms + RMS_EPS)) * g_ref[...]).astype(BF16)


def _norm(x_prompt, x_sample, first_blk, g):
    nbp, per_b = ROWS_P // TMN, T_PAD // TMN

    def frames(i):
        ip = jnp.minimum(i, nbp - 1)
        return ((ip // per_b) * (SEQ // TMN) + jnp.maximum(ip % per_b - 1, 0), 0)

    return pl.pallas_call(
        _norm_kernel,
        grid=(ROWS // TMN,),
        in_specs=[pl.BlockSpec((TMN, D), frames),
                  pl.BlockSpec((TMN, D), lambda i: (jnp.maximum(i - nbp, 0), 0)),
                  pl.BlockSpec((TMN, D), lambda i: (0, 0)),
                  pl.BlockSpec((1, D), lambda i: (0, 0))],
        out_specs=[pl.BlockSpec((TMN, D), lambda i: (i, 0))] * 2,
        out_shape=[jax.ShapeDtypeStruct((ROWS, D), F32), jax.ShapeDtypeStruct((ROWS, D), BF16)],
        compiler_params=_cparams(("arbitrary",)),
        name="norm_in",
    )(x_prompt.reshape(NB * SEQ, D), x_sample.reshape(ROWS_S, D), first_blk, g)


def _inproj_kernel(h_ref, wm_ref, wt_ref, o_ref):
    j = pl.program_id(0)

    @pl.when(j < N_MAIN)
    def _():
        o_ref[...] = jnp.dot(h_ref[...], wm_ref[...], preferred_element_type=F32)

    @pl.when(j >= N_MAIN)
    def _():
        o_ref[...] = jnp.dot(h_ref[...], wt_ref[...], preferred_element_type=F32)


def _inproj(h, w_main, w_tail):
    return pl.pallas_call(
        _inproj_kernel,
        grid=(PW // TN1, ROWS // TM1),
        in_specs=[pl.BlockSpec((TM1, D), lambda j, i: (i, 0)),
                  pl.BlockSpec((D, TN1), lambda j, i: (0, jnp.minimum(j, N_MAIN - 1))),
                  pl.BlockSpec((D, TN1), lambda j, i: (0, jnp.maximum(j - N_MAIN, 0)))],
        out_specs=pl.BlockSpec((TM1, TN1), lambda j, i: (i, j)),
        out_shape=jax.ShapeDtypeStruct((ROWS, PW), F32),
        compiler_params=_cparams(("arbitrary", "arbitrary")),
        name="inproj",
    )(h, w_main, w_tail)


def _lam_value(lamp_ref):
    lp = lamp_ref[...]
    s1 = jnp.sum(lp[0:1, :] * lp[1:2, :], axis=-1, keepdims=True)
    s2 = jnp.sum(lp[2:3, :] * lp[3:4, :], axis=-1, keepdims=True)
    return jnp.exp(s1) - jnp.exp(s2) + LAM_INIT


def _bias_from_buckets(bkt, table_ref, h):
    out = jnp.zeros(bkt.shape, F32)
    for n in range(N_BUCKETS):
        out = jnp.where(bkt == n, table_ref[n, h], out)
    return out


TQ = 256
NQB = T_PAD // TQ
BIAS_META, BIAS_DIAG, BIAS_SUB, BIAS_SUB_PAD, BIAS_FAR_PAD, BIAS_FAR = range(6)


def _attn_prompt_kernel(table_ref, q_ref, k_ref, v_ref, bkt_ref, lamp_ref, subg_ref, o_ref,
                        kb_scr, vt_scr, bias_scr, s_scr, acc_scr):
    h = pl.program_id(0)
    b = pl.program_id(1)

    @pl.when(b == 0)
    def _():
        kk = lax.broadcasted_iota(jnp.int32, (TQ, TQ), 0)
        qq = lax.broadcasted_iota(jnp.int32, (TQ, TQ), 1)
        chunk_ok = (kk // CHUNK) <= (qq // CHUNK)
        pad = jnp.where(kk < FRONT, NEG, 0.0).astype(F32)
        b_diag = _bias_from_buckets(bkt_ref[0], table_ref, h)
        b_sub = _bias_from_buckets(bkt_ref[1], table_ref, h)
        b_far = jnp.zeros((TQ, TQ), F32) + table_ref[N_BUCKETS // 2 - 1, h]
        tiles = (jnp.where(chunk_ok, b_diag, NEG) + pad,
                 jnp.where(chunk_ok, b_diag, NEG),
                 b_sub,
                 b_sub + pad,
                 b_far + pad,
                 b_far)
        for i, t in enumerate(tiles):
            bias_scr[i] = jnp.concatenate([t, t], axis=1) * LOG2E

    kb_scr[...] = k_ref[...].astype(BF16)
    for i in range(NQB):
        vt_scr[i] = v_ref[i * TQ:(i + 1) * TQ, :].T.astype(BF16)

    lam = _lam_value(lamp_ref)
    drow = lax.broadcasted_iota(jnp.int32, (A_VD, TQ), 0)

    def q_operand(qi):
        q0 = pl.multiple_of(qi * TQ, TQ)
        qt = (q_ref[pl.ds(q0, TQ), :] * (A_DH ** -0.5 * LOG2E)).T
        q1 = jnp.where(drow < A_DH, qt, 0.0)
        q2 = jnp.where(drow >= A_DH, qt, 0.0)
        return jnp.concatenate([q1, q2], axis=1).astype(BF16)

    def scores(kj, qpt, slot):
        k0 = pl.multiple_of(kj * TQ, TQ)
        s_scr[slot] = jnp.dot(kb_scr[pl.ds(k0, TQ), :], qpt, preferred_element_type=F32)

    def accumulate(kj, bias_id, slot, ml):
        m, l = ml
        s = s_scr[slot] + bias_scr[bias_id]
        m_new = jnp.maximum(m, jnp.max(s, axis=0, keepdims=True))
        alpha = jnp.exp2(m - m_new)
        p = jnp.exp2(s - m_new)
        l = l * alpha + jnp.sum(p, axis=0, keepdims=True)
        acc_scr[...] = acc_scr[...] * alpha + jnp.dot(vt_scr[kj], p.astype(BF16), preferred_element_type=F32)
        return m_new, l

    def finish(qi, ml):
        on = acc_scr[...] / ml[1]
        ot = on[:, :TQ] - lam * on[:, TQ:]
        ms = jnp.mean(ot * ot, axis=0, keepdims=True)
        o = (ot * lax.rsqrt(ms + RMS_EPS)).T * subg_ref[...] * (1.0 - LAM_INIT)
        q0 = pl.multiple_of(qi * TQ, TQ)
        o_ref[pl.ds(q0, TQ), :] = o

    def start(qi):
        qpt = q_operand(qi)
        acc_scr[...] = jnp.zeros(acc_scr.shape, F32)
        scores(0, qpt, 0)
        return qpt, (jnp.full((1, 2 * TQ), NEG, F32), jnp.zeros((1, 2 * TQ), F32))

    finish(0, accumulate(0, BIAS_META, 0, start(0)[1]))

    def q_block(qi, _):
        qpt, ml = start(qi)

        def half(kj, cur, ml):
            scores(kj + 1, qpt, 1 - cur)
            first = kj == 0
            bias_id = jnp.where(kj == qi - 1, jnp.where(first, BIAS_SUB_PAD, BIAS_SUB),
                                jnp.where(first, BIAS_FAR_PAD, BIAS_FAR))
            return accumulate(kj, bias_id, cur, ml)

        ml = lax.fori_loop(0, qi // 2, lambda i, ml: half(2 * i + 1, 1, half(2 * i, 0, ml)), ml)
        ml = lax.cond(qi % 2 == 1,
                      lambda ml: accumulate(qi, BIAS_DIAG, 1, half(qi - 1, 0, ml)),
                      lambda ml: accumulate(qi, BIAS_DIAG, 0, ml), ml)
        finish(qi, ml)
        return 0

    lax.fori_loop(1, NQB, q_block, 0)


def _attn_prompt(p_all, table, bkt, lamp, subg):
    cb = D // A_VD
    return pl.pallas_call(
        _attn_prompt_kernel,
        grid_spec=pltpu.PrefetchScalarGridSpec(
            num_scalar_prefetch=0,
            grid=(A_HEADS, NB),
            in_specs=[pl.BlockSpec(memory_space=pltpu.SMEM),
                      pl.BlockSpec((T_PAD, A_VD), lambda h, b: (b, h)),
                      pl.BlockSpec((T_PAD, A_VD), lambda h, b: (b, cb + h)),
                      pl.BlockSpec((T_PAD, A_VD), lambda h, b: (b, 2 * cb + h)),
                      pl.BlockSpec((2, TQ, TQ), lambda h, b: (0, 0, 0)),
                      pl.BlockSpec((4, A_DH), lambda h, b: (0, 0)),
                      pl.BlockSpec((1, A_VD), lambda h, b: (0, 0))],
            out_specs=pl.BlockSpec((T_PAD, A_VD), lambda h, b: (b, h)),
            scratch_shapes=[pltpu.VMEM((T_PAD, A_VD), BF16),
                            pltpu.VMEM((NQB, A_VD, TQ), BF16),
                            pltpu.VMEM((6, TQ, 2 * TQ), F32),
                            pltpu.VMEM((2, TQ, 2 * TQ), F32),
                            pltpu.VMEM((A_VD, 2 * TQ), F32)]),
        out_shape=jax.ShapeDtypeStruct((ROWS_P, D), F32),
        compiler_params=_cparams(("arbitrary", "arbitrary")),
        name="attn_prompt",
    )(table, p_all, p_all, p_all, bkt, lamp, subg)


HG = 4
NEAR = 512


def _attn_sample_kernel(table_ref, q_ref, kn_ref, vn_ref, bl_ref, bn_ref, lamp_ref, subg_ref, ck_hbm, cv_hbm,
                        o_ref, kbuf, vbuf, sem):
    hg = pl.program_id(1)
    ng = pl.num_programs(1)
    step = pl.program_id(0) * ng + hg
    nsteps = pl.num_programs(0) * ng

    def copies(n, slot):
        out = []
        for i in range(HG):
            h = (n % ng) * HG + i
            out.append(pltpu.make_async_copy(ck_hbm.at[n // ng, :, h, :], kbuf.at[slot, i], sem.at[slot, i]))
            out.append(pltpu.make_async_copy(cv_hbm.at[n // ng, :, h, :], vbuf.at[slot, i], sem.at[slot, HG + i]))
        return out

    cur = step % 2

    @pl.when(step == 0)
    def _():
        for c in copies(step, 0):
            c.start()

    @pl.when(step + 1 < nsteps)
    def _():
        for c in copies(step + 1, 1 - cur):
            c.start()

    for c in copies(step, cur):
        c.wait()
    ck_refs = [kbuf.at[cur, i] for i in range(HG)]
    cv_refs = [vbuf.at[cur, i] for i in range(HG)]
    lam = _lam_value(lamp_ref)
    lane = lax.broadcasted_iota(jnp.int32, (DEC_S, A_VD), 1)
    nt = (((1,), (1,)), ((), ()))
    for i in range(HG):
        h = hg * HG + i
        cols = slice(i * A_VD, (i + 1) * A_VD)
        q = q_ref[:, cols] * (A_DH ** -0.5)
        qp = jnp.concatenate([jnp.where(lane < A_DH, q, 0.0), jnp.where(lane >= A_DH, q, 0.0)],
                             axis=0).astype(BF16)
        far = jnp.zeros((DEC_S, PAST - NEAR), F32) + table_ref[N_BUCKETS // 2 - 1, h]
        bl = jnp.concatenate([far, _bias_from_buckets(bl_ref[...], table_ref, h)], axis=1)
        bn = _bias_from_buckets(bn_ref[...], table_ref, h)
        s = (lax.dot_general(qp, ck_refs[i][...].astype(BF16), nt, preferred_element_type=F32)
             + jnp.concatenate([bl, bl], axis=0))
        sn = (lax.dot_general(qp, kn_ref[:, cols].astype(BF16), nt, preferred_element_type=F32)
              + jnp.concatenate([bn, bn], axis=0))
        m = jnp.maximum(jnp.max(s, axis=-1, keepdims=True), jnp.max(sn, axis=-1, keepdims=True))
        p = jnp.exp(s - m)
        pn = jnp.exp(sn - m)
        l = jnp.sum(p, axis=-1, keepdims=True) + jnp.sum(pn, axis=-1, keepdims=True)
        acc = (jnp.dot(p.astype(BF16), cv_refs[i][...].astype(BF16), preferred_element_type=F32)
               + jnp.dot(pn.astype(BF16), vn_ref[:, cols].astype(BF16), preferred_element_type=F32))
        on = acc / l
        o = on[:DEC_S] - lam * on[DEC_S:]
        ms = jnp.mean(o * o, axis=-1, keepdims=True)
        o_ref[:, cols] = (o * lax.rsqrt(ms + RMS_EPS)) * subg_ref[...] * (1.0 - LAM_INIT)


def _attn_sample(p_all, cache_k, cache_v, table, bkt_last, bkt_new, lamp, subg):
    r0 = ROWS_P // DEC_S
    w = HG * A_VD
    new = lambda off: pl.BlockSpec((DEC_S, w), lambda b, g: (r0 + b, off // w + g))
    return pl.pallas_call(
        _attn_sample_kernel,
        grid=(DEC_B, A_HEADS // HG),
        in_specs=[pl.BlockSpec(memory_space=pltpu.SMEM), new(OFF_Q), new(OFF_K), new(OFF_V),
                  pl.BlockSpec((DEC_S, NEAR), lambda b, g: (0, 0)),
                  pl.BlockSpec((DEC_S, DEC_S), lambda b, g: (0, 0)),
                  pl.BlockSpec((4, A_DH), lambda b, g: (0, 0)),
                  pl.BlockSpec((1, A_VD), lambda b, g: (0, 0)),
                  pl.BlockSpec(memory_space=pl.ANY), pl.BlockSpec(memory_space=pl.ANY)],
        out_specs=pl.BlockSpec((DEC_S, w), lambda b, g: (b, g)),
        out_shape=jax.ShapeDtypeStruct((ROWS_S, D), F32),
        scratch_shapes=[pltpu.VMEM((2, HG, PAST, A_VD), F32), pltpu.VMEM((2, HG, PAST, A_VD), F32),
                        pltpu.SemaphoreType.DMA((2, 2 * HG))],
        compiler_params=_cparams(("arbitrary", "arbitrary")),
        name="attn_sample",
    )(table, p_all, p_all, p_all, bkt_last, bkt_new, lamp, subg, cache_k, cache_v)


def _rwkv_prep_kernel(chain_out, r_ref, k_ref, v_ref, lo_ref, first_ref, mu_ref, mul_ref, dbase_ref, abase_ref,
                      wd_ref, wa_ref, wg_ref,
                      xr_ref, xk_ref, xv_ref, wp_ref, ap_ref, g_ref, last_ref, carry_scr, carryl_scr):
    i = pl.program_id(1)
    tm = r_ref.shape[0]
    row = lax.broadcasted_iota(jnp.int32, (tm, 1), 0)

    def shift(x, first, mu):
        prev = jnp.where(row == 0, first, pltpu.roll(x, 1, axis=0))
        return x + (prev - x) * mu

    def put(dst, val):
        if chain_out:
            for hp in range(B_HEADS // 2):
                t = val[:, hp * 128:(hp + 1) * 128].T
                dst[2 * hp] = t[:B_HEAD]
                dst[2 * hp + 1] = t[B_HEAD:]
        else:
            dst[...] = val

    for s, (src, dst) in enumerate(((r_ref, xr_ref), (k_ref, xk_ref), (v_ref, xv_ref))):
        x = src[...]
        first = jnp.where(i == 0, first_ref[:, s * D:(s + 1) * D], carry_scr[:, s * D:(s + 1) * D])
        put(dst, shift(x, first, mu_ref[:, s * D:(s + 1) * D]))
        carry_scr[:, s * D:(s + 1) * D] = x[tm - 1:tm, :]
        last_ref[:, s * D:(s + 1) * D] = x[tm - 1:tm, :]

    xl = lo_ref[...]
    first = jnp.where(i == 0, first_ref[:, 3 * D:], carryl_scr[...])
    xs = shift(xl, first, mul_ref[...])
    carryl_scr[...] = xl[tm - 1:tm, :]
    last_ref[:, 3 * D:] = xl[tm - 1:tm, :]
    put(wp_ref, dbase_ref[...] + jnp.dot(jnp.tanh(xs).astype(BF16), wd_ref[...], preferred_element_type=F32))
    put(ap_ref, abase_ref[...] + jnp.dot(xs.astype(BF16), wa_ref[...], preferred_element_type=F32))
    g_ref[...] = jnp.dot(_sigmoid(xs).astype(BF16), wg_ref[...], preferred_element_type=F32)


def _rwkv_prep(p_all, first, mu, mul, dbase, abase, wd, wa, wg, nseq, nblk, tm, row_blk0, chain_out):
    rows = nseq * nblk * tm
    cs = lambda off: (lambda s, i: (row_blk0 + s * nblk + i, off))
    full = lambda shp: pl.BlockSpec(shp, lambda s, i: tuple(0 for _ in shp))
    out = jax.ShapeDtypeStruct((rows, D), F32)
    ospec = pl.BlockSpec((tm, D), lambda s, i: (s * nblk + i, 0))
    if chain_out:
        cout = jax.ShapeDtypeStruct((nseq * B_HEADS, B_HEAD, nblk * tm), F32)
        cspec = pl.BlockSpec((B_HEADS, B_HEAD, tm), lambda s, i: (s, 0, i))
    else:
        cout, cspec = out, ospec
    return pl.pallas_call(
        functools.partial(_rwkv_prep_kernel, chain_out),
        grid=(nseq, nblk),
        in_specs=[pl.BlockSpec((tm, D), cs(OFF_R // D)),
                  pl.BlockSpec((tm, D), cs(OFF_KX // D)),
                  pl.BlockSpec((tm, D), cs(OFF_VX // D)),
                  pl.BlockSpec((tm, LORA_P), cs(OFF_LORA // LORA_P)),
                  pl.BlockSpec((None, 1, 3 * D + LORA_P), lambda s, i: (s, 0, 0)),
                  full((1, 3 * D)), full((1, LORA_P)), full((1, D)), full((1, D)),
                  full((LORA_P, D)), full((LORA_P, D)), full((LORA_P, D))],
        out_specs=[cspec] * 5 + [ospec, pl.BlockSpec((None, 1, 3 * D + LORA_P), lambda s, i: (s, 0, 0))],
        out_shape=[cout] * 5 + [out, jax.ShapeDtypeStruct((nseq, 1, 3 * D + LORA_P), F32)],
        scratch_shapes=[pltpu.VMEM((1, 3 * D), F32), pltpu.VMEM((1, LORA_P), F32)],
        compiler_params=_cparams(("arbitrary", "arbitrary")),
        name="rwkv_prep",
    )(p_all, p_all, p_all, p_all, first, mu, mul, dbase, abase, wd, wa, wg)


TB = 32
IH = 32


def _rwkv_rec_kernel(skip, *refs):
    tb = pl.program_id(1)
    s0_ref, o_ref, sfin_ref, S = refs[10], refs[11], refs[12], refs[13]

    @pl.when(tb == 0)
    def _():
        S[...] = s0_ref[...]

    if skip:
        @pl.when(tb < skip)
        def _():
            o_ref[...] = jnp.zeros(o_ref.shape, F32)

        pl.when(tb >= skip)(functools.partial(_rwkv_rec_block, *refs))
    else:
        _rwkv_rec_block(*refs)
    sfin_ref[...] = S[...]


def _rwkv_rec_block(xr_ref, xk_ref, xv_ref, wp_ref, ap_ref, kk_ref, ka_ref, rk_ref, gnw_ref, gnb_ref, s0_ref,
                    o_ref, sfin_ref, S, w_s, k_s, a_s, b_s, sa_s, y_s):
    z = -wp_ref[...]
    softplus = jnp.maximum(z, 0.0) + jnp.log(1.0 + jnp.exp(-jnp.abs(z)))
    w_s[...] = jnp.exp(-jnp.exp(-softplus - 0.5))
    a = _sigmoid(ap_ref[...])
    xk = xk_ref[...]
    kk = xk * kk_ref[...]
    nrm = jnp.sqrt(jnp.sum(kk * kk, axis=1, keepdims=True))
    kk = kk / jnp.maximum(nrm, 1e-12)
    k_s[...] = xk * (1.0 + (a - 1.0) * ka_ref[...])
    a_s[...] = -kk
    b_s[...] = kk * a

    for hf in range(B_HEAD // IH):
        i0 = hf * IH
        acc = jnp.zeros((IH, 128), F32)
        for j in range(B_HEAD):
            acc = acc + S[j, i0:i0 + IH, :] * a_s[0, j:j + 1, :]
        sa_s[0, i0:i0 + IH, :] = acc

    def token(t, _):
        cur = t % 2
        tn = jnp.minimum(t + 1, TB - 1)
        for hf in range(B_HEAD // IH):
            i0 = hf * IH
            sa = sa_s[cur, i0:i0 + IH, :]
            v = xv_ref[t, i0:i0 + IH, :]
            y = jnp.zeros((IH, 128), F32)
            san = jnp.zeros((IH, 128), F32)
            for j in range(B_HEAD):
                s_new = (S[j, i0:i0 + IH, :] * w_s[t, j:j + 1, :] + sa * b_s[t, j:j + 1, :]
                         + v * k_s[t, j:j + 1, :])
                S[j, i0:i0 + IH, :] = s_new
                y = y + s_new * xr_ref[t, j:j + 1, :]
                san = san + s_new * a_s[tn, j:j + 1, :]
            y_s[i0:i0 + IH, :] = y
            sa_s[1 - cur, i0:i0 + IH, :] = san
        y = y_s[...]
        mu = jnp.mean(y, axis=0, keepdims=True)
        yc = y - mu
        var = jnp.mean(yc * yc, axis=0, keepdims=True)
        bonus = jnp.sum(xr_ref[t] * k_s[t] * rk_ref[...], axis=0, keepdims=True)
        o_ref[t] = yc * lax.rsqrt(var + GN_EPS) * gnw_ref[...] + gnb_ref[...] + bonus * xv_ref[t]
        return 0

    lax.fori_loop(0, TB, token, 0)


def _rwkv_rec(xr, xk, xv, wp, ap, kk, ka, rk, gnw, gnb, s0, skip=0):
    t, _, c = xr.shape
    tok = pl.BlockSpec((TB, B_HEAD, 128), lambda g, tb: (tb, 0, g))
    par = pl.BlockSpec((B_HEAD, 128), lambda g, tb: (0, 0))
    st = pl.BlockSpec((B_HEAD, B_HEAD, 128), lambda g, tb: (0, 0, g))
    blk = pltpu.VMEM((TB, B_HEAD, 128), F32)
    return pl.pallas_call(
        functools.partial(_rwkv_rec_kernel, skip),
        grid=(c // 128, t // TB),
        in_specs=[tok] * 5 + [par] * 5 + [st],
        out_specs=[tok, st],
        out_shape=[jax.ShapeDtypeStruct((t, B_HEAD, c), F32), jax.ShapeDtypeStruct((B_HEAD, B_HEAD, c), F32)],
        scratch_shapes=[pltpu.VMEM((B_HEAD, B_HEAD, 128), F32), blk, blk, blk, blk,
                        pltpu.VMEM((2, B_HEAD, 128), F32), pltpu.VMEM((B_HEAD, 128), F32)],
        compiler_params=_cparams(("arbitrary", "arbitrary")),
        name="rwkv_rec",
    )(xr, xk, xv, wp, ap, kk, ka, rk, gnw, gnb, s0)


def _to_chains(a, nb, t):
    return a.reshape(nb, t, B_HEADS, B_HEAD).transpose(1, 3, 0, 2).reshape(t, B_HEAD, nb * B_HEADS)


def _from_chains(a, nb, t):
    return a.reshape(t, B_HEAD, nb, B_HEADS).transpose(2, 0, 3, 1).reshape(nb * t, D)


TM5 = 256


def _mix_kernel(yo_p_ref, yo_s_ref, g_p_ref, g_s_ref, oa_p_ref, oa_s_ref, ga_ref, gb_ref, x_ref,
                wo_ref, n2_ref, wr_ref, br_ref, ltri_ref, x1_ref, h2_ref, te_ref, tg_ref, rank_ref, cnt_ref, cnt_scr):
    i = pl.program_id(0)
    is_p = i < ROWS_P // TM5
    tiles = [jnp.concatenate([yo_p_ref[2 * hp], yo_p_ref[2 * hp + 1]], axis=0).T for hp in range(B_HEADS // 2)]
    yo = jnp.where(is_p, jnp.concatenate(tiles, axis=1), yo_s_ref[...])
    g = jnp.where(is_p, g_p_ref[...], g_s_ref[...])
    oa = jnp.where(is_p, oa_p_ref[...], oa_s_ref[...])
    mixed = _sigmoid(ga_ref[...]) * oa + _sigmoid(gb_ref[...]) * (yo * g)
    x1 = x_ref[...] + jnp.dot(mixed.astype(BF16), wo_ref[...], preferred_element_type=F32)
    x1_ref[...] = x1
    ms = jnp.mean(x1 * x1, axis=-1, keepdims=True)
    h2 = (x1 * lax.rsqrt(ms + RMS_EPS)) * n2_ref[...]
    h2_ref[...] = h2
    logits = jnp.dot(h2, wr_ref[...], preferred_element_type=F32, precision=lax.Precision.HIGHEST) + br_ref[...]
    lane = lax.broadcasted_iota(jnp.int32, logits.shape, 1)
    lane_f = lane.astype(F32)
    te = jnp.zeros(logits.shape, jnp.int32)
    tv = jnp.full(logits.shape, NEG, F32)
    chosen = []
    for k in range(TOP_K):
        m = jnp.max(logits, axis=-1, keepdims=True)
        idx = jnp.min(jnp.where(logits == m, lane_f, 128.0), axis=-1, keepdims=True).astype(jnp.int32)
        te = jnp.where(lane == k, idx, te)
        tv = jnp.where(lane == k, m, tv)
        chosen.append(lane == idx)
        logits = jnp.where(chosen[k], NEG, logits)
    e = jnp.where(lane < TOP_K, jnp.exp(tv - jnp.max(tv, axis=-1, keepdims=True)), 0.0)
    te_ref[...] = te
    tg_ref[...] = e / jnp.sum(e, axis=-1, keepdims=True)

    @pl.when(i == 0)
    def _():
        cnt_scr[...] = jnp.zeros(cnt_scr.shape, F32)

    first_row = jnp.where(is_p, (i % (T_PAD // TM5)) * TM5, FRONT)
    routed = (lax.broadcasted_iota(jnp.int32, (TM5, 1), 0) + first_row) >= FRONT
    hits = jnp.zeros(logits.shape, F32)
    for k in range(TOP_K):
        hits = hits + jnp.where(chosen[k], 1.0, 0.0)
    hits = jnp.where(routed, hits, 0.0)
    before = cnt_scr[...] + jnp.dot(ltri_ref[...], hits.astype(BF16), preferred_element_type=F32)
    rank = jnp.zeros(logits.shape, jnp.int32)
    for k in range(TOP_K):
        rk = jnp.sum(jnp.where(chosen[k], before, 0.0), axis=-1, keepdims=True).astype(jnp.int32)
        rank = jnp.where(lane == k, rk, rank)
    rank_ref[...] = rank
    cnt_scr[...] = cnt_scr[...] + jnp.sum(hits, axis=0, keepdims=True)
    cnt_ref[...] = jnp.broadcast_to(cnt_scr[...], cnt_ref.shape)


def _mix(yo_p, yo_s, g_p, g_s, oa_p, oa_s, p_all, x_all, wo, n2, wr, br):
    nbp = ROWS_P // TM5
    rb = lambda i: (i, 0)
    pb = pl.BlockSpec((TM5, D), lambda i: (jnp.minimum(i, nbp - 1), 0))
    sb = pl.BlockSpec((TM5, D), lambda i: (jnp.maximum(i - nbp, 0), 0))
    full = lambda shp: pl.BlockSpec(shp, lambda i: tuple(0 for _ in shp))
    return pl.pallas_call(
        _mix_kernel,
        grid=(ROWS // TM5,),
        in_specs=[pl.BlockSpec((B_HEADS, B_HEAD, TM5),
                               lambda i: (jnp.minimum(i, nbp - 1) // (T_PAD // TM5), 0,
                                          jnp.minimum(i, nbp - 1) % (T_PAD // TM5))),
                  sb, pb, sb, pb, sb,
                  pl.BlockSpec((TM5, D), lambda i: (i, OFF_GA // D)),
                  pl.BlockSpec((TM5, D), lambda i: (i, OFF_GB // D)),
                  pl.BlockSpec((TM5, D), rb),
                  full((D, D)), full((1, D)), full((D, 128)), full((1, 128)), full((TM5, TM5))],
        out_specs=[pl.BlockSpec((TM5, D), rb), pl.BlockSpec((TM5, D), rb),
                   pl.BlockSpec((TM5, 128), rb), pl.BlockSpec((TM5, 128), rb),
                   pl.BlockSpec((TM5, 128), rb), full((8, 128))],
        out_shape=[jax.ShapeDtypeStruct((ROWS, D), F32), jax.ShapeDtypeStruct((ROWS, D), F32),
                   jax.ShapeDtypeStruct((ROWS, 128), jnp.int32), jax.ShapeDtypeStruct((ROWS, 128), F32),
                   jax.ShapeDtypeStruct((ROWS, 128), jnp.int32), jax.ShapeDtypeStruct((8, 128), F32)],
        scratch_shapes=[pltpu.VMEM((1, 128), F32)],
        compiler_params=_cparams(("arbitrary",)),
        name="mix",
    )(yo_p, yo_s, g_p, g_s, oa_p, oa_s, p_all, p_all, x_all, wo, n2, wr, br,
      jnp.asarray(np.tril(np.ones((TM5, TM5), np.float32), -1), BF16))


N_TOK = NB * (N_META + SEQ) + ROWS_S
NK = N_TOK * TOP_K
TMM = 256
N_BLK = -(-NK // TMM) + N_EXPERTS
NSLOT = N_BLK * TMM
TN_G1 = 2048
TN_G2 = 2048
TMC = 128


def _gather_kernel(slot_ref, nused_ref, h_hbm, o_ref, buf, sem):
    m = pl.program_id(0)
    cur = m % 2

    def issue(blk, slot):
        def row(r, _):
            tok = slot_ref[blk * TMM + r]
            pltpu.make_async_copy(h_hbm.at[pl.ds(tok, 1), :], buf.at[slot, pl.ds(r, 1), :], sem.at[slot]).start()
            return 0

        lax.fori_loop(0, TMM, row, 0, unroll=8)

    @pl.when((m == 0) & (m < nused_ref[0]))
    def _():
        issue(m, 0)

    @pl.when(m + 1 < nused_ref[0])
    def _():
        issue(m + 1, 1 - cur)

    @pl.when(m < nused_ref[0])
    def _():
        pltpu.make_async_copy(h_hbm.at[pl.ds(0, TMM), :], buf.at[cur], sem.at[cur]).wait()
        o_ref[...] = buf[cur].astype(BF16)

    @pl.when(m >= nused_ref[0])
    def _():
        o_ref[...] = jnp.zeros(o_ref.shape, BF16)


def _gather(slot_tok, nused, h2):
    return pl.pallas_call(
        _gather_kernel,
        grid_spec=pltpu.PrefetchScalarGridSpec(
            num_scalar_prefetch=2,
            grid=(N_BLK,),
            in_specs=[pl.BlockSpec(memory_space=pl.ANY)],
            out_specs=pl.BlockSpec((TMM, D), lambda m, st, nu: (m, 0)),
            scratch_shapes=[pltpu.VMEM((2, TMM, D), F32), pltpu.SemaphoreType.DMA((2,))]),
        out_shape=jax.ShapeDtypeStruct((NSLOT, D), BF16),
        compiler_params=_cparams(("arbitrary",)),
        name="moe_gather",
    )(slot_tok, nused, h2)


def _load_expert_weights(sched, w_hbm, wstage, wsem, wb_scr, tn):
    blk_e_ref, first_ref, runid_ref, nexte_ref, meta_ref = sched
    f, m = pl.program_id(0), pl.program_id(1)
    n_runs = meta_ref[1]

    def fetch(e, ff, slot):
        cols = pl.ds(pl.multiple_of(ff * tn, tn), tn)
        return pltpu.make_async_copy(w_hbm.at[e, :, cols], wstage.at[slot], wsem.at[slot])

    @pl.when((m < meta_ref[0]) & (first_ref[m] == 1))
    def _():
        r = runid_ref[m]
        g = f * n_runs + r
        slot = g % 2

        @pl.when(g == 0)
        def _():
            fetch(blk_e_ref[m], f, 0).start()

        nf = f + jnp.where(r == n_runs - 1, 1, 0)

        @pl.when(nf < pl.num_programs(0))
        def _():
            fetch(nexte_ref[m], nf, 1 - slot).start()

        fetch(blk_e_ref[m], f, slot).wait()
        wb_scr[...] = wstage[slot].astype(BF16)


def _gm1_kernel(blk_e_ref, first_ref, runid_ref, nexte_ref, meta_ref, x_ref, w_hbm, b_ref, sel_ref, o_ref,
                wb_scr, wstage, wsem):
    m = pl.program_id(1)
    used = m < meta_ref[0]
    _load_expert_weights((blk_e_ref, first_ref, runid_ref, nexte_ref, meta_ref), w_hbm, wstage, wsem, wb_scr, TN_G1)

    @pl.when(used)
    def _():
        hd = jnp.dot(x_ref[...], wb_scr[...], preferred_element_type=F32) + b_ref[...]
        glu = jnp.minimum(hd, SWIGLU_LIMIT)
        lin = jnp.clip(hd, -SWIGLU_LIMIT, SWIGLU_LIMIT) + 1.0
        act = glu * _sigmoid(SWIGLU_ALPHA * glu)
        for c in range(TN_G1 // 512):
            parts = []
            for q in range(4):
                sl = slice(c * 512 + q * 128, c * 512 + (q + 1) * 128)
                parts.append(act[:, sl] * pltpu.roll(lin[:, sl], 127, axis=1))
            z = jnp.concatenate(parts, axis=1).astype(BF16)
            o_ref[:, c * 256:(c + 1) * 256] = jnp.dot(z, sel_ref[...], preferred_element_type=F32).astype(BF16)

    @pl.when(jnp.logical_not(used))
    def _():
        o_ref[...] = jnp.zeros(o_ref.shape, BF16)


def _meff(m, meta):
    return jnp.maximum(jnp.minimum(m, meta[0] - 1), 0)


def _gm1(sched, xs, w1, b1, sel):
    return pl.pallas_call(
        _gm1_kernel,
        grid_spec=pltpu.PrefetchScalarGridSpec(
            num_scalar_prefetch=5,
            grid=(2 * D_FF // TN_G1, N_BLK),
            in_specs=[pl.BlockSpec((TMM, D), lambda f, m, be, fi, ri, ne, mt: (_meff(m, mt), 0)),
                      pl.BlockSpec(memory_space=pl.ANY),
                      pl.BlockSpec((None, 1, TN_G1), lambda f, m, be, fi, ri, ne, mt: (be[_meff(m, mt)], 0, f)),
                      pl.BlockSpec((512, 256), lambda f, m, be, fi, ri, ne, mt: (0, 0))],
            out_specs=pl.BlockSpec((TMM, TN_G1 // 2), lambda f, m, be, fi, ri, ne, mt: (m, f)),
            scratch_shapes=[pltpu.VMEM((D, TN_G1), BF16), pltpu.VMEM((2, D, TN_G1), F32),
                            pltpu.SemaphoreType.DMA((2,))]),
        out_shape=jax.ShapeDtypeStruct((NSLOT, D_FF), BF16),
        compiler_params=_cparams(("arbitrary", "arbitrary")),
        name="moe_up",
    )(*sched, xs, w1, b1, sel)


def _gm2_kernel(blk_e_ref, first_ref, runid_ref, nexte_ref, meta_ref, x_ref, w_hbm, b_ref, o_ref,
                wb_scr, wstage, wsem):
    m = pl.program_id(1)
    used = m < meta_ref[0]
    _load_expert_weights((blk_e_ref, first_ref, runid_ref, nexte_ref, meta_ref), w_hbm, wstage, wsem, wb_scr, TN_G2)

    @pl.when(used)
    def _():
        o_ref[...] = jnp.dot(x_ref[...], wb_scr[...], preferred_element_type=F32) + b_ref[...]

    @pl.when(jnp.logical_not(used))
    def _():
        o_ref[...] = jnp.zeros(o_ref.shape, F32)


def _gm2(sched, hid, w2, b2):
    return pl.pallas_call(
        _gm2_kernel,
        grid_spec=pltpu.PrefetchScalarGridSpec(
            num_scalar_prefetch=5,
            grid=(D // TN_G2, N_BLK),
            in_specs=[pl.BlockSpec((TMM, D_FF), lambda f, m, be, fi, ri, ne, mt: (_meff(m, mt), 0)),
                      pl.BlockSpec(memory_space=pl.ANY),
                      pl.BlockSpec((None, 1, TN_G2), lambda f, m, be, fi, ri, ne, mt: (be[_meff(m, mt)], 0, f))],
            out_specs=pl.BlockSpec((TMM, TN_G2), lambda f, m, be, fi, ri, ne, mt: (m, f)),
            scratch_shapes=[pltpu.VMEM((D_FF, TN_G2), BF16), pltpu.VMEM((2, D_FF, TN_G2), F32),
                            pltpu.SemaphoreType.DMA((2,))]),
        out_shape=jax.ShapeDtypeStruct((NSLOT, D), F32),
        compiler_params=_cparams(("arbitrary", "arbitrary")),
        name="moe_down",
    )(*sched, hid, w2, b2)


def _combine_kernel(dest_ref, x1_ref, tg_ref, fg_ref, y_hbm, yp_ref, ys_ref, buf, sem):
    m = pl.program_id(0)
    cur = m % 2

    def issue(blk, slot):
        def row(r, _):
            for k in range(TOP_K):
                d = dest_ref[(blk * TMC + r) * TOP_K + k]
                pltpu.make_async_copy(y_hbm.at[pl.ds(d, 1), :], buf.at[slot, pl.ds(k * TMC + r, 1), :],
                                      sem.at[slot]).start()
            return 0

        lax.fori_loop(0, TMC, row, 0, unroll=4)

    @pl.when(m == 0)
    def _():
        issue(m, 0)

    @pl.when(m + 1 < pl.num_programs(0))
    def _():
        issue(m + 1, 1 - cur)

    pltpu.make_async_copy(y_hbm.at[pl.ds(0, TOP_K * TMC), :], buf.at[cur], sem.at[cur]).wait()
    x2 = x1_ref[...]
    tg = tg_ref[...]
    for k in range(TOP_K):
        x2 = x2 + tg[:, k:k + 1] * buf[cur, k * TMC:(k + 1) * TMC, :]
    ms = jnp.mean(x2 * x2, axis=-1, keepdims=True)
    y = (x2 * lax.rsqrt(ms + RMS_EPS)) * fg_ref[...]

    @pl.when((m < ROWS_P // TMC) & (m % (T_PAD // TMC) >= CB_SKIP))
    def _():
        yp_ref[...] = y

    @pl.when(m >= ROWS_P // TMC)
    def _():
        ys_ref[...] = y


CB_SKIP = (FRONT + N_META) // TMC


def _combine(dest, x1, tg, fg, y):
    per_b = T_PAD // TMC

    def prompt_blk(m, d):
        mp = jnp.minimum(m, ROWS_P // TMC - 1)
        return ((mp // per_b) * (SEQ // TMC) + jnp.maximum(mp % per_b - CB_SKIP, 0), 0)

    return pl.pallas_call(
        _combine_kernel,
        grid_spec=pltpu.PrefetchScalarGridSpec(
            num_scalar_prefetch=1,
            grid=(ROWS // TMC,),
            in_specs=[pl.BlockSpec((TMC, D), lambda m, d: (m, 0)),
                      pl.BlockSpec((TMC, 128), lambda m, d: (m, 0)),
                      pl.BlockSpec((1, D), lambda m, d: (0, 0)),
                      pl.BlockSpec(memory_space=pl.ANY)],
            out_specs=[pl.BlockSpec((TMC, D), prompt_blk),
                       pl.BlockSpec((TMC, D), lambda m, d: (jnp.maximum(m - ROWS_P // TMC, 0), 0))],
            scratch_shapes=[pltpu.VMEM((2, TOP_K * TMC, D), F32), pltpu.SemaphoreType.DMA((2,))]),
        out_shape=[jax.ShapeDtypeStruct((NB * SEQ, D), F32), jax.ShapeDtypeStruct((ROWS_S, D), F32)],
        compiler_params=_cparams(("arbitrary",)),
        name="moe_combine",
    )(dest, x1, tg, fg, y)


def _t5_bucket(rel):
    nb = N_BUCKETS // 2
    ret = jnp.where(rel > 0, nb, 0)
    n = jnp.abs(rel)
    max_exact = nb // 2
    nf = jnp.maximum(n, 1).astype(jnp.float32)
    large = max_exact + (jnp.log(nf / max_exact) / math.log(MAX_DISTANCE / max_exact) * (nb - max_exact)).astype(jnp.int32)
    large = jnp.minimum(large, nb - 1)
    return ret + jnp.where(n < max_exact, n, large)


def _valid_rows():
    rows = [np.arange(b * T_PAD + FRONT, (b + 1) * T_PAD) for b in range(NB)]
    rows.append(np.arange(ROWS_P, ROWS))
    return np.concatenate(rows).astype(np.int32)


TR_KV = math.gcd(FRONT, N_META + SEQ)


def _kv_out_kernel(k_ref, v_ref, ko_ref, vo_ref):
    for h in range(A_HEADS):
        ko_ref[:, h, :] = k_ref[:, h * A_VD:(h + 1) * A_VD]
        vo_ref[:, h, :] = v_ref[:, h * A_VD:(h + 1) * A_VD]


def _kv_out(p_all):
    per_b = T_PAD // TR_KV
    src = lambda off: pl.BlockSpec((TR_KV, D), lambda b, i: (b * per_b + FRONT // TR_KV + i, off // D))
    dst = pl.BlockSpec((None, TR_KV, A_HEADS, A_VD), lambda b, i: (b, i, 0, 0))
    out = jax.ShapeDtypeStruct((NB, N_META + SEQ, A_HEADS, A_VD), F32)
    return pl.pallas_call(
        _kv_out_kernel,
        grid=(NB, (N_META + SEQ) // TR_KV),
        in_specs=[src(OFF_K), src(OFF_V)],
        out_specs=[dst, dst],
        out_shape=[out, out],
        compiler_params=_cparams(("arbitrary", "arbitrary")),
        name="kv_out",
    )(p_all, p_all)


def _slot_tables(te, rank, cnt):
    counts = cnt[0, :N_EXPERTS].astype(jnp.int32)
    padded = (counts + TMM - 1) // TMM * TMM
    pad_end = jnp.cumsum(padded)
    pad_start = pad_end - padded
    experts = jnp.arange(N_EXPERTS, dtype=jnp.int32)
    routed = jnp.asarray(np.isin(np.arange(ROWS), _valid_rows()))[:, None]
    start_of = jnp.sum(jnp.where(te[:, :TOP_K, None] == experts, pad_start, 0), axis=-1)
    dest2 = start_of + rank[:, :TOP_K]
    dest = jnp.where(routed, dest2, 0).reshape(-1)
    rows = jnp.broadcast_to(jnp.arange(ROWS, dtype=jnp.int32)[:, None], (ROWS, TOP_K))
    slot_tok = jnp.zeros((NSLOT,), jnp.int32).at[jnp.where(routed, dest2, NSLOT).reshape(-1)].set(
        rows.reshape(-1), mode='drop')
    blk_start = jnp.arange(N_BLK, dtype=jnp.int32) * TMM
    blk_e = jnp.minimum(jnp.sum(pad_end[None, :] <= blk_start[:, None], axis=-1), N_EXPERTS - 1).astype(jnp.int32)
    nused = (pad_end[-1:] // TMM).astype(jnp.int32)
    blk = jnp.arange(N_BLK, dtype=jnp.int32)
    first = (blk < nused[0]) & ((blk == 0) | (blk_e != jnp.roll(blk_e, 1)))
    run_id = jnp.cumsum(first.astype(jnp.int32)) - 1
    n_runs = jnp.sum(first.astype(jnp.int32))
    run_e = jnp.zeros((N_BLK,), jnp.int32).at[jnp.where(first, run_id, N_BLK)].set(blk_e, mode='drop')
    next_e = run_e[(run_id + 1) % jnp.maximum(n_runs, 1)]
    sched = (blk_e, first.astype(jnp.int32), run_id, next_e, jnp.stack([nused[0], n_runs]))
    return dest, slot_tok, nused, sched


def kernel(x_prompt, x_sample, cache_k, cache_v, state_wkv, state_shift, meta_tokens, rel_bias_table, norm1_g, w_in, shift_mu, decay_base, w_decay_up, a_base, w_a_up, w_g_up, k_k, k_a, r_k, gn_w, gn_b, lam_q1, lam_k1, lam_q2, lam_k2, subln_g, w_out, norm2_g, w_router, b_router, w_e1, b_e1, w_e2, b_e2, final_g):
    first_blk = jnp.concatenate([jnp.zeros((FRONT, D), F32), meta_tokens], axis=0)
    x_all, h_all = _norm(x_prompt, x_sample, first_blk, norm1_g)
    lora0 = OFF_R + 3 * D
    w_main = w_in[0].astype(BF16)
    w_tail = jnp.concatenate([w_main[:, lora0 + LORA_W:], w_main[:, lora0:lora0 + LORA_W],
                              jnp.zeros((D, LORA_P - LORA_W), BF16)], axis=1)

    p_all = _inproj(h_all, w_main, w_tail)

    kq = jnp.arange(TQ, dtype=jnp.int32)
    bkt = jnp.stack([_t5_bucket(kq[:, None] - kq[None, :]), _t5_bucket(kq[:, None] - kq[None, :] - TQ)])
    lamp = jnp.concatenate([lam_q1, lam_k1, lam_q2, lam_k2], axis=0)
    oa_p = _attn_prompt(p_all, rel_bias_table, bkt, lamp, subln_g)
    qpos = PAST + jnp.arange(DEC_S, dtype=jnp.int32)
    bkt_last = _t5_bucket((PAST - NEAR + jnp.arange(NEAR, dtype=jnp.int32))[None, :] - qpos[:, None])
    bkt_new = _t5_bucket(qpos[None, :] - qpos[:, None])
    oa_s = _attn_sample(p_all, cache_k[0], cache_v[0],
                        rel_bias_table, bkt_last, bkt_new, lamp, subln_g)

    mu = shift_mu[0]
    mu3 = mu[:3 * D][None]
    mul = jnp.pad(mu[3 * D:], (0, LORA_P - LORA_W))[None]
    wd = jnp.zeros((LORA_P, D), F32).at[:DECAY_LORA].set(w_decay_up[0]).astype(BF16)
    wa = jnp.zeros((LORA_P, D), F32).at[DECAY_LORA:DECAY_LORA + AAA_LORA].set(w_a_up[0]).astype(BF16)
    wg = jnp.zeros((LORA_P, D), F32).at[DECAY_LORA + AAA_LORA:LORA_W].set(w_g_up[0]).astype(BF16)
    first_p = jnp.zeros((NB, 1, 3 * D + LORA_P), F32)
    first_s = jnp.pad(state_shift[0], ((0, 0), (0, LORA_P - LORA_W)))[:, None, :]
    prep_p = _rwkv_prep(p_all, first_p, mu3, mul, decay_base, a_base, wd, wa, wg, NB, T_PAD // 256, 256, 0, True)
    prep_s = _rwkv_prep(p_all, first_s, mu3, mul, decay_base, a_base, wd, wa, wg, DEC_B, 1, DEC_S, ROWS_P // DEC_S,
                        False)

    def chain_tile(v):
        return jnp.tile(v.reshape(B_HEADS, B_HEAD).T, (1, 128 // B_HEADS))

    par = [chain_tile(v.reshape(-1)) for v in (k_k[0], k_a[0], r_k[0], gn_w[0], gn_b[0])]
    s0_p = jnp.zeros((B_HEAD, B_HEAD, NB * B_HEADS), F32)
    s0_s = state_wkv[0].transpose(3, 2, 0, 1).reshape(B_HEAD, B_HEAD, DEC_B * B_HEADS)
    yo_p, sfin_p = _rwkv_rec(*[a.transpose(2, 1, 0) for a in prep_p[:5]], *par, s0_p, skip=FRONT // TB)
    yo_s, sfin_s = _rwkv_rec(*[_to_chains(a, DEC_B, DEC_S) for a in prep_s[:5]], *par, s0_s)

    wo = w_out[0].astype(BF16)
    wr = jnp.pad(w_router[0], ((0, 0), (0, 128 - N_EXPERTS)))
    br = jnp.concatenate([b_router[0], jnp.full((128 - N_EXPERTS,), NEG, F32)])[None]
    x1, h2, te, tg, rank, cnt = _mix(yo_p.transpose(2, 1, 0), _from_chains(yo_s, DEC_B, DEC_S), prep_p[5], prep_s[5],
                                     oa_p, oa_s, p_all, x_all, wo, norm2_g, wr, br)

    dest, slot_tok, nused, sched = _slot_tables(te, rank, cnt)

    sel_np = np.zeros((512, 256), np.float32)
    sel_np[2 * np.arange(256), np.arange(256)] = 1.0
    xs = _gather(slot_tok, nused, h2)
    hid = _gm1(sched, xs, w_e1[0], b_e1[0][:, None, :], jnp.asarray(sel_np, BF16))
    ys = _gm2(sched, hid, w_e2[0], b_e2[0][:, None, :])
    y_prompt, y_sample = _combine(dest, x1, tg, final_g[None], ys)

    y_prompt = y_prompt.reshape(NB, SEQ, D)
    y_sample = y_sample.reshape(DEC_B, DEC_S, D)
    def last_ps(last):
        return last[:, 0, :SHIFT_W][None]

    kp, vp = _kv_out(p_all)
    k_prompt, v_prompt = kp[None], vp[None]
    k_sample = p_all[ROWS_P:, OFF_K:OFF_K + D].reshape(1, DEC_B, DEC_S, A_HEADS, A_VD)
    v_sample = p_all[ROWS_P:, OFF_V:OFF_V + D].reshape(1, DEC_B, DEC_S, A_HEADS, A_VD)
    shift_prompt = last_ps(prep_p[6])
    shift_sample = last_ps(prep_s[6])
    wkv_prompt = sfin_p.reshape(B_HEAD, B_HEAD, NB, B_HEADS).transpose(2, 3, 1, 0)[None]
    wkv_sample = sfin_s.reshape(B_HEAD, B_HEAD, DEC_B, B_HEADS).transpose(2, 3, 1, 0)[None].astype(state_wkv.dtype)
    return (y_prompt, y_sample, k_prompt, v_prompt, wkv_prompt, shift_prompt,
            k_sample, v_sample, wkv_sample, shift_sample)
```
